```python
import math
import jax, jax.numpy as jnp
from jax import lax
import numpy as np

D_MODEL = 4096
BATCH = 2
SEQ = 4096
DEPTH = 2

HEAD_DIM = 64
D_MIX = D_MODEL
RWKV_WIDTH = D_MIX // 2
RWKV_HEADS = RWKV_WIDTH // HEAD_DIM
ATTN_WIDTH = D_MIX - RWKV_WIDTH
N_Q_HEADS = ATTN_WIDTH // HEAD_DIM
N_KV_HEADS = max(1, N_Q_HEADS // 8)
Q_PER_KV = N_Q_HEADS // N_KV_HEADS
KV_WIDTH = N_KV_HEADS * HEAD_DIM
DECAY_LORA = max(32, round(1.8 * RWKV_WIDTH ** 0.5 / 32) * 32)
AAA_LORA = max(32, round(1.8 * RWKV_WIDTH ** 0.5 / 32) * 32)
GATE_LORA = max(32, round(0.6 * RWKV_WIDTH ** 0.8 / 32) * 32)
RWKV_COLS = 3 * RWKV_WIDTH + DECAY_LORA + AAA_LORA + GATE_LORA
ATTN_COLS = ATTN_WIDTH + 2 * KV_WIDTH
IN_COLS = RWKV_COLS + ATTN_COLS
WINDOW = 128
BLOCK = 128
N_BUCKETS = 32
MAX_DISTANCE = 128
N_EXPERTS = 64
N_EXPERT_GROUPS = 8
TOPK_GROUPS = 4
TOP_K = 8
EXPERT_FF = 3 * D_MODEL // 32
SHARED_FF = EXPERT_FF
ROUTED_SCALE = 2.5
MOE_BLOCK = 128
RMS_EPS = 1e-6
GN_EPS = 64e-5

kernel_name = "hybrid_rwkv7_swa_sink_moe_block"


def rmsnorm(x, gain):
    xf = x.astype(jnp.float32)
    y = xf * lax.rsqrt(jnp.mean(xf * xf, axis=-1, keepdims=True) + RMS_EPS)
    return (y * gain.astype(jnp.float32)).astype(x.dtype)


def modulate(x, shift, scale):
    return x * (1 + scale[:, None, :]) + shift[:, None, :]


def token_shift(t):
    return jnp.pad(t[:, :-1], ((0, 0), (1, 0), (0, 0)))


def t5_bucket(dist):
    max_exact = N_BUCKETS // 2
    d = jnp.maximum(dist, 0)
    far = max_exact + (jnp.log(jnp.maximum(d, max_exact).astype(jnp.float32) / max_exact)
                       / math.log(MAX_DISTANCE / max_exact) * (N_BUCKETS - max_exact)).astype(jnp.int32)
    return jnp.where(d < max_exact, d, jnp.minimum(far, N_BUCKETS - 1))


def _rwkv7_step(state, inp):
    r, w, k, v, neg_kk, kk_a = inp
    sa = jnp.einsum('bhvk,bhk->bhv', state, neg_kk)
    state = state * w[:, :, None, :] + sa[..., None] * kk_a[:, :, None, :] + v[..., None] * k[:, :, None, :]
    return state, jnp.einsum('bhvk,bhk->bhv', state, r)


def rwkv7_time_mix(p, mu, w0, w2, a0, a2, g2, k_k, k_a, r_k, ln_w, ln_b):
    B, S, _ = p.shape
    p = p + (token_shift(p) - p) * mu
    splits = [RWKV_WIDTH, 2 * RWKV_WIDTH, 3 * RWKV_WIDTH,
              3 * RWKV_WIDTH + DECAY_LORA, 3 * RWKV_WIDTH + DECAY_LORA + AAA_LORA]
    r, k, v, xw, xa, xg = jnp.split(p, splits, axis=-1)
    w_log = -jax.nn.softplus(-(w0 + jnp.tanh(xw) @ w2)) - 0.5
    a = jax.nn.sigmoid(a0 + xa @ a2)
    g = jax.nn.sigmoid(xg) @ g2

    def heads(t):
        return t.reshape(B, S, RWKV_HEADS, HEAD_DIM).astype(jnp.float32)

    kk = heads(k * k_k)
    kk = kk / jnp.maximum(jnp.sqrt(jnp.sum(kk * kk, axis=-1, keepdims=True)), 1e-12)
    k = k * (1 + (a - 1) * k_a)
    r_h, k_h, v_h, a_h = heads(r), heads(k), heads(v), heads(a)
    decay = jnp.exp(-jnp.exp(heads(w_log)))
    xs = tuple(jnp.moveaxis(t, 1, 0) for t in (r_h, decay, k_h, v_h, -kk, kk * a_h))
    s0 = jnp.zeros((B, RWKV_HEADS, HEAD_DIM, HEAD_DIM), jnp.float32)
    _, y = lax.scan(_rwkv7_step, s0, xs)
    y = jnp.moveaxis(y, 0, 1)
    mean = jnp.mean(y, axis=-1, keepdims=True)
    var = jnp.mean(jnp.square(y - mean), axis=-1, keepdims=True)
    y = ((y - mean) * lax.rsqrt(var + GN_EPS)).reshape(B, S, RWKV_WIDTH)
    y = y * ln_w.astype(jnp.float32) + ln_b.astype(jnp.float32)
    bonus = (jnp.sum(r_h * k_h * r_k.astype(jnp.float32), axis=-1, keepdims=True) * v_h).reshape(B, S, RWKV_WIDTH)
    return ((y + bonus) * g).astype(p.dtype)


def sliding_window_attention(p, sinks, rel_bias):
    B, S, _ = p.shape
    nb = S // BLOCK
    q, k, v = jnp.split(p, [ATTN_WIDTH, ATTN_WIDTH + KV_WIDTH], axis=-1)
    q = q.reshape(B, nb, BLOCK, N_KV_HEADS, Q_PER_KV, HEAD_DIM).astype(jnp.float32)

    def windows(t):
        t = t.reshape(B, S, N_KV_HEADS, HEAD_DIM).astype(jnp.float32)
        t = jnp.pad(t, ((0, 0), (BLOCK, 0), (0, 0), (0, 0))).reshape(B, nb + 1, BLOCK, N_KV_HEADS, HEAD_DIM)
        return jnp.concatenate([t[:, :-1], t[:, 1:]], axis=2)

    kw, vw = windows(k), windows(v)
    logits = jnp.einsum('bnqhgd,bnkhd->bnhgqk', q, kw) * (HEAD_DIM ** -0.5)
    qi = jnp.arange(BLOCK)[:, None]
    kj = jnp.arange(2 * BLOCK)[None, :]
    dist = qi + BLOCK - kj
    key_pos = jnp.arange(nb)[:, None, None] * BLOCK - BLOCK + kj[None]
    valid = (dist >= 0) & (dist < WINDOW) & (key_pos >= 0)
    bias = rel_bias.astype(jnp.float32)[t5_bucket(dist)]
    bias = bias.transpose(2, 0, 1).reshape(N_KV_HEADS, Q_PER_KV, BLOCK, 2 * BLOCK)
    logits = jnp.where(valid[None, :, None, None], logits + bias, -jnp.inf)
    sink = sinks.astype(jnp.float32).reshape(N_KV_HEADS, Q_PER_KV)[:, :, None, None]
    m = jnp.maximum(jnp.max(logits, axis=-1, keepdims=True), sink)
    pr = jnp.exp(logits - m)
    denom = jnp.sum(pr, axis=-1, keepdims=True) + jnp.exp(sink - m)
    out = jnp.einsum('bnhgqk,bnkhd->bnqhgd', pr / denom, vw)
    return out.reshape(B, S, ATTN_WIDTH).astype(p.dtype)


def moe_ffn(u, router_w, router_bias, w_gu, w_down, ws_gu, ws_down):
    B, S, D = u.shape
    t = u.reshape(B * S, D)
    T = t.shape[0]
    scores = jax.nn.sigmoid((t @ router_w).astype(jnp.float32))
    sel = scores + router_bias.astype(jnp.float32)
    grp = sel.reshape(T, N_EXPERT_GROUPS, N_EXPERTS // N_EXPERT_GROUPS)
    grp_score = jnp.sum(lax.top_k(grp, 2)[0], axis=-1)
    _, top_g = lax.top_k(grp_score, TOPK_GROUPS)
    gmask = jnp.any(top_g[..., None] == jnp.arange(N_EXPERT_GROUPS), axis=1)
    emask = jnp.repeat(gmask, N_EXPERTS // N_EXPERT_GROUPS, axis=1)
    sel = jnp.where(emask, sel, -jnp.inf)
    _, idx = lax.top_k(sel, TOP_K)
    wts = jnp.take_along_axis(scores, idx, axis=1)
    wts = wts / jnp.sum(wts, axis=-1, keepdims=True) * ROUTED_SCALE
    gates = jnp.einsum('tk,tke->te', wts, jax.nn.one_hot(idx, N_EXPERTS, dtype=jnp.float32))

    def block(args):
        tb, gb = args
        h = jnp.einsum('td,edf->tef', tb, w_gu)
        hg, hu = jnp.split(h, 2, axis=-1)
        act = jax.nn.silu(hg) * hu * gb[..., None].astype(h.dtype)
        return jnp.einsum('tef,efd->td', act, w_down)

    routed = lax.map(block, (t.reshape(-1, MOE_BLOCK, D), gates.reshape(-1, MOE_BLOCK, N_EXPERTS)))
    hs_g, hs_u = jnp.split(t @ ws_gu, 2, axis=-1)
    shared = (jax.nn.silu(hs_g) * hs_u) @ ws_down
    return (routed.reshape(T, D) + shared).reshape(B, S, D).astype(u.dtype)


def setup_inputs(seed: int = 0) -> dict:
    key = jax.random.key(seed)
    ks = jax.random.split(key, 32)

    def nrm(i, shape, scale):
        return scale * jax.random.normal(ks[i], shape, jnp.float32)

    L = DEPTH
    E = N_EXPERTS
    return {
        "x": nrm(0, (BATCH, SEQ, D_MODEL), 1.0),
        "c": nrm(1, (BATCH, D_MODEL), 1.0),
        "rel_bias": nrm(2, (N_BUCKETS, N_Q_HEADS), 0.5),
        "ada_w": nrm(3, (L, D_MODEL, 6 * D_MODEL), 0.5 * D_MODEL ** -0.5),
        "ada_b": nrm(4, (L, 6 * D_MODEL), 0.02),
        "norm_gains": 1.0 + nrm(5, (L, 4, D_MODEL), 0.1),
        "w_in": nrm(6, (L, D_MODEL, IN_COLS), D_MODEL ** -0.5),
        "shift_mu": jax.random.uniform(ks[7], (L, RWKV_COLS), jnp.float32),
        "rwkv_w0": jax.random.uniform(ks[8], (L, RWKV_WIDTH), jnp.float32, -5.0, -1.0),
        "rwkv_w2": nrm(9, (L, DECAY_LORA, RWKV_WIDTH), DECAY_LORA ** -0.5),
        "rwkv_a0": nrm(10, (L, RWKV_WIDTH), 0.5),
        "rwkv_a2": nrm(11, (L, AAA_LORA, RWKV_WIDTH), AAA_LORA ** -0.5),
        "rwkv_g2": nrm(12, (L, GATE_LORA, RWKV_WIDTH), GATE_LORA ** -0.5),
        "rwkv_k_k": 0.85 + nrm(13, (L, RWKV_WIDTH), 0.1),
        "rwkv_k_a": 1.0 + nrm(14, (L, RWKV_WIDTH), 0.1),
        "rwkv_r_k": nrm(15, (L, RWKV_HEADS, HEAD_DIM), 0.1),
        "rwkv_ln_w": 1.0 + nrm(16, (L, RWKV_WIDTH), 0.1),
        "rwkv_ln_b": nrm(17, (L, RWKV_WIDTH), 0.02),
        "attn_sinks": nrm(18, (L, N_Q_HEADS), 1.0),
        "w_out": nrm(19, (L, D_MIX, D_MODEL), D_MIX ** -0.5),
        "router_w": nrm(20, (L, D_MODEL, E), D_MODEL ** -0.5),
        "router_bias": nrm(21, (L, E), 0.01),
        "expert_w_gu": nrm(22, (L, E, D_MODEL, 2 * EXPERT_FF), D_MODEL ** -0.5),
        "expert_w_down": nrm(23, (L, E, EXPERT_FF, D_MODEL), EXPERT_FF ** -0.5),
        "shared_w_gu": nrm(24, (L, D_MODEL, 2 * SHARED_FF), D_MODEL ** -0.5),
        "shared_w_down": nrm(25, (L, SHARED_FF, D_MODEL), SHARED_FF ** -0.5),
    }


def reference(x, c, rel_bias, ada_w, ada_b, norm_gains, w_in, shift_mu, rwkv_w0, rwkv_w2, rwkv_a0,
              rwkv_a2, rwkv_g2, rwkv_k_k, rwkv_k_a, rwkv_r_k, rwkv_ln_w, rwkv_ln_b, attn_sinks, w_out,
              router_w, router_bias, expert_w_gu, expert_w_down, shared_w_gu, shared_w_down):
    cond = jax.nn.silu(c)
    for l in range(DEPTH):
        ada = cond @ ada_w[l] + ada_b[l]
        sh_m, sc_m, gt_m, sh_f, sc_f, gt_f = jnp.split(ada, 6, axis=-1)
        u = modulate(rmsnorm(x, norm_gains[l, 0]), sh_m, sc_m)
        proj = u @ w_in[l]
        y_r = rwkv7_time_mix(proj[..., :RWKV_COLS], shift_mu[l], rwkv_w0[l], rwkv_w2[l], rwkv_a0[l],
                             rwkv_a2[l], rwkv_g2[l], rwkv_k_k[l], rwkv_k_a[l], rwkv_r_k[l],
                             rwkv_ln_w[l], rwkv_ln_b[l])
        y_a = sliding_window_attention(proj[..., RWKV_COLS:], attn_sinks[l], rel_bias)
        y = jnp.concatenate([y_r, y_a], axis=-1) @ w_out[l]
        x = x + gt_m[:, None, :] * rmsnorm(y, norm_gains[l, 1])
        u = modulate(rmsnorm(x, norm_gains[l, 2]), sh_f, sc_f)
        y = moe_ffn(u, router_w[l], router_bias[l], expert_w_gu[l], expert_w_down[l],
                    shared_w_gu[l], shared_w_down[l])
        x = x + gt_f[:, None, :] * rmsnorm(y, norm_gains[l, 3])
    return x
```

```python
import functools
import math

import numpy as np
import jax
import jax.numpy as jnp
from jax import lax
from jax.experimental import pallas as pl
from jax.experimental.pallas import tpu as pltpu

F32 = jnp.float32
BF16 = jnp.bfloat16

HEAD_DIM = 64
LANES = 128
WINDOW = 128
N_BUCKETS = 32
MAX_DISTANCE = 128
N_EXPERT_GROUPS = 8
TOPK_GROUPS = 4
TOP_K = 8
ROUTED_SCALE = 2.5
RMS_EPS = 1e-6
GN_EPS = 64e-5
CHUNK = 64
HEADS_PER_GROUP = 4
GROUP_LANES = HEADS_PER_GROUP * HEAD_DIM
VMEM_LIMIT = 56 * 1024 * 1024

HIGHEST = lax.Precision.HIGHEST


def _round_up(n, m):
    return (n + m - 1) // m * m


def _cparams(*sem):
    return pltpu.CompilerParams(dimension_semantics=sem, vmem_limit_bytes=VMEM_LIMIT)


def _dot(a, b):
    return jnp.dot(a, b, preferred_element_type=F32)


def _dot_nt(a, b):
    return lax.dot_general(a, b, (((1,), (1,)), ((), ())), preferred_element_type=F32)


def _dot_tn(a, b):
    return lax.dot_general(a, b, (((0,), (0,)), ((), ())), preferred_element_type=F32)


def _sigmoid(x):
    return 1.0 / (1.0 + jnp.exp(-x))


def _norm_mod(x, gain, shift, scale):
    y = x * lax.rsqrt(jnp.mean(x * x, axis=-1, keepdims=True) + RMS_EPS) * gain
    return y * (1.0 + scale) + shift


def _ada_kernel(c_ref, w_ref, b_ref, o_ref):
    c = c_ref[...]
    cond = c * _sigmoid(c)
    o_ref[...] = jnp.dot(cond, w_ref[...], preferred_element_type=F32, precision=HIGHEST) + b_ref[...]


def _ada(c_pad, ada_w, ada_b):
    nl, d, n = ada_w.shape
    tn = 512
    return pl.pallas_call(
        _ada_kernel,
        out_shape=jax.ShapeDtypeStruct((nl, c_pad.shape[0], n), F32),
        grid=(nl, n // tn),
        in_specs=[
            pl.BlockSpec((c_pad.shape[0], d), lambda l, j: (0, 0)),
            pl.BlockSpec((None, d, tn), lambda l, j: (l, 0, j)),
            pl.BlockSpec((None, 1, tn), lambda l, j: (l, 0, j)),
        ],
        out_specs=pl.BlockSpec((None, c_pad.shape[0], tn), lambda l, j: (l, 0, j)),
        compiler_params=_cparams("parallel", "parallel"),
        name="ada",
    )(c_pad, ada_w, ada_b.reshape(nl, 1, n))


def _bias_kernel(rbt_ref, oh_ref, o_ref):
    o_ref[...] = jnp.dot(rbt_ref[...], oh_ref[...], preferred_element_type=F32, precision=HIGHEST)


def _bucket_onehot():
    qi = np.arange(WINDOW)[:, None]
    kj = np.arange(2 * WINDOW)[None, :]
    dist = qi + WINDOW - kj
    max_exact = N_BUCKETS // 2
    d = np.maximum(dist, 0)
    far = max_exact + (np.log(np.maximum(d, max_exact).astype(np.float32) / np.float32(max_exact))
                       / np.float32(math.log(MAX_DISTANCE / max_exact))
                       * np.float32(N_BUCKETS - max_exact)).astype(np.int32)
    bucket = np.where(d < max_exact, d, np.minimum(far, N_BUCKETS - 1)).reshape(-1)
    return (np.arange(N_BUCKETS)[:, None] == bucket[None, :]).astype(np.float32)


def _bias_table(rel_bias):
    nb, nh = rel_bias.shape
    onehot = jnp.asarray(_bucket_onehot())
    out = pl.pallas_call(
        _bias_kernel,
        out_shape=jax.ShapeDtypeStruct((nh, WINDOW * 2 * WINDOW), F32),
        name="bias_table",
        compiler_params=pltpu.CompilerParams(vmem_limit_bytes=VMEM_LIMIT),
    )(rel_bias.T, onehot)
    return out.reshape(nh, WINDOW, 2 * WINDOW)


def _in_proj_kernel(x_ref, gain_ref, sh_ref, sc_ref, w_ref, o_ref, u_ref):
    @pl.when(pl.program_id(1) == 0)
    def _():
        u_ref[...] = _norm_mod(x_ref[...], gain_ref[...], sh_ref[...], sc_ref[...]).astype(BF16)

    o_ref[...] = _dot(u_ref[...], w_ref[...]).astype(o_ref.dtype)


def _in_proj(x2, gain, shift, scale, w, seq, tm=512, tn=512):
    t, d = x2.shape
    n = w.shape[1]
    bmap = lambda i, j: ((i * tm) // seq, 0, 0)
    return pl.pallas_call(
        _in_proj_kernel,
        out_shape=jax.ShapeDtypeStruct((t, n), F32),
        grid=(t // tm, n // tn),
        in_specs=[
            pl.BlockSpec((tm, d), lambda i, j: (i, 0)),
            pl.BlockSpec((1, d), lambda i, j: (0, 0)),
            pl.BlockSpec((None, 1, d), bmap),
            pl.BlockSpec((None, 1, d), bmap),
            pl.BlockSpec((d, tn), lambda i, j: (0, j)),
        ],
        out_specs=pl.BlockSpec((tm, tn), lambda i, j: (i, j)),
        scratch_shapes=[pltpu.VMEM((tm, d), BF16)],
        compiler_params=_cparams("parallel", "arbitrary"),
        name="in_proj",
    )(x2, gain, shift, scale, w)


def _rwkv_prep_kernel(rw, lp, glp, p_ref, prev_ref, mu_ref, w0_ref, w2_ref, a0_ref, a2_ref, g2_ref,
                      r_ref, k_ref, v_ref, a_ref, g_ref, lw_ref, *, blocks_per_seq):
    i = pl.program_id(0)
    cur = p_ref[...]
    ts = cur.shape[0]
    first = (i % blocks_per_seq) == 0
    prev_row = jnp.where(first, 0.0, prev_ref[7:8, :])
    rolled = pltpu.roll(cur, 1, axis=0)
    row = lax.broadcasted_iota(jnp.int32, cur.shape, 0)
    shifted = jnp.where(row == 0, prev_row, rolled)
    x = cur + (shifted - cur) * mu_ref[...]
    r_ref[...] = x[:, 0:rw].astype(r_ref.dtype)
    k_ref[...] = x[:, rw:2 * rw].astype(k_ref.dtype)
    v_ref[...] = x[:, 2 * rw:3 * rw].astype(v_ref.dtype)
    o = 3 * rw
    xw = x[:, o:o + lp]
    xa = x[:, o + lp:o + 2 * lp]
    xg = x[:, o + 2 * lp:o + 2 * lp + glp]
    z = w0_ref[...] + _dot(jnp.tanh(xw).astype(BF16), w2_ref[...])
    lw_ref[...] = -math.exp(-0.5) * _sigmoid(z)
    a_ref[...] = _sigmoid(a0_ref[...] + _dot(xa.astype(BF16), a2_ref[...])).astype(a_ref.dtype)
    g_ref[...] = _dot(_sigmoid(xg).astype(BF16), g2_ref[...]).astype(g_ref.dtype)


def _rwkv_prep(proj, mu, w0, w2, a0, a2, g2, rw, lp, glp, seq, ts=256):
    t = proj.shape[0]
    width = 3 * rw + 2 * lp + glp
    wblock = _round_up(width, 512)
    full = lambda i: (0, 0)
    outs = [jax.ShapeDtypeStruct((t, rw), BF16)] * 5 + [jax.ShapeDtypeStruct((t, rw), F32)]
    ospec = pl.BlockSpec((ts, rw), lambda i: (i, 0))
    return pl.pallas_call(
        functools.partial(_rwkv_prep_kernel, rw, lp, glp, blocks_per_seq=seq // ts),
        out_shape=outs,
        grid=(t // ts,),
        in_specs=[
            pl.BlockSpec((ts, wblock), lambda i: (i, 0)),
            pl.BlockSpec((8, wblock), lambda i: (jnp.maximum(i * (ts // 8) - 1, 0), 0)),
            pl.BlockSpec((1, wblock), full),
            pl.BlockSpec((1, rw), full),
            pl.BlockSpec((lp, rw), full),
            pl.BlockSpec((1, rw), full),
            pl.BlockSpec((lp, rw), full),
            pl.BlockSpec((glp, rw), full),
        ],
        out_specs=[ospec] * 6,
        compiler_params=_cparams("parallel"),
        name="rwkv_prep",
    )(proj, proj, mu, w0, w2, a0, a2, g2)


def _chunk_constants():
    n = GROUP_LANES
    i = np.arange(n)[:, None]
    j = np.arange(n)[None, :]
    same_head = (i // CHUNK) == (j // CHUNK)
    code = np.zeros((n, n), np.int32)
    for lvl in range(int(math.log2(CHUNK)), 0, -1):
        code = np.where(same_head & (i > j) & ((i >> lvl) == (j >> lvl)), lvl, code)
    return code, same_head.astype(np.float32)


def _seg_sum(x, ones_bd):
    hi = x.astype(BF16)
    lo = (x - hi.astype(F32)).astype(BF16)
    return _dot(hi, ones_bd) + _dot(lo, ones_bd)


def _rwkv_chunk_kernel(r_ref, k_ref, v_ref, a_ref, g_ref, lw_ref,
                       kk_ref, ka_ref, rk_ref, lnw_ref, lnb_ref, code_ref, bd_ref,
                       o_ref, s_ref):
    @pl.when(pl.program_id(2) == 0)
    def _():
        s_ref[...] = jnp.zeros_like(s_ref)

    n = GROUP_LANES
    code = code_ref[...]
    bd = bd_ref[...]
    bdmask = bd > 0
    rowc = lax.broadcasted_iota(jnp.int32, (CHUNK, n), 0)
    lanec = lax.broadcasted_iota(jnp.int32, (CHUNK, n), 1) % CHUNK
    tri_incl = rowc >= lanec
    ltri = (lax.broadcasted_iota(jnp.int32, (CHUNK, CHUNK), 0)
            >= lax.broadcasted_iota(jnp.int32, (CHUNK, CHUNK), 1)).astype(F32)
    eye = (lax.broadcasted_iota(jnp.int32, (n, n), 0) == lax.broadcasted_iota(jnp.int32, (n, n), 1)).astype(F32)
    k_k = kk_ref[...]
    k_a = ka_ref[...]
    r_k = rk_ref[...]
    ln_w = lnw_ref[...]
    ln_b = lnb_ref[...]

    def stack(x):
        return jnp.where(bdmask, jnp.concatenate([x] * HEADS_PER_GROUP, axis=0), 0.0).astype(BF16)

    def chunk(ci, carry):
        rows = pl.ds(pl.multiple_of(ci * CHUNK, CHUNK), CHUNK)
        r = r_ref[rows, :].astype(F32)
        k = k_ref[rows, :].astype(F32)
        v = v_ref[rows, :].astype(F32)
        a = a_ref[rows, :].astype(F32)
        g = g_ref[rows, :].astype(F32)
        lw = lw_ref[rows, :]

        kkp = k * k_k
        kk = kkp / jnp.maximum(jnp.sqrt(_seg_sum(kkp * kkp, bd)), 1e-12)
        k2 = k * (1.0 + (a - 1.0) * k_a)

        cum = jnp.dot(ltri, lw, preferred_element_type=F32, precision=HIGHEST)
        cum_end = cum[CHUNK - 1:CHUNK, :]
        e_in = jnp.exp(cum)
        e_ex = jnp.exp(cum - lw)
        e_neg = jnp.exp(-cum)
        e_dec = jnp.exp(cum_end - cum)
        r_t = r * e_in
        a_t = -kk * e_ex
        kb = kk * a
        b_t = kb * e_neg
        k_t = k2 * e_neg

        ar = jnp.concatenate([a_t, r_t], axis=0).astype(BF16)
        b_s = stack(b_t)
        k_s = stack(k_t)
        v_s = stack(v)
        a_s = stack(a_t)
        ab_rb = _dot_nt(ar, b_s)
        ak_rk = _dot_nt(ar, k_s)
        rb_w = jnp.where(tri_incl, ab_rb[CHUNK:], 0.0).astype(BF16)
        rk_w = jnp.where(tri_incl, ak_rk[CHUNK:], 0.0).astype(BF16)
        amat = jnp.where(code > 0, jnp.concatenate([ab_rb[:CHUNK]] * HEADS_PER_GROUP, axis=0), 0.0)
        akmat = jnp.where(code > 0, jnp.concatenate([ak_rk[:CHUNK]] * HEADS_PER_GROUP, axis=0), 0.0).astype(BF16)

        tinv = eye + jnp.where(code == 1, amat, 0.0)
        for lvl in range(2, int(math.log2(CHUNK)) + 1):
            tb = tinv.astype(BF16)
            off = jnp.where(code == lvl, amat, 0.0).astype(BF16)
            tinv = tinv + _dot(tb, _dot(off, tb).astype(BF16))

        s_bd = s_ref[...]
        s_b = s_bd.astype(BF16)
        x_s = _dot_nt(a_s, s_b) + _dot(akmat, v_s)
        u_s = _dot(tinv.astype(BF16), x_s.astype(BF16))
        u_b = u_s.astype(BF16)
        y = _dot_nt(r_t.astype(BF16), s_b) + _dot(rb_w, u_b) + _dot(rk_w, v_s)
        u = u_s[0:CHUNK]
        for h in range(1, HEADS_PER_GROUP):
            u = u + u_s[h * CHUNK:(h + 1) * CHUNK]
        uv = jnp.concatenate([u, v], axis=0).astype(BF16)
        bk = jnp.concatenate([kb * e_dec, k2 * e_dec], axis=0).astype(BF16)
        upd = _dot_tn(uv, bk)
        s_ref[...] = s_bd * jnp.exp(cum_end) + jnp.where(bdmask, upd, 0.0)

        inv_n = 1.0 / HEAD_DIM
        mean = _seg_sum(y, bd) * inv_n
        dlt = y - mean
        var = _seg_sum(dlt * dlt, bd) * inv_n
        yn = dlt * lax.rsqrt(var + GN_EPS) * ln_w + ln_b
        bonus = _seg_sum(r * k2 * r_k, bd) * v
        o_ref[rows, :] = ((yn + bonus) * g).astype(o_ref.dtype)
        return carry

    lax.fori_loop(0, r_ref.shape[0] // CHUNK, chunk, 0)


def _rwkv_chunk(r, k, v, a, g, lw, k_k, k_a, r_k, ln_w, ln_b, batch, seq, tb=256):
    t, rw = r.shape
    n = GROUP_LANES
    code, bd = _chunk_constants()
    nblk = seq // tb
    tmap = lambda b, h, c: (b * nblk + c, h)
    pmap = lambda b, h, c: (0, h)
    cmap = lambda b, h, c: (0, 0)
    tspec = pl.BlockSpec((tb, n), tmap)
    pspec = pl.BlockSpec((1, n), pmap)
    return pl.pallas_call(
        _rwkv_chunk_kernel,
        out_shape=jax.ShapeDtypeStruct((t, rw), BF16),
        grid=(batch, rw // n, nblk),
        in_specs=[tspec] * 6 + [pspec] * 5 + [pl.BlockSpec((n, n), cmap), pl.BlockSpec((n, n), cmap)],
        out_specs=pl.BlockSpec((tb, n), tmap),
        scratch_shapes=[pltpu.VMEM((n, n), F32)],
        compiler_params=_cparams("parallel", "parallel", "arbitrary"),
        name="rwkv_chunk",
    )(r, k, v, a, g, lw, k_k, k_a, r_k, ln_w, ln_b, jnp.asarray(code), jnp.asarray(bd, dtype=BF16))


def _swa_kernel(sink_ref, q_ref, kp_ref, kc_ref, vp_ref, vc_ref, bias_ref, o_ref, *, q_per_kv):
    j = pl.program_id(1)
    nblk = pl.program_id(2)
    blk = WINDOW
    kw = jnp.concatenate([kp_ref[...], kc_ref[...]], axis=0).astype(BF16)
    vw = jnp.concatenate([vp_ref[...], vc_ref[...]], axis=0)
    lane = lax.broadcasted_iota(jnp.int32, (1, LANES), 1)
    lo = lane < HEAD_DIM
    v_half = (jnp.where(lo, vw, 0.0).astype(BF16), jnp.where(lo, 0.0, vw).astype(BF16))
    qi = lax.broadcasted_iota(jnp.int32, (blk, 2 * blk), 0)
    kj = lax.broadcasted_iota(jnp.int32, (blk, 2 * blk), 1)
    valid = (kj > qi) & (kj <= qi + blk) & ((nblk > 0) | (kj >= blk))
    scale = HEAD_DIM ** -0.5
    for p in range(q_per_kv // 2):
        qp = q_ref[:, p * LANES:(p + 1) * LANES] * scale
        q_half = (jnp.where(lo, qp, 0.0).astype(BF16), jnp.where(lo, 0.0, qp).astype(BF16))
        acc = None
        for half in range(2):
            g = 2 * p + half
            sink = sink_ref[j * q_per_kv + g]
            s = _dot_nt(q_half[half], kw) + bias_ref[g]
            s = jnp.where(valid, s, -jnp.inf)
            m = jnp.maximum(jnp.max(s, axis=-1, keepdims=True), sink)
            pr = jnp.exp(s - m)
            denom = jnp.sum(pr, axis=-1, keepdims=True) + jnp.exp(sink - m)
            o = _dot(pr.astype(BF16), v_half[half]) / denom
            acc = o if acc is None else acc + o
        o_ref[:, p * LANES:(p + 1) * LANES] = acc.astype(o_ref.dtype)


def _swa(proj, sinks, bias, batch, seq, q_off, n_kv, q_per_kv):
    t = proj.shape[0]
    blk = WINDOW
    nb = seq // blk
    qw = q_per_kv * HEAD_DIM
    k_off = q_off + n_kv * qw
    v_off = k_off + n_kv * LANES
    cur = lambda b, j, n: b * nb + n
    prev = lambda b, j, n: b * nb + jnp.maximum(n - 1, 0)
    return pl.pallas_call(
        functools.partial(_swa_kernel, q_per_kv=q_per_kv),
        out_shape=jax.ShapeDtypeStruct((t, n_kv * qw), BF16),
        grid=(batch, n_kv, nb),
        in_specs=[
            pl.BlockSpec(memory_space=pltpu.SMEM),
            pl.BlockSpec((blk, qw), lambda b, j, n: (cur(b, j, n), q_off // qw + j)),
            pl.BlockSpec((blk, LANES), lambda b, j, n: (prev(b, j, n), k_off // LANES + j)),
            pl.BlockSpec((blk, LANES), lambda b, j, n: (cur(b, j, n), k_off // LANES + j)),
            pl.BlockSpec((blk, LANES), lambda b, j, n: (prev(b, j, n), v_off // LANES + j)),
            pl.BlockSpec((blk, LANES), lambda b, j, n: (cur(b, j, n), v_off // LANES + j)),
            pl.BlockSpec((q_per_kv, blk, 2 * blk), lambda b, j, n: (j, 0, 0)),
        ],
        out_specs=pl.BlockSpec((blk, qw), lambda b, j, n: (cur(b, j, n), j)),
        compiler_params=_cparams("parallel", "parallel", "parallel"),
        name="swa",
    )(sinks, proj, proj, proj, proj, proj, bias)


def _out_proj_kernel(yr_ref, ya_ref, w_ref, x_ref, gain_ref, gate_ref, o_ref, y_ref, *, tn):
    j = pl.program_id(1)
    rw = yr_ref.shape[1]
    y = _dot(yr_ref[...], w_ref[0:rw, :]) + _dot(ya_ref[...], w_ref[rw:, :])
    y_ref[:, pl.ds(pl.multiple_of(j * tn, tn), tn)] = y

    @pl.when(j == pl.num_programs(1) - 1)
    def _():
        yy = y_ref[...]
        yn = yy * lax.rsqrt(jnp.mean(yy * yy, axis=-1, keepdims=True) + RMS_EPS) * gain_ref[...]
        o_ref[...] = x_ref[...] + gate_ref[...] * yn


def _out_proj(y_r, y_a, w, x2, gain, gate, seq, tm=256, tn=512):
    t, d = x2.shape
    rw = y_r.shape[1]
    aw = y_a.shape[1]
    bmap = lambda i, j: ((i * tm) // seq, 0, 0)
    return pl.pallas_call(
        functools.partial(_out_proj_kernel, tn=tn),
        out_shape=jax.ShapeDtypeStruct((t, d), F32),
        grid=(t // tm, d // tn),
        in_specs=[
            pl.BlockSpec((tm, rw), lambda i, j: (i, 0)),
            pl.BlockSpec((tm, aw), lambda i, j: (i, 0)),
            pl.BlockSpec((rw + aw, tn), lambda i, j: (0, j)),
            pl.BlockSpec((tm, d), lambda i, j: (i, 0)),
            pl.BlockSpec((1, d), lambda i, j: (0, 0)),
            pl.BlockSpec((None, 1, d), bmap),
        ],
        out_specs=pl.BlockSpec((tm, d), lambda i, j: (i, 0)),
        scratch_shapes=[pltpu.VMEM((tm, d), F32)],
        compiler_params=_cparams("parallel", "arbitrary"),
        name="out_proj",
    )(y_r, y_a, w, x2, gain, gate)


def _rank_below(vals, limit):
    n = vals.shape[0]
    idx = lax.broadcasted_iota(jnp.int32, vals.shape, 0)
    rank = jnp.zeros(vals.shape, jnp.int32)
    for e in range(n):
        row = vals[e:e + 1, :]
        beats = (row > vals) | ((row == vals) & (e < idx))
        rank = rank + beats.astype(jnp.int32)
    return rank < limit


def _router_kernel(x_ref, gain_ref, sh_ref, sc_ref, rwt_ref, rb_ref, u_ref, gates_ref):
    u = _norm_mod(x_ref[...], gain_ref[...], sh_ref[...], sc_ref[...])
    u_ref[...] = u.astype(u_ref.dtype)
    logits = lax.dot_general(rwt_ref[...], u, (((1,), (1,)), ((), ())),
                             preferred_element_type=F32, precision=HIGHEST)
    scores = _sigmoid(logits)
    sel = scores + rb_ref[:, 0:1]
    ne = sel.shape[0]
    gsz = ne // N_EXPERT_GROUPS
    sub = lax.broadcasted_iota(jnp.int32, (gsz, sel.shape[1]), 0)
    gs_rows = []
    for gi in range(N_EXPERT_GROUPS):
        grp = sel[gi * gsz:(gi + 1) * gsz, :]
        m1 = jnp.max(grp, axis=0, keepdims=True)
        i1 = jnp.min(jnp.where(grp == m1, sub, gsz), axis=0, keepdims=True)
        m2 = jnp.max(jnp.where(sub == i1, -jnp.inf, grp), axis=0, keepdims=True)
        gs_rows.append(m1 + m2)
    gscore = jnp.concatenate(gs_rows, axis=0)
    gkeep = _rank_below(gscore, TOPK_GROUPS)
    ekeep = jnp.concatenate(
        [jnp.broadcast_to(gkeep[gi:gi + 1, :], (gsz, sel.shape[1])) for gi in range(N_EXPERT_GROUPS)], axis=0)
    sel = jnp.where(ekeep, sel, -jnp.inf)
    chosen = _rank_below(sel, TOP_K)
    wts = jnp.where(chosen, scores, 0.0)
    gates_t = wts / jnp.sum(wts, axis=0, keepdims=True) * ROUTED_SCALE
    gates_ref[...] = gates_t.T


def _router(x2, gain, shift, scale, rw_t, rb, seq, tm=256):
    t, d = x2.shape
    ne = rw_t.shape[0]
    bmap = lambda i: ((i * tm) // seq, 0, 0)
    return pl.pallas_call(
        _router_kernel,
        out_shape=[jax.ShapeDtypeStruct((t, d), BF16), jax.ShapeDtypeStruct((t, ne), F32)],
        grid=(t // tm,),
        in_specs=[
            pl.BlockSpec((tm, d), lambda i: (i, 0)),
            pl.BlockSpec((1, d), lambda i: (0, 0)),
            pl.BlockSpec((None, 1, d), bmap),
            pl.BlockSpec((None, 1, d), bmap),
            pl.BlockSpec((ne, d), lambda i: (0, 0)),
            pl.BlockSpec((ne, 1), lambda i: (0, 0)),
        ],
        out_specs=[pl.BlockSpec((tm, d), lambda i: (i, 0)), pl.BlockSpec((tm, ne), lambda i: (i, 0))],
        compiler_params=_cparams("parallel"),
        name="router",
    )(x2, gain, shift, scale, rw_t, rb)


def _moe_kernel(u_ref, g_ref, wgu_ref, wd_ref, o_ref):
    e = pl.program_id(1)
    ff = wd_ref.shape[0]
    h = _dot(u_ref[...], wgu_ref[...])
    hg = h[:, :ff]
    hu = h[:, ff:]
    lane = lax.broadcasted_iota(jnp.int32, g_ref.shape, 1)
    gcol = jnp.sum(jnp.where(lane == e, g_ref[...], 0.0), axis=-1, keepdims=True)
    act = (hg * _sigmoid(hg) * hu * gcol).astype(BF16)
    contrib = _dot(act, wd_ref[...])

    @pl.when(e == 0)
    def _():
        o_ref[...] = contrib

    @pl.when(e > 0)
    def _():
        o_ref[...] += contrib


def _moe(u, gates, w_gu, w_down, tm=512):
    t, d = u.shape
    ne, _, ff2 = w_gu.shape
    ff = ff2 // 2
    return pl.pallas_call(
        _moe_kernel,
        out_shape=jax.ShapeDtypeStruct((t, d), F32),
        grid=(t // tm, ne),
        in_specs=[
            pl.BlockSpec((tm, d), lambda i, e: (i, 0)),
            pl.BlockSpec((tm, gates.shape[1]), lambda i, e: (i, 0)),
            pl.BlockSpec((None, d, ff2), lambda i, e: (e, 0, 0)),
            pl.BlockSpec((None, ff, d), lambda i, e: (e, 0, 0)),
        ],
        out_specs=pl.BlockSpec((tm, d), lambda i, e: (i, 0)),
        compiler_params=_cparams("parallel", "arbitrary"),
        name="moe",
    )(u, gates, w_gu, w_down)


def _residual_kernel(x_ref, y_ref, gain_ref, gate_ref, o_ref):
    y = y_ref[...]
    yn = y * lax.rsqrt(jnp.mean(y * y, axis=-1, keepdims=True) + RMS_EPS) * gain_ref[...]
    o_ref[...] = x_ref[...] + gate_ref[...] * yn


def _residual(x2, y, gain, gate, seq, tm=256):
    t, d = x2.shape
    bmap = lambda i: ((i * tm) // seq, 0, 0)
    row = pl.BlockSpec((tm, d), lambda i: (i, 0))
    return pl.pallas_call(
        _residual_kernel,
        out_shape=jax.ShapeDtypeStruct((t, d), F32),
        grid=(t // tm,),
        in_specs=[row, row, pl.BlockSpec((1, d), lambda i: (0, 0)), pl.BlockSpec((None, 1, d), bmap)],
        out_specs=row,
        compiler_params=_cparams("parallel"),
        name="residual",
    )(x2, y, gain, gate)


def _pad_cols(w, n):
    return jnp.pad(w, ((0, 0), (0, n - w.shape[1])))


def _pad_rows(w, n):
    return jnp.pad(w, ((0, n - w.shape[0]), (0, 0)))


def _layout_rwkv_cols(w, rw, dl, al, gl, lp, glp):
    o = 3 * rw
    parts = [w[:, :o], _pad_cols(w[:, o:o + dl], lp), _pad_cols(w[:, o + dl:o + dl + al], lp),
             _pad_cols(w[:, o + dl + al:o + dl + al + gl], glp)]
    out = jnp.concatenate(parts, axis=1)
    return _pad_cols(out, _round_up(out.shape[1], 512))


def _layout_in_cols(w, rw, dl, al, gl, aw, n_kv, lp, glp):
    parts = [_layout_rwkv_cols(w, rw, dl, al, gl, lp, glp)]
    a0 = 3 * rw + dl + al + gl
    parts.append(w[:, a0:a0 + aw])
    for base in (a0 + aw, a0 + aw + n_kv * HEAD_DIM):
        for j in range(n_kv):
            kv = w[:, base + j * HEAD_DIM:base + (j + 1) * HEAD_DIM]
            parts += [kv, kv]
    out = jnp.concatenate(parts, axis=1)
    return _pad_cols(out, _round_up(out.shape[1], 512))


def kernel(x, c, rel_bias, ada_w, ada_b, norm_gains, w_in, shift_mu, rwkv_w0, rwkv_w2, rwkv_a0, rwkv_a2, rwkv_g2,
           rwkv_k_k, rwkv_k_a, rwkv_r_k, rwkv_ln_w, rwkv_ln_b, attn_sinks, w_out, router_w, router_bias,
           expert_w_gu, expert_w_down, shared_w_gu, shared_w_down):
    batch, seq, d = x.shape
    depth = ada_w.shape[0]
    t = batch * seq
    rw = rwkv_w0.shape[1]
    dl, al, gl = rwkv_w2.shape[1], rwkv_a2.shape[1], rwkv_g2.shape[1]
    aw = w_out.shape[1] - rw
    n_q = aw // HEAD_DIM
    n_kv = (w_in.shape[2] - (3 * rw + dl + al + gl) - aw) // (2 * HEAD_DIM)
    q_per_kv = n_q // n_kv
    ne = router_w.shape[2]
    lp = _round_up(max(dl, al), LANES)
    glp = _round_up(gl, LANES)
    q_off = _round_up(3 * rw + 2 * lp + glp, 512)

    c_pad = jnp.pad(c, ((0, 8 - batch % 8 if batch % 8 else 0), (0, 0)))
    ada = _ada(c_pad, ada_w, ada_b)[:, :batch]
    bias = _bias_table(rel_bias)

    x2 = x.reshape(t, d)
    for l in range(depth):
        sh_m, sc_m, gt_m, sh_f, sc_f, gt_f = [a.reshape(batch, 1, d) for a in jnp.split(ada[l], 6, axis=-1)]
        gains = norm_gains[l].reshape(4, 1, d)

        w_in_l = _layout_in_cols(w_in[l], rw, dl, al, gl, aw, n_kv, lp, glp).astype(BF16)
        mu_l = _layout_rwkv_cols(shift_mu[l][None, :], rw, dl, al, gl, lp, glp)
        proj = _in_proj(x2, gains[0], sh_m, sc_m, w_in_l, seq)

        r, k, v, a, g, lw = _rwkv_prep(
            proj, mu_l, rwkv_w0[l][None, :], _pad_rows(rwkv_w2[l], lp).astype(BF16), rwkv_a0[l][None, :],
            _pad_rows(rwkv_a2[l], lp).astype(BF16), _pad_rows(rwkv_g2[l], glp).astype(BF16), rw, lp, glp, seq)
        y_r = _rwkv_chunk(r, k, v, a, g, lw, rwkv_k_k[l][None, :], rwkv_k_a[l][None, :],
                          rwkv_r_k[l].reshape(1, rw), rwkv_ln_w[l][None, :], rwkv_ln_b[l][None, :], batch, seq)
        y_a = _swa(proj, attn_sinks[l], bias, batch, seq, q_off, n_kv, q_per_kv)
        x2 = _out_proj(y_r, y_a, w_out[l].astype(BF16), x2, gains[1], gt_m, seq)

        u, gates = _router(x2, gains[2], sh_f, sc_f, router_w[l].T, router_bias[l][:, None], seq)
        gates_all = jnp.concatenate(
            [gates, jnp.ones((t, 1), F32), jnp.zeros((t, _round_up(ne + 1, LANES) - ne - 1), F32)], axis=1)
        w_gu_all = jnp.concatenate([expert_w_gu[l].astype(BF16), shared_w_gu[l][None].astype(BF16)], axis=0)
        w_down_all = jnp.concatenate([expert_w_down[l].astype(BF16), shared_w_down[l][None].astype(BF16)], axis=0)
        y = _moe(u, gates_all, w_gu_all, w_down_all)
        x2 = _residual(x2, y, gains[3], gt_f, seq)
    return x2.reshape(batch, seq, d)
```

```python
import functools
import math

import numpy as np
import jax
import jax.numpy as jnp
from jax import lax
from jax.experimental import pallas as pl
from jax.experimental.pallas import tpu as pltpu

F32 = jnp.float32
BF16 = jnp.bfloat16

HEAD_DIM = 64
LANES = 128
WINDOW = 128
N_BUCKETS = 32
MAX_DISTANCE = 128
N_EXPERT_GROUPS = 8
TOPK_GROUPS = 4
TOP_K = 8
ROUTED_SCALE = 2.5
RMS_EPS = 1e-6
GN_EPS = 64e-5
CHUNK = 64
HEADS_PER_GROUP = 4
GROUP_LANES = HEADS_PER_GROUP * HEAD_DIM
VMEM_LIMIT = 56 * 1024 * 1024

HIGHEST = lax.Precision.HIGHEST


def _round_up(n, m):
    return (n + m - 1) // m * m


def _cparams(*sem):
    return pltpu.CompilerParams(dimension_semantics=sem, vmem_limit_bytes=VMEM_LIMIT)


def _dot(a, b):
    return jnp.dot(a, b, preferred_element_type=F32)


def _dot_nt(a, b):
    return lax.dot_general(a, b, (((1,), (1,)), ((), ())), preferred_element_type=F32)


def _dot_tn(a, b):
    return lax.dot_general(a, b, (((0,), (0,)), ((), ())), preferred_element_type=F32)


def _sigmoid(x):
    return 1.0 / (1.0 + jnp.exp(-x))


def _norm_mod(x, gain, shift, scale):
    y = x * lax.rsqrt(jnp.mean(x * x, axis=-1, keepdims=True) + RMS_EPS) * gain
    return y * (1.0 + scale) + shift


def _ada_kernel(c_ref, w_ref, b_ref, o_ref):
    c = c_ref[...]
    cond = c * _sigmoid(c)
    o_ref[...] = jnp.dot(cond, w_ref[...], preferred_element_type=F32, precision=HIGHEST) + b_ref[...]


def _ada(c_pad, ada_w, ada_b):
    nl, d, n = ada_w.shape
    tn = 512
    return pl.pallas_call(
        _ada_kernel,
        out_shape=jax.ShapeDtypeStruct((nl, c_pad.shape[0], n), F32),
        grid=(nl, n // tn),
        in_specs=[
            pl.BlockSpec((c_pad.shape[0], d), lambda l, j: (0, 0)),
            pl.BlockSpec((None, d, tn), lambda l, j: (l, 0, j)),
            pl.BlockSpec((None, 1, tn), lambda l, j: (l, 0, j)),
        ],
        out_specs=pl.BlockSpec((None, c_pad.shape[0], tn), lambda l, j: (l, 0, j)),
        compiler_params=_cparams("parallel", "parallel"),
        name="ada",
    )(c_pad, ada_w, ada_b.reshape(nl, 1, n))


def _bias_kernel(rbt_ref, oh_ref, o_ref):
    o_ref[...] = jnp.dot(rbt_ref[...], oh_ref[...], preferred_element_type=F32, precision=HIGHEST)


def _bucket_onehot():
    qi = np.arange(WINDOW)[:, None]
    kj = np.arange(2 * WINDOW)[None, :]
    dist = qi + WINDOW - kj
    max_exact = N_BUCKETS // 2
    d = np.maximum(dist, 0)
    far = max_exact + (np.log(np.maximum(d, max_exact).astype(np.float32) / np.float32(max_exact))
                       / np.float32(math.log(MAX_DISTANCE / max_exact))
                       * np.float32(N_BUCKETS - max_exact)).astype(np.int32)
    bucket = np.where(d < max_exact, d, np.minimum(far, N_BUCKETS - 1)).reshape(-1)
    return (np.arange(N_BUCKETS)[:, None] == bucket[None, :]).astype(np.float32)


def _bias_table(rel_bias):
    nb, nh = rel_bias.shape
    onehot = jnp.asarray(_bucket_onehot())
    out = pl.pallas_call(
        _bias_kernel,
        out_shape=jax.ShapeDtypeStruct((nh, WINDOW * 2 * WINDOW), F32),
        name="bias_table",
        compiler_params=pltpu.CompilerParams(vmem_limit_bytes=VMEM_LIMIT),
    )(rel_bias.T, onehot)
    return out.reshape(nh, WINDOW, 2 * WINDOW)


def _in_proj_kernel(x_ref, gain_ref, sh_ref, sc_ref, w_ref, o_ref, u_ref):
    @pl.when(pl.program_id(1) == 0)
    def _():
        u_ref[...] = _norm_mod(x_ref[...], gain_ref[...], sh_ref[...], sc_ref[...]).astype(BF16)

    o_ref[...] = _dot(u_ref[...], w_ref[...]).astype(o_ref.dtype)


def _in_proj(x2, gain, shift, scale, w, seq, tm=512, tn=512):
    t, d = x2.shape
    n = w.shape[1]
    bmap = lambda i, j: ((i * tm) // seq, 0, 0)
    return pl.pallas_call(
        _in_proj_kernel,
        out_shape=jax.ShapeDtypeStruct((t, n), F32),
        grid=(t // tm, n // tn),
        in_specs=[
            pl.BlockSpec((tm, d), lambda i, j: (i, 0)),
            pl.BlockSpec((1, d), lambda i, j: (0, 0)),
            pl.BlockSpec((None, 1, d), bmap),
            pl.BlockSpec((None, 1, d), bmap),
            pl.BlockSpec((d, tn), lambda i, j: (0, j)),
        ],
        out_specs=pl.BlockSpec((tm, tn), lambda i, j: (i, j)),
        scratch_shapes=[pltpu.VMEM((tm, d), BF16)],
        compiler_params=_cparams("parallel", "arbitrary"),
        name="in_proj",
    )(x2, gain, shift, scale, w)


def _rwkv_prep_kernel(rw, lp, glp, p_ref, prev_ref, mu_ref, w0_ref, w2_ref, a0_ref, a2_ref, g2_ref,
                      r_ref, k_ref, v_ref, a_ref, g_ref, lw_ref, *, blocks_per_seq):
    i = pl.program_id(0)
    cur = p_ref[...]
    ts = cur.shape[0]
    first = (i % blocks_per_seq) == 0
    prev_row = jnp.where(first, 0.0, prev_ref[7:8, :])
    rolled = pltpu.roll(cur, 1, axis=0)
    row = lax.broadcasted_iota(jnp.int32, cur.shape, 0)
    shifted = jnp.where(row == 0, prev_row, rolled)
    x = cur + (shifted - cur) * mu_ref[...]
    r_ref[...] = x[:, 0:rw].astype(r_ref.dtype)
    k_ref[...] = x[:, rw:2 * rw].astype(k_ref.dtype)
    v_ref[...] = x[:, 2 * rw:3 * rw].astype(v_ref.dtype)
    o = 3 * rw
    xw = x[:, o:o + lp]
    xa = x[:, o + lp:o + 2 * lp]
    xg = x[:, o + 2 * lp:o + 2 * lp + glp]
    z = w0_ref[...] + _dot(jnp.tanh(xw).astype(BF16), w2_ref[...])
    lw_ref[...] = -math.exp(-0.5) * _sigmoid(z)
    a_ref[...] = _sigmoid(a0_ref[...] + _dot(xa.astype(BF16), a2_ref[...])).astype(a_ref.dtype)
    g_ref[...] = _dot(_sigmoid(xg).astype(BF16), g2_ref[...]).astype(g_ref.dtype)


def _rwkv_prep(proj, mu, w0, w2, a0, a2, g2, rw, lp, glp, seq, ts=256):
    t = proj.shape[0]
    width = 3 * rw + 2 * lp + glp
    wblock = _round_up(width, 512)
    full = lambda i: (0, 0)
    outs = [jax.ShapeDtypeStruct((t, rw), BF16)] * 5 + [jax.ShapeDtypeStruct((t, rw), F32)]
    ospec = pl.BlockSpec((ts, rw), lambda i: (i, 0))
    return pl.pallas_call(
        functools.partial(_rwkv_prep_kernel, rw, lp, glp, blocks_per_seq=seq // ts),
        out_shape=outs,
        grid=(t // ts,),
        in_specs=[
            pl.BlockSpec((ts, wblock), lambda i: (i, 0)),
            pl.BlockSpec((8, wblock), lambda i: (jnp.maximum(i * (ts // 8) - 1, 0), 0)),
            pl.BlockSpec((1, wblock), full),
            pl.BlockSpec((1, rw), full),
            pl.BlockSpec((lp, rw), full),
            pl.BlockSpec((1, rw), full),
            pl.BlockSpec((lp, rw), full),
            pl.BlockSpec((glp, rw), full),
        ],
        out_specs=[ospec] * 6,
        compiler_params=_cparams("parallel"),
        name="rwkv_prep",
    )(proj, proj, mu, w0, w2, a0, a2, g2)


def _chunk_constants():
    n = GROUP_LANES
    i = np.arange(n)[:, None]
    j = np.arange(n)[None, :]
    same_head = (i // CHUNK) == (j // CHUNK)
    code = np.zeros((n, n), np.int32)
    for lvl in range(int(math.log2(CHUNK)), 0, -1):
        code = np.where(same_head & (i > j) & ((i >> lvl) == (j >> lvl)), lvl, code)
    return code, same_head.astype(np.float32)


def _seg_sum(x, ones_bd):
    hi = x.astype(BF16)
    lo = (x - hi.astype(F32)).astype(BF16)
    return _dot(hi, ones_bd) + _dot(lo, ones_bd)


def _rwkv_chunk_kernel(r_ref, k_ref, v_ref, a_ref, g_ref, lw_ref,
                       kk_ref, ka_ref, rk_ref, lnw_ref, lnb_ref, code_ref, bd_ref,
                       o_ref, s_ref):
    @pl.when(pl.program_id(2) == 0)
    def _():
        s_ref[...] = jnp.zeros_like(s_ref)

    n = GROUP_LANES
    code = code_ref[...]
    bd = bd_ref[...]
    bdmask = bd > 0
    rowc = lax.broadcasted_iota(jnp.int32, (CHUNK, n), 0)
    lanec = lax.broadcasted_iota(jnp.int32, (CHUNK, n), 1) % CHUNK
    tri_incl = rowc >= lanec
    ltri = (lax.broadcasted_iota(jnp.int32, (CHUNK, CHUNK), 0)
            >= lax.broadcasted_iota(jnp.int32, (CHUNK, CHUNK), 1)).astype(F32)
    eye = (lax.broadcasted_iota(jnp.int32, (n, n), 0) == lax.broadcasted_iota(jnp.int32, (n, n), 1)).astype(F32)
    k_k = kk_ref[...]
    k_a = ka_ref[...]
    r_k = rk_ref[...]
    ln_w = lnw_ref[...]
    ln_b = lnb_ref[...]

    def stack(x):
        return jnp.where(bdmask, jnp.concatenate([x] * HEADS_PER_GROUP, axis=0), 0.0).astype(BF16)

    def chunk(ci, carry):
        rows = pl.ds(pl.multiple_of(ci * CHUNK, CHUNK), CHUNK)
        r = r_ref[rows, :].astype(F32)
        k = k_ref[rows, :].astype(F32)
        v = v_ref[rows, :].astype(F32)
        a = a_ref[rows, :].astype(F32)
        g = g_ref[rows, :].astype(F32)
        lw = lw_ref[rows, :]

        kkp = k * k_k
        kk = kkp / jnp.maximum(jnp.sqrt(_seg_sum(kkp * kkp, bd)), 1e-12)
        k2 = k * (1.0 + (a - 1.0) * k_a)

        cum = jnp.dot(ltri, lw, preferred_element_type=F32, precision=HIGHEST)
        cum_end = cum[CHUNK - 1:CHUNK, :]
        e_in = jnp.exp(cum)
        e_ex = jnp.exp(cum - lw)
        e_neg = jnp.exp(-cum)
        e_dec = jnp.exp(cum_end - cum)
        r_t = r * e_in
        a_t = -kk * e_ex
        kb = kk * a
        b_t = kb * e_neg
        k_t = k2 * e_neg

        ar = jnp.concatenate([a_t, r_t], axis=0).astype(BF16)
        b_s = stack(b_t)
        k_s = stack(k_t)
        v_s = stack(v)
        a_s = stack(a_t)
        ab_rb = _dot_nt(ar, b_s)
        ak_rk = _dot_nt(ar, k_s)
        rb_w = jnp.where(tri_incl, ab_rb[CHUNK:], 0.0).astype(BF16)
        rk_w = jnp.where(tri_incl, ak_rk[CHUNK:], 0.0).astype(BF16)
        amat = jnp.where(code > 0, jnp.concatenate([ab_rb[:CHUNK]] * HEADS_PER_GROUP, axis=0), 0.0)
        akmat = jnp.where(code > 0, jnp.concatenate([ak_rk[:CHUNK]] * HEADS_PER_GROUP, axis=0), 0.0).astype(BF16)

        tinv = eye + jnp.where(code == 1, amat, 0.0)
        for lvl in range(2, int(math.log2(CHUNK)) + 1):
            tb = tinv.astype(BF16)
            off = jnp.where(code == lvl, amat, 0.0).astype(BF16)
            tinv = tinv + _dot(tb, _dot(off, tb).astype(BF16))

        s_bd = s_ref[...]
        s_b = s_bd.astype(BF16)
        x_s = _dot_nt(a_s, s_b) + _dot(akmat, v_s)
        u_s = _dot(tinv.astype(BF16), x_s.astype(BF16))
        u_b = u_s.astype(BF16)
        y = _dot_nt(r_t.astype(BF16), s_b) + _dot(rb_w, u_b) + _dot(rk_w, v_s)
        u = u_s[0:CHUNK]
        for h in range(1, HEADS_PER_GROUP):
            u = u + u_s[h * CHUNK:(h + 1) * CHUNK]
        uv = jnp.concatenate([u, v], axis=0).astype(BF16)
        bk = jnp.concatenate([kb * e_dec, k2 * e_dec], axis=0).astype(BF16)
        upd = _dot_tn(uv, bk)
        s_ref[...] = s_bd * jnp.exp(cum_end) + jnp.where(bdmask, upd, 0.0)

        inv_n = 1.0 / HEAD_DIM
        mean = _seg_sum(y, bd) * inv_n
        dlt = y - mean
        var = _seg_sum(dlt * dlt, bd) * inv_n
        yn = dlt * lax.rsqrt(var + GN_EPS) * ln_w + ln_b
        bonus = _seg_sum(r * k2 * r_k, bd) * v
        o_ref[rows, :] = ((yn + bonus) * g).astype(o_ref.dtype)
        return carry

    lax.fori_loop(0, r_ref.shape[0] // CHUNK, chunk, 0)


def _rwkv_chunk(r, k, v, a, g, lw, k_k, k_a, r_k, ln_w, ln_b, batch, seq, tb=256):
    t, rw = r.shape
    n = GROUP_LANES
    code, bd = _chunk_constants()
    nblk = seq // tb
    tmap = lambda b, h, c: (b * nblk + c, h)
    pmap = lambda b, h, c: (0, h)
    cmap = lambda b, h, c: (0, 0)
    tspec = pl.BlockSpec((tb, n), tmap)
    pspec = pl.BlockSpec((1, n), pmap)
    return pl.pallas_call(
        _rwkv_chunk_kernel,
        out_shape=jax.ShapeDtypeStruct((t, rw), BF16),
        grid=(batch, rw // n, nblk),
        in_specs=[tspec] * 6 + [pspec] * 5 + [pl.BlockSpec((n, n), cmap), pl.BlockSpec((n, n), cmap)],
        out_specs=pl.BlockSpec((tb, n), tmap),
        scratch_shapes=[pltpu.VMEM((n, n), F32)],
        compiler_params=_cparams("parallel", "parallel", "arbitrary"),
        name="rwkv_chunk",
    )(r, k, v, a, g, lw, k_k, k_a, r_k, ln_w, ln_b, jnp.asarray(code), jnp.asarray(bd, dtype=BF16))


def _swa_kernel(sink_ref, q_ref, kp_ref, kc_ref, vp_ref, vc_ref, bias_ref, o_ref, *, q_per_kv):
    j = pl.program_id(1)
    nblk = pl.program_id(2)
    blk = WINDOW
    kw = jnp.concatenate([kp_ref[...], kc_ref[...]], axis=0).astype(BF16)
    vw = jnp.concatenate([vp_ref[...], vc_ref[...]], axis=0)
    lane = lax.broadcasted_iota(jnp.int32, (1, LANES), 1)
    lo = lane < HEAD_DIM
    v_half = (jnp.where(lo, vw, 0.0).astype(BF16), jnp.where(lo, 0.0, vw).astype(BF16))
    qi = lax.broadcasted_iota(jnp.int32, (blk, 2 * blk), 0)
    kj = lax.broadcasted_iota(jnp.int32, (blk, 2 * blk), 1)
    valid = (kj > qi) & (kj <= qi + blk) & ((nblk > 0) | (kj >= blk))
    scale = HEAD_DIM ** -0.5
    for p in range(q_per_kv // 2):
        qp = q_ref[:, p * LANES:(p + 1) * LANES] * scale
        q_half = (jnp.where(lo, qp, 0.0).astype(BF16), jnp.where(lo, 0.0, qp).astype(BF16))
        acc = None
        for half in range(2):
            g = 2 * p + half
            sink = sink_ref[j * q_per_kv + g]
            s = _dot_nt(q_half[half], kw) + bias_ref[g]
            s = jnp.where(valid, s, -jnp.inf)
            m = jnp.maximum(jnp.max(s, axis=-1, keepdims=True), sink)
            pr = jnp.exp(s - m)
            denom = jnp.sum(pr, axis=-1, keepdims=True) + jnp.exp(sink - m)
            o = _dot(pr.astype(BF16), v_half[half]) / denom
            acc = o if acc is None else acc + o
        o_ref[:, p * LANES:(p + 1) * LANES] = acc.astype(o_ref.dtype)


def _swa(proj, sinks, bias, batch, seq, q_off, n_kv, q_per_kv):
    t = proj.shape[0]
    blk = WINDOW
    nb = seq // blk
    qw = q_per_kv * HEAD_DIM
    k_off = q_off + n_kv * qw
    v_off = k_off + n_kv * LANES
    cur = lambda b, j, n: b * nb + n
    prev = lambda b, j, n: b * nb + jnp.maximum(n - 1, 0)
    return pl.pallas_call(
        functools.partial(_swa_kernel, q_per_kv=q_per_kv),
        out_shape=jax.ShapeDtypeStruct((t, n_kv * qw), BF16),
        grid=(batch, n_kv, nb),
        in_specs=[
            pl.BlockSpec(memory_space=pltpu.SMEM),
            pl.BlockSpec((blk, qw), lambda b, j, n: (cur(b, j, n), q_off // qw + j)),
            pl.BlockSpec((blk, LANES), lambda b, j, n: (prev(b, j, n), k_off // LANES + j)),
            pl.BlockSpec((blk, LANES), lambda b, j, n: (cur(b, j, n), k_off // LANES + j)),
            pl.BlockSpec((blk, LANES), lambda b, j, n: (prev(b, j, n), v_off // LANES + j)),
            pl.BlockSpec((blk, LANES), lambda b, j, n: (cur(b, j, n), v_off // LANES + j)),
            pl.BlockSpec((q_per_kv, blk, 2 * blk), lambda b, j, n: (j, 0, 0)),
        ],
        out_specs=pl.BlockSpec((blk, qw), lambda b, j, n: (cur(b, j, n), j)),
        compiler_params=_cparams("parallel", "parallel", "parallel"),
        name="swa",
    )(sinks, proj, proj, proj, proj, proj, bias)


def _out_proj_kernel(yr_ref, ya_ref, w_ref, x_ref, gain_ref, gate_ref, o_ref, y_ref, *, tn):
    j = pl.program_id(1)
    rw = yr_ref.shape[1]
    y = _dot(yr_ref[...], w_ref[0:rw, :]) + _dot(ya_ref[...], w_ref[rw:, :])
    y_ref[:, pl.ds(pl.multiple_of(j * tn, tn), tn)] = y

    @pl.when(j == pl.num_programs(1) - 1)
    def _():
        yy = y_ref[...]
        yn = yy * lax.rsqrt(jnp.mean(yy * yy, axis=-1, keepdims=True) + RMS_EPS) * gain_ref[...]
        o_ref[...] = x_ref[...] + gate_ref[...] * yn


def _out_proj(y_r, y_a, w, x2, gain, gate, seq, tm=256, tn=512):
    t, d = x2.shape
    rw = y_r.shape[1]
    aw = y_a.shape[1]
    bmap = lambda i, j: ((i * tm) // seq, 0, 0)
    return pl.pallas_call(
        functools.partial(_out_proj_kernel, tn=tn),
        out_shape=jax.ShapeDtypeStruct((t, d), F32),
        grid=(t // tm, d // tn),
        in_specs=[
            pl.BlockSpec((tm, rw), lambda i, j: (i, 0)),
            pl.BlockSpec((tm, aw), lambda i, j: (i, 0)),
            pl.BlockSpec((rw + aw, tn), lambda i, j: (0, j)),
            pl.BlockSpec((tm, d), lambda i, j: (i, 0)),
            pl.BlockSpec((1, d), lambda i, j: (0, 0)),
            pl.BlockSpec((None, 1, d), bmap),
        ],
        out_specs=pl.BlockSpec((tm, d), lambda i, j: (i, 0)),
        scratch_shapes=[pltpu.VMEM((tm, d), F32)],
        compiler_params=_cparams("parallel", "arbitrary"),
        name="out_proj",
    )(y_r, y_a, w, x2, gain, gate)


def _rank(vals):
    n = vals.shape[0]
    idx = lax.broadcasted_iota(jnp.int32, vals.shape, 0)
    rank = jnp.zeros(vals.shape, jnp.int32)
    for e in range(n):
        row = vals[e:e + 1, :]
        beats = (row > vals) | ((row == vals) & (e < idx))
        rank = rank + beats.astype(jnp.int32)
    return rank


def _router_kernel(x_ref, gain_ref, sh_ref, sc_ref, rwt_ref, rb_ref,
                   u_ref, idx_ref, pos_ref, gate_ref, cnt_ref, carry_ref):
    @pl.when(pl.program_id(0) == 0)
    def _():
        carry_ref[...] = jnp.zeros_like(carry_ref)

    u = _norm_mod(x_ref[...], gain_ref[...], sh_ref[...], sc_ref[...])
    u_ref[...] = u
    logits = lax.dot_general(rwt_ref[...], u, (((1,), (1,)), ((), ())),
                             preferred_element_type=F32, precision=HIGHEST)
    scores = _sigmoid(logits)
    sel = scores + rb_ref[:, 0:1]
    ne, tm = sel.shape
    gsz = ne // N_EXPERT_GROUPS
    sub = lax.broadcasted_iota(jnp.int32, (gsz, tm), 0)
    gs_rows = []
    for gi in range(N_EXPERT_GROUPS):
        grp = sel[gi * gsz:(gi + 1) * gsz, :]
        m1 = jnp.max(grp, axis=0, keepdims=True)
        i1 = jnp.min(jnp.where(grp == m1, sub, gsz), axis=0, keepdims=True)
        m2 = jnp.max(jnp.where(sub == i1, -jnp.inf, grp), axis=0, keepdims=True)
        gs_rows.append(m1 + m2)
    gkeep = _rank(jnp.concatenate(gs_rows, axis=0)) < TOPK_GROUPS
    ekeep = jnp.concatenate(
        [jnp.broadcast_to(gkeep[gi:gi + 1, :], (gsz, tm)) for gi in range(N_EXPERT_GROUPS)], axis=0)
    rank = _rank(jnp.where(ekeep, sel, -jnp.inf))
    chosen = rank < TOP_K
    wts = jnp.where(chosen, scores, 0.0)
    gates_t = wts / jnp.sum(wts, axis=0, keepdims=True) * ROUTED_SCALE

    cf = jnp.where(chosen, 1.0, 0.0)
    before = (lax.broadcasted_iota(jnp.int32, (tm, tm), 0) < lax.broadcasted_iota(jnp.int32, (tm, tm), 1))
    carry = carry_ref[:, 0:1]
    pos = carry + _dot(cf.astype(BF16), jnp.where(before, 1.0, 0.0).astype(BF16))
    carry_ref[...] = jnp.broadcast_to(carry + jnp.sum(cf, axis=1, keepdims=True), carry_ref.shape)
    cnt_ref[...] = carry_ref[...]

    eidx = lax.broadcasted_iota(jnp.int32, (ne, tm), 0)
    rows_i, rows_p, rows_g = [], [], []
    for k in range(TOP_K):
        mk = rank == k
        rows_i.append(jnp.sum(jnp.where(mk, eidx, 0), axis=0, keepdims=True))
        rows_p.append(jnp.sum(jnp.where(mk, pos, 0.0), axis=0, keepdims=True))
        rows_g.append(jnp.sum(jnp.where(mk, gates_t, 0.0), axis=0, keepdims=True))
    idx_ref[...] = jnp.concatenate(rows_i, axis=0)
    pos_ref[...] = jnp.concatenate(rows_p, axis=0).astype(jnp.int32)
    gate_ref[...] = jnp.concatenate(rows_g, axis=0)


def _router(x2, gain, shift, scale, rw_t, rb, seq, tm=256):
    t, d = x2.shape
    ne = rw_t.shape[0]
    bmap = lambda i: ((i * tm) // seq, 0, 0)
    pick = lambda dt: jax.ShapeDtypeStruct((TOP_K, t), dt)
    pspec = pl.BlockSpec((TOP_K, tm), lambda i: (0, i))
    return pl.pallas_call(
        _router_kernel,
        out_shape=[jax.ShapeDtypeStruct((t, d), F32), pick(jnp.int32), pick(jnp.int32), pick(F32),
                   jax.ShapeDtypeStruct((ne, LANES), F32)],
        grid=(t // tm,),
        in_specs=[
            pl.BlockSpec((tm, d), lambda i: (i, 0)),
            pl.BlockSpec((1, d), lambda i: (0, 0)),
            pl.BlockSpec((None, 1, d), bmap),
            pl.BlockSpec((None, 1, d), bmap),
            pl.BlockSpec((ne, d), lambda i: (0, 0)),
            pl.BlockSpec((ne, 1), lambda i: (0, 0)),
        ],
        out_specs=[pl.BlockSpec((tm, d), lambda i: (i, 0)), pspec, pspec, pspec,
                   pl.BlockSpec((ne, LANES), lambda i: (0, 0))],
        scratch_shapes=[pltpu.VMEM((ne, LANES), F32)],
        compiler_params=_cparams("arbitrary"),
        name="router",
    )(x2, gain, shift, scale, rw_t, rb)


def _row_copy(src, src_row, dst, dst_row, sem):
    return pltpu.make_async_copy(src.at[pl.ds(src_row, 1), :], dst.at[pl.ds(dst_row, 1), :], sem)


def _dispatch_kernel(rows_ref, u_ref, x_hbm, sem, *, n_pick):
    td = u_ref.shape[0]

    def body(tok, carry):
        for k in range(n_pick):
            _row_copy(u_ref, tok, x_hbm, rows_ref[0, k * td + tok], sem).start()
        return carry

    lax.fori_loop(0, td, body, 0)
    for k in range(n_pick):
        pltpu.make_async_copy(u_ref, x_hbm.at[pl.ds(0, td), :], sem).wait()


def _dispatch(u, rows_tiled, n_rows, td):
    t, d = u.shape
    n_pick = rows_tiled.shape[2] // td
    return pl.pallas_call(
        functools.partial(_dispatch_kernel, n_pick=n_pick),
        out_shape=jax.ShapeDtypeStruct((n_rows, d), F32),
        grid=(t // td,),
        in_specs=[
            pl.BlockSpec((None, 1, n_pick * td), lambda i: (i, 0, 0), memory_space=pltpu.SMEM),
            pl.BlockSpec((td, d), lambda i: (i, 0)),
        ],
        out_specs=pl.BlockSpec(memory_space=pl.ANY),
        scratch_shapes=[pltpu.SemaphoreType.DMA(())],
        compiler_params=_cparams("arbitrary"),
        name="moe_dispatch",
    )(rows_tiled, u)


def _experts_kernel(tile_ref, exp_ref, lo_ref, hi_ref, first_ref, n_ref, x_ref, wgu_ref, wd_ref, y_ref):
    p = pl.program_id(0)

    @pl.when(p < n_ref[0])
    def _():
        ff = wd_ref.shape[0]
        h = _dot(x_ref[...].astype(BF16), wgu_ref[...])
        hg = h[:, :ff]
        hu = h[:, ff:]
        act = (hg * _sigmoid(hg) * hu).astype(BF16)
        row = lax.broadcasted_iota(jnp.int32, (x_ref.shape[0], 1), 0)
        y = jnp.where((row >= lo_ref[p]) & (row < hi_ref[p]), _dot(act, wd_ref[...]), 0.0)

        @pl.when(first_ref[p] == 1)
        def _():
            y_ref[...] = y

        @pl.when(first_ref[p] == 0)
        def _():
            y_ref[...] += y


def _experts(x_sorted, work, w_gu, w_down, tm):
    n_rows, d = x_sorted.shape
    ne1, _, ff2 = w_gu.shape
    ff = ff2 // 2
    n_work_max = n_rows // tm + ne1 - 1
    return pl.pallas_call(
        _experts_kernel,
        out_shape=jax.ShapeDtypeStruct((n_rows, d), F32),
        grid_spec=pltpu.PrefetchScalarGridSpec(
            num_scalar_prefetch=6,
            grid=(n_work_max,),
            in_specs=[
                pl.BlockSpec((tm, d), lambda p, tile, exp, *_: (tile[p], 0)),
                pl.BlockSpec((None, d, ff2), lambda p, tile, exp, *_: (exp[p], 0, 0)),
                pl.BlockSpec((None, ff, d), lambda p, tile, exp, *_: (exp[p], 0, 0)),
            ],
            out_specs=pl.BlockSpec((tm, d), lambda p, tile, exp, *_: (tile[p], 0)),
        ),
        compiler_params=_cparams("arbitrary"),
        name="moe_experts",
    )(*work, x_sorted, w_gu, w_down)


def _expert_work_list(counts, n_rows, tm):
    ne1 = counts.shape[0]
    n_work_max = n_rows // tm + ne1 - 1
    ends = jnp.cumsum(counts)
    starts = ends - counts
    first_tile = starts // tm
    n_tile = jnp.where(counts > 0, (ends - 1) // tm - first_tile + 1, 0)
    work_end = jnp.cumsum(n_tile)
    n_work = work_end[-1:]
    p = jnp.minimum(jnp.arange(n_work_max, dtype=jnp.int32), n_work - 1)
    exp = jnp.searchsorted(work_end, p, side="right").astype(jnp.int32)
    tile = first_tile[exp] + p - (work_end - n_tile)[exp]
    lo = jnp.maximum(starts[exp] - tile * tm, 0)
    hi = jnp.minimum(ends[exp] - tile * tm, tm)
    first = jnp.concatenate([jnp.ones((1,), jnp.int32), (tile[1:] != tile[:-1]).astype(jnp.int32)])
    as_i32 = lambda a: a.astype(jnp.int32)
    return tuple(map(as_i32, (tile, exp, lo, hi, first, n_work))), starts


def _combine_kernel(rows_ref, g_ref, x_ref, gain_ref, gate_ref, y_hbm, o_ref, buf, sem, *, n_pick):
    tt = x_ref.shape[0]

    def body(tok, carry):
        for k in range(n_pick):
            _row_copy(y_hbm, rows_ref[0, k * tt + tok], buf.at[k], tok, sem).start()
        return carry

    lax.fori_loop(0, tt, body, 0)
    for k in range(n_pick):
        pltpu.make_async_copy(y_hbm.at[pl.ds(0, tt), :], buf.at[k], sem).wait()
    g = g_ref[...]
    acc = buf[0] * g[:, 0:1]
    for k in range(1, n_pick):
        acc = acc + buf[k] * g[:, k:k + 1]
    yn = acc * lax.rsqrt(jnp.mean(acc * acc, axis=-1, keepdims=True) + RMS_EPS) * gain_ref[...]
    o_ref[...] = x_ref[...] + gate_ref[...] * yn


def _combine(y_sorted, rows_tiled, gates, x2, gain, gate, seq, tt):
    t, d = x2.shape
    n_pick = rows_tiled.shape[2] // tt
    bmap = lambda i: ((i * tt) // seq, 0, 0)
    return pl.pallas_call(
        functools.partial(_combine_kernel, n_pick=n_pick),
        out_shape=jax.ShapeDtypeStruct((t, d), F32),
        grid=(t // tt,),
        in_specs=[
            pl.BlockSpec((None, 1, n_pick * tt), lambda i: (i, 0, 0), memory_space=pltpu.SMEM),
            pl.BlockSpec((tt, gates.shape[1]), lambda i: (i, 0)),
            pl.BlockSpec((tt, d), lambda i: (i, 0)),
            pl.BlockSpec((1, d), lambda i: (0, 0)),
            pl.BlockSpec((None, 1, d), bmap),
            pl.BlockSpec(memory_space=pl.ANY),
        ],
        out_specs=pl.BlockSpec((tt, d), lambda i: (i, 0)),
        scratch_shapes=[pltpu.VMEM((n_pick, tt, d), F32), pltpu.SemaphoreType.DMA(())],
        compiler_params=_cparams("arbitrary"),
        name="moe_combine",
    )(rows_tiled, gates, x2, gain, gate, y_sorted)


def _tile_rows(rows, tile):
    n_pick, t = rows.shape
    return rows.reshape(n_pick, t // tile, tile).transpose(1, 0, 2).reshape(t // tile, 1, n_pick * tile)


def _moe_block(x2, gain_in, shift, scale, rw_t, rb, w_gu, w_down, gain_out, gate, seq,
               tm=256, td=128, tt=64):
    t, d = x2.shape
    ne = rw_t.shape[0]
    u, idx, pos, gate_k, counts = _router(x2, gain_in, shift, scale, rw_t, rb, seq)
    n_pick = TOP_K + 1
    n_rows = n_pick * t
    counts = jnp.concatenate([counts[:, 0].astype(jnp.int32), jnp.full((1,), t, jnp.int32)])
    work, starts = _expert_work_list(counts, n_rows, tm)
    rows = jnp.concatenate([starts[idx] + pos, (starts[ne] + jnp.arange(t, dtype=jnp.int32))[None, :]], axis=0)
    gates = jnp.concatenate([gate_k, jnp.ones((1, t), F32), jnp.zeros((16 - n_pick, t), F32)], axis=0).T
    x_sorted = _dispatch(u, _tile_rows(rows, td), n_rows, td)
    y_sorted = _experts(x_sorted, work, w_gu, w_down, tm)
    return _combine(y_sorted, _tile_rows(rows, tt), gates, x2, gain_out, gate, seq, tt)


def _pad_cols(w, n):
    return jnp.pad(w, ((0, 0), (0, n - w.shape[1])))


def _pad_rows(w, n):
    return jnp.pad(w, ((0, n - w.shape[0]), (0, 0)))


def _layout_rwkv_cols(w, rw, dl, al, gl, lp, glp):
    o = 3 * rw
    parts = [w[:, :o], _pad_cols(w[:, o:o + dl], lp), _pad_cols(w[:, o + dl:o + dl + al], lp),
             _pad_cols(w[:, o + dl + al:o + dl + al + gl], glp)]
    out = jnp.concatenate(parts, axis=1)
    return _pad_cols(out, _round_up(out.shape[1], 512))


def _layout_in_cols(w, rw, dl, al, gl, aw, n_kv, lp, glp):
    parts = [_layout_rwkv_cols(w, rw, dl, al, gl, lp, glp)]
    a0 = 3 * rw + dl + al + gl
    parts.append(w[:, a0:a0 + aw])
    for base in (a0 + aw, a0 + aw + n_kv * HEAD_DIM):
        for j in range(n_kv):
            kv = w[:, base + j * HEAD_DIM:base + (j + 1) * HEAD_DIM]
            parts += [kv, kv]
    out = jnp.concatenate(parts, axis=1)
    return _pad_cols(out, _round_up(out.shape[1], 512))


def kernel(x, c, rel_bias, ada_w, ada_b, norm_gains, w_in, shift_mu, rwkv_w0, rwkv_w2, rwkv_a0, rwkv_a2, rwkv_g2,
           rwkv_k_k, rwkv_k_a, rwkv_r_k, rwkv_ln_w, rwkv_ln_b, attn_sinks, w_out, router_w, router_bias,
           expert_w_gu, expert_w_down, shared_w_gu, shared_w_down):
    batch, seq, d = x.shape
    depth = ada_w.shape[0]
    t = batch * seq
    rw = rwkv_w0.shape[1]
    dl, al, gl = rwkv_w2.shape[1], rwkv_a2.shape[1], rwkv_g2.shape[1]
    aw = w_out.shape[1] - rw
    n_q = aw // HEAD_DIM
    n_kv = (w_in.shape[2] - (3 * rw + dl + al + gl) - aw) // (2 * HEAD_DIM)
    q_per_kv = n_q // n_kv
    ne = router_w.shape[2]
    lp = _round_up(max(dl, al), LANES)
    glp = _round_up(gl, LANES)
    q_off = _round_up(3 * rw + 2 * lp + glp, 512)

    c_pad = jnp.pad(c, ((0, 8 - batch % 8 if batch % 8 else 0), (0, 0)))
    ada = _ada(c_pad, ada_w, ada_b)[:, :batch]
    bias = _bias_table(rel_bias)

    x2 = x.reshape(t, d)
    for l in range(depth):
        sh_m, sc_m, gt_m, sh_f, sc_f, gt_f = [a.reshape(batch, 1, d) for a in jnp.split(ada[l], 6, axis=-1)]
        gains = norm_gains[l].reshape(4, 1, d)

        w_in_l = _layout_in_cols(w_in[l], rw, dl, al, gl, aw, n_kv, lp, glp).astype(BF16)
        mu_l = _layout_rwkv_cols(shift_mu[l][None, :], rw, dl, al, gl, lp, glp)
        proj = _in_proj(x2, gains[0], sh_m, sc_m, w_in_l, seq)

        r, k, v, a, g, lw = _rwkv_prep(
            proj, mu_l, rwkv_w0[l][None, :], _pad_rows(rwkv_w2[l], lp).astype(BF16), rwkv_a0[l][None, :],
            _pad_rows(rwkv_a2[l], lp).astype(BF16), _pad_rows(rwkv_g2[l], glp).astype(BF16), rw, lp, glp, seq)
        y_r = _rwkv_chunk(r, k, v, a, g, lw, rwkv_k_k[l][None, :], rwkv_k_a[l][None, :],
                          rwkv_r_k[l].reshape(1, rw), rwkv_ln_w[l][None, :], rwkv_ln_b[l][None, :], batch, seq)
        y_a = _swa(proj, attn_sinks[l], bias, batch, seq, q_off, n_kv, q_per_kv)
        x2 = _out_proj(y_r, y_a, w_out[l].astype(BF16), x2, gains[1], gt_m, seq)

        w_gu_all = jnp.concatenate([expert_w_gu[l].astype(BF16), shared_w_gu[l][None].astype(BF16)], axis=0)
        w_down_all = jnp.concatenate([expert_w_down[l].astype(BF16), shared_w_down[l][None].astype(BF16)], axis=0)
        x2 = _moe_block(x2, gains[2], sh_f, sc_f, router_w[l].T, router_bias[l][:, None], w_gu_all, w_down_all,
                        gains[3], gt_f, seq)
    return x2.reshape(batch, seq, d)
```

```python
import functools
import math

import numpy as np
import jax
import jax.numpy as jnp
from jax import lax
from jax.experimental import pallas as pl
from jax.experimental.pallas import tpu as pltpu

F32 = jnp.float32
BF16 = jnp.bfloat16

HEAD_DIM = 64
LANES = 128
WINDOW = 128
N_BUCKETS = 32
MAX_DISTANCE = 128
N_EXPERT_GROUPS = 8
TOPK_GROUPS = 4
TOP_K = 8
ROUTED_SCALE = 2.5
RMS_EPS = 1e-6
GN_EPS = 64e-5
CHUNK = 64
HEADS_PER_GROUP = 4
GROUP_LANES = HEADS_PER_GROUP * HEAD_DIM
VMEM_LIMIT = 56 * 1024 * 1024

HIGHEST = lax.Precision.HIGHEST


def _round_up(n, m):
    return (n + m - 1) // m * m


def _cparams(*sem):
    return pltpu.CompilerParams(dimension_semantics=sem, vmem_limit_bytes=VMEM_LIMIT)


def _dot(a, b):
    return jnp.dot(a, b, preferred_element_type=F32)


def _dot_nt(a, b):
    return lax.dot_general(a, b, (((1,), (1,)), ((), ())), preferred_element_type=F32)


def _dot_tn(a, b):
    return lax.dot_general(a, b, (((0,), (0,)), ((), ())), preferred_element_type=F32)


def _bdot(a, b):
    return lax.dot_general(a, b, (((2,), (1,)), ((0,), (0,))), preferred_element_type=F32)


def _bdot_nt(a, b):
    return lax.dot_general(a, b, (((2,), (2,)), ((0,), (0,))), preferred_element_type=F32)


def _sigmoid(x):
    return 1.0 / (1.0 + jnp.exp(-x))


def _norm_mod(x, gain, shift, scale):
    y = x * lax.rsqrt(jnp.mean(x * x, axis=-1, keepdims=True) + RMS_EPS) * gain
    return y * (1.0 + scale) + shift


def _ada_kernel(c_ref, w_ref, b_ref, o_ref):
    c = c_ref[...]
    cond = c * _sigmoid(c)
    o_ref[...] = _dot(cond.astype(BF16), w_ref[...].astype(BF16)) + b_ref[...]


def _ada(c_pad, ada_w, ada_b):
    nl, d, n = ada_w.shape
    tn = 512
    return pl.pallas_call(
        _ada_kernel,
        out_shape=jax.ShapeDtypeStruct((nl, c_pad.shape[0], n), F32),
        grid=(nl, n // tn),
        in_specs=[
            pl.BlockSpec((c_pad.shape[0], d), lambda l, j: (0, 0)),
            pl.BlockSpec((None, d, tn), lambda l, j: (l, 0, j)),
            pl.BlockSpec((None, 1, tn), lambda l, j: (l, 0, j)),
        ],
        out_specs=pl.BlockSpec((None, c_pad.shape[0], tn), lambda l, j: (l, 0, j)),
        compiler_params=_cparams("parallel", "parallel"),
        name="ada",
    )(c_pad, ada_w, ada_b.reshape(nl, 1, n))


def _bias_kernel(rbt_ref, oh_ref, o_ref):
    o_ref[...] = jnp.dot(rbt_ref[...], oh_ref[...], preferred_element_type=F32, precision=HIGHEST)


def _bucket_onehot():
    qi = np.arange(WINDOW)[:, None]
    kj = np.arange(2 * WINDOW)[None, :]
    dist = qi + WINDOW - kj
    max_exact = N_BUCKETS // 2
    d = np.maximum(dist, 0)
    far = max_exact + (np.log(np.maximum(d, max_exact).astype(np.float32) / np.float32(max_exact))
                       / np.float32(math.log(MAX_DISTANCE / max_exact))
                       * np.float32(N_BUCKETS - max_exact)).astype(np.int32)
    bucket = np.where(d < max_exact, d, np.minimum(far, N_BUCKETS - 1)).reshape(-1)
    return (np.arange(N_BUCKETS)[:, None] == bucket[None, :]).astype(np.float32)


def _bias_table(rel_bias):
    nb, nh = rel_bias.shape
    onehot = jnp.asarray(_bucket_onehot())
    out = pl.pallas_call(
        _bias_kernel,
        out_shape=jax.ShapeDtypeStruct((nh, WINDOW * 2 * WINDOW), F32),
        name="bias_table",
        compiler_params=pltpu.CompilerParams(vmem_limit_bytes=VMEM_LIMIT),
    )(rel_bias.T, onehot)
    return out.reshape(nh, WINDOW, 2 * WINDOW)


def _in_proj_kernel(x_ref, gain_ref, sh_ref, sc_ref, w_ref, o_ref, u_ref):
    @pl.when(pl.program_id(1) == 0)
    def _():
        u_ref[...] = _norm_mod(x_ref[...], gain_ref[...], sh_ref[...], sc_ref[...]).astype(BF16)

    o_ref[...] = _dot(u_ref[...], w_ref[...]).astype(o_ref.dtype)


def _in_proj(x2, gain, shift, scale, w, seq, tm=512, tn=512):
    t, d = x2.shape
    n = w.shape[1]
    bmap = lambda i, j: ((i * tm) // seq, 0, 0)
    return pl.pallas_call(
        _in_proj_kernel,
        out_shape=jax.ShapeDtypeStruct((t, n), F32),
        grid=(t // tm, n // tn),
        in_specs=[
            pl.BlockSpec((tm, d), lambda i, j: (i, 0)),
            pl.BlockSpec((1, d), lambda i, j: (0, 0)),
            pl.BlockSpec((None, 1, d), bmap),
            pl.BlockSpec((None, 1, d), bmap),
            pl.BlockSpec((d, tn), lambda i, j: (0, j)),
        ],
        out_specs=pl.BlockSpec((tm, tn), lambda i, j: (i, j)),
        scratch_shapes=[pltpu.VMEM((tm, d), BF16)],
        compiler_params=_cparams("parallel", "arbitrary"),
        name="in_proj",
    )(x2, gain, shift, scale, w)


def _rwkv_prep_kernel(rw, lp, glp, p_ref, prev_ref, mu_ref, w0_ref, w2_ref, a0_ref, a2_ref, g2_ref,
                      r_ref, k_ref, v_ref, a_ref, g_ref, lw_ref, *, blocks_per_seq):
    i = pl.program_id(0)
    cur = p_ref[...]
    ts = cur.shape[0]
    first = (i % blocks_per_seq) == 0
    prev_row = jnp.where(first, 0.0, prev_ref[7:8, :])
    rolled = pltpu.roll(cur, 1, axis=0)
    row = lax.broadcasted_iota(jnp.int32, cur.shape, 0)
    shifted = jnp.where(row == 0, prev_row, rolled)
    x = cur + (shifted - cur) * mu_ref[...]
    r_ref[...] = x[:, 0:rw].astype(r_ref.dtype)
    k_ref[...] = x[:, rw:2 * rw].astype(k_ref.dtype)
    v_ref[...] = x[:, 2 * rw:3 * rw].astype(v_ref.dtype)
    o = 3 * rw
    xw = x[:, o:o + lp]
    xa = x[:, o + lp:o + 2 * lp]
    xg = x[:, o + 2 * lp:o + 2 * lp + glp]
    z = w0_ref[...] + _dot(jnp.tanh(xw).astype(BF16), w2_ref[...])
    lw_ref[...] = -math.exp(-0.5) * _sigmoid(z)
    a_ref[...] = _sigmoid(a0_ref[...] + _dot(xa.astype(BF16), a2_ref[...])).astype(a_ref.dtype)
    g_ref[...] = _dot(_sigmoid(xg).astype(BF16), g2_ref[...]).astype(g_ref.dtype)


def _rwkv_prep(proj, mu, w0, w2, a0, a2, g2, rw, lp, glp, seq, ts=256):
    t = proj.shape[0]
    width = 3 * rw + 2 * lp + glp
    wblock = _round_up(width, 512)
    full = lambda i: (0, 0)
    outs = [jax.ShapeDtypeStruct((t, rw), BF16)] * 5 + [jax.ShapeDtypeStruct((t, rw), F32)]
    ospec = pl.BlockSpec((ts, rw), lambda i: (i, 0))
    return pl.pallas_call(
        functools.partial(_rwkv_prep_kernel, rw, lp, glp, blocks_per_seq=seq // ts),
        out_shape=outs,
        grid=(t // ts,),
        in_specs=[
            pl.BlockSpec((ts, wblock), lambda i: (i, 0)),
            pl.BlockSpec((8, wblock), lambda i: (jnp.maximum(i * (ts // 8) - 1, 0), 0)),
            pl.BlockSpec((1, wblock), full),
            pl.BlockSpec((1, rw), full),
            pl.BlockSpec((lp, rw), full),
            pl.BlockSpec((1, rw), full),
            pl.BlockSpec((lp, rw), full),
            pl.BlockSpec((glp, rw), full),
        ],
        out_specs=[ospec] * 6,
        compiler_params=_cparams("parallel"),
        name="rwkv_prep",
    )(proj, proj, mu, w0, w2, a0, a2, g2)


CHUNK_LEVELS = int(math.log2(CHUNK))


def _chunk_constants():
    n = GROUP_LANES
    i = np.arange(n)[:, None]
    j = np.arange(n)[None, :]
    same_head = (i // CHUNK) == (j // CHUNK)
    code = np.zeros((n, n), np.int32)
    for lvl in range(CHUNK_LEVELS, 0, -1):
        code = np.where(same_head & (i > j) & ((i >> lvl) == (j >> lvl)), lvl, code)
    planes = [code == lvl for lvl in range(1, CHUNK_LEVELS + 1)] + [code > 0, same_head]
    return np.stack(planes).astype(np.float32)


def _seg_sum(x, ones_bd):
    hi = x.astype(BF16)
    lo = (x - hi.astype(F32)).astype(BF16)
    return _dot(hi, ones_bd) + _dot(lo, ones_bd)


def _rwkv_chunk_kernel(r_ref, k_ref, v_ref, a_ref, g_ref, lw_ref,
                       kk_ref, ka_ref, rk_ref, lnw_ref, lnb_ref, planes_ref,
                       o_ref, s_ref):
    @pl.when(pl.program_id(2) == 0)
    def _():
        s_ref[...] = jnp.zeros_like(s_ref)

    n = GROUP_LANES
    lvl_mask = [planes_ref[l] for l in range(CHUNK_LEVELS)]
    low_mask = planes_ref[CHUNK_LEVELS]
    bd = planes_ref[CHUNK_LEVELS + 1]
    tb = r_ref.shape[0]
    nch = tb // CHUNK
    tri_incl = (lax.broadcasted_iota(jnp.int32, (1, CHUNK, n), 1)
                >= lax.broadcasted_iota(jnp.int32, (1, CHUNK, n), 2) % CHUNK)
    ti = lax.broadcasted_iota(jnp.int32, (tb, tb), 0)
    tj = lax.broadcasted_iota(jnp.int32, (tb, tb), 1)
    ltri = jnp.where((ti >= tj) & (ti // CHUNK == tj // CHUNK), 1.0, 0.0)
    eye = jnp.where(lax.broadcasted_iota(jnp.int32, (1, n, n), 1)
                    == lax.broadcasted_iota(jnp.int32, (1, n, n), 2), 1.0, 0.0).astype(BF16)

    def chunks(x):
        return x.reshape(nch, CHUNK, n)

    def tile_heads(x):
        return jnp.concatenate([x] * HEADS_PER_GROUP, axis=1)

    def stack(x):
        return tile_heads(x.astype(BF16)) * bd[None]

    r = r_ref[...].astype(F32)
    k = k_ref[...].astype(F32)
    v = v_ref[...].astype(F32)
    a = a_ref[...].astype(F32)
    lw = lw_ref[...]
    kkp = k * kk_ref[...]
    kk = kkp / jnp.maximum(jnp.sqrt(_seg_sum(kkp * kkp, bd)), 1e-12)
    k2 = k * (1.0 + (a - 1.0) * ka_ref[...])
    kb = kk * a
    bonus = _seg_sum(r * k2 * rk_ref[...], bd) * v

    cum = chunks(jnp.dot(ltri, lw, preferred_element_type=F32, precision=HIGHEST))
    cum_end = cum[:, CHUNK - 1:CHUNK, :]
    e_in = jnp.exp(cum)
    e_ex = jnp.exp(cum - chunks(lw))
    e_neg = jnp.exp(-cum)
    e_dec = jnp.exp(cum_end - cum)
    decay = jnp.exp(cum_end)
    r_t = chunks(r) * e_in
    a_t = -chunks(kk) * e_ex
    kb3 = chunks(kb)
    k23 = chunks(k2)

    ar = jnp.concatenate([a_t, r_t], axis=1).astype(BF16)
    v_s = stack(chunks(v))
    ab_rb = _bdot_nt(ar, stack(kb3 * e_neg))
    ak_rk = _bdot_nt(ar, stack(k23 * e_neg))
    rb_w = jnp.where(tri_incl, ab_rb[:, CHUNK:], 0.0).astype(BF16)
    rk_w = jnp.where(tri_incl, ak_rk[:, CHUNK:], 0.0).astype(BF16)
    amat = tile_heads(ab_rb[:, :CHUNK].astype(BF16))
    akmat = tile_heads(ak_rk[:, :CHUNK].astype(BF16)) * low_mask[None]

    tinv = eye + amat * lvl_mask[0][None]
    for lvl in range(1, CHUNK_LEVELS):
        tinv = tinv + _bdot(tinv, _bdot(amat * lvl_mask[lvl][None], tinv).astype(BF16)).astype(BF16)

    w_s = _bdot(tinv, stack(a_t)).astype(BF16)
    z_s = _bdot(tinv, _bdot(akmat, v_s).astype(BF16)).astype(BF16)
    bh_s = stack(kb3 * e_dec)
    kh_s = stack(k23 * e_dec)
    r_eff = (r_t + _bdot(rb_w, w_s)).astype(BF16)
    y0 = _bdot(rb_w, z_s) + _bdot(rk_w, v_s)

    s = s_ref[...]
    ys = []
    for ci in range(nch):
        p_mat = _dot_tn(w_s[ci], bh_s[ci]).astype(BF16)
        q_mat = _dot_tn(z_s[ci], bh_s[ci]) + _dot_tn(v_s[ci], kh_s[ci])
        s_b = s.astype(BF16)
        ys.append(_dot_nt(r_eff[ci], s_b) + y0[ci])
        s = s * decay[ci] + _dot(s_b, p_mat) + q_mat
    s_ref[...] = s

    y = jnp.concatenate(ys, axis=0)
    inv_n = 1.0 / HEAD_DIM
    mean = _seg_sum(y, bd) * inv_n
    dlt = y - mean
    var = _seg_sum(dlt * dlt, bd) * inv_n
    yn = dlt * lax.rsqrt(var + GN_EPS) * lnw_ref[...] + lnb_ref[...]
    o_ref[...] = ((yn + bonus) * g_ref[...].astype(F32)).astype(o_ref.dtype)


def _rwkv_chunk(r, k, v, a, g, lw, k_k, k_a, r_k, ln_w, ln_b, batch, seq, tb=512):
    t, rw = r.shape
    n = GROUP_LANES
    planes = jnp.asarray(_chunk_constants(), dtype=BF16)
    nblk = seq // tb
    tmap = lambda b, h, c: (b * nblk + c, h)
    pmap = lambda b, h, c: (0, h)
    tspec = pl.BlockSpec((tb, n), tmap)
    pspec = pl.BlockSpec((1, n), pmap)
    return pl.pallas_call(
        _rwkv_chunk_kernel,
        out_shape=jax.ShapeDtypeStruct((t, rw), BF16),
        grid=(batch, rw // n, nblk),
        in_specs=[tspec] * 6 + [pspec] * 5 + [pl.BlockSpec(planes.shape, lambda b, h, c: (0, 0, 0))],
        out_specs=pl.BlockSpec((tb, n), tmap),
        scratch_shapes=[pltpu.VMEM((n, n), F32)],
        compiler_params=_cparams("parallel", "parallel", "arbitrary"),
        name="rwkv_chunk",
    )(r, k, v, a, g, lw, k_k, k_a, r_k, ln_w, ln_b, planes)


def _swa_kernel(sink_ref, q_ref, kp_ref, kc_ref, vp_ref, vc_ref, bias_ref, o_ref, *, q_per_kv):
    j = pl.program_id(1)
    nblk = pl.program_id(2)
    blk = WINDOW
    kw = jnp.concatenate([kp_ref[...], kc_ref[...]], axis=0).astype(BF16)
    vw = jnp.concatenate([vp_ref[...], vc_ref[...]], axis=0)
    lane = lax.broadcasted_iota(jnp.int32, (1, LANES), 1)
    lo = lane < HEAD_DIM
    v_half = (jnp.where(lo, vw, 0.0).astype(BF16), jnp.where(lo, 0.0, vw).astype(BF16))
    qi = lax.broadcasted_iota(jnp.int32, (blk, 2 * blk), 0)
    kj = lax.broadcasted_iota(jnp.int32, (blk, 2 * blk), 1)
    valid = (kj > qi) & (kj <= qi + blk) & ((nblk > 0) | (kj >= blk))
    scale = HEAD_DIM ** -0.5
    for p in range(q_per_kv // 2):
        qp = q_ref[:, p * LANES:(p + 1) * LANES] * scale
        q_half = (jnp.where(lo, qp, 0.0).astype(BF16), jnp.where(lo, 0.0, qp).astype(BF16))
        acc = None
        for half in range(2):
            g = 2 * p + half
            sink = sink_ref[j * q_per_kv + g]
            s = _dot_nt(q_half[half], kw) + bias_ref[g]
            s = jnp.where(valid, s, -jnp.inf)
            m = jnp.maximum(jnp.max(s, axis=-1, keepdims=True), sink)
            pr = jnp.exp(s - m)
            denom = jnp.sum(pr, axis=-1, keepdims=True) + jnp.exp(sink - m)
            o = _dot(pr.astype(BF16), v_half[half]) / denom
            acc = o if acc is None else acc + o
        o_ref[:, p * LANES:(p + 1) * LANES] = acc.astype(o_ref.dtype)


def _swa(proj, sinks, bias, batch, seq, q_off, n_kv, q_per_kv):
    t = proj.shape[0]
    blk = WINDOW
    nb = seq // blk
    qw = q_per_kv * HEAD_DIM
    k_off = q_off + n_kv * qw
    v_off = k_off + n_kv * LANES
    cur = lambda b, j, n: b * nb + n
    prev = lambda b, j, n: b * nb + jnp.maximum(n - 1, 0)
    return pl.pallas_call(
        functools.partial(_swa_kernel, q_per_kv=q_per_kv),
        out_shape=jax.ShapeDtypeStruct((t, n_kv * qw), BF16),
        grid=(batch, n_kv, nb),
        in_specs=[
            pl.BlockSpec(memory_space=pltpu.SMEM),
            pl.BlockSpec((blk, qw), lambda b, j, n: (cur(b, j, n), q_off // qw + j)),
            pl.BlockSpec((blk, LANES), lambda b, j, n: (prev(b, j, n), k_off // LANES + j)),
            pl.BlockSpec((blk, LANES), lambda b, j, n: (cur(b, j, n), k_off // LANES + j)),
            pl.BlockSpec((blk, LANES), lambda b, j, n: (prev(b, j, n), v_off // LANES + j)),
            pl.BlockSpec((blk, LANES), lambda b, j, n: (cur(b, j, n), v_off // LANES + j)),
            pl.BlockSpec((q_per_kv, blk, 2 * blk), lambda b, j, n: (j, 0, 0)),
        ],
        out_specs=pl.BlockSpec((blk, qw), lambda b, j, n: (cur(b, j, n), j)),
        compiler_params=_cparams("parallel", "parallel", "parallel"),
        name="swa",
    )(sinks, proj, proj, proj, proj, proj, bias)


def _out_proj_kernel(yr_ref, ya_ref, w_ref, x_ref, gain_ref, gate_ref, o_ref, y_ref, *, tn):
    j = pl.program_id(1)
    rw = yr_ref.shape[1]
    y = _dot(yr_ref[...], w_ref[0:rw, :]) + _dot(ya_ref[...], w_ref[rw:, :])
    y_ref[:, pl.ds(pl.multiple_of(j * tn, tn), tn)] = y

    @pl.when(j == pl.num_programs(1) - 1)
    def _():
        yy = y_ref[...]
        yn = yy * lax.rsqrt(jnp.mean(yy * yy, axis=-1, keepdims=True) + RMS_EPS) * gain_ref[...]
        o_ref[...] = x_ref[...] + gate_ref[...] * yn


def _out_proj(y_r, y_a, w, x2, gain, gate, seq, tm=256, tn=512):
    t, d = x2.shape
    rw = y_r.shape[1]
    aw = y_a.shape[1]
    bmap = lambda i, j: ((i * tm) // seq, 0, 0)
    return pl.pallas_call(
        functools.partial(_out_proj_kernel, tn=tn),
        out_shape=jax.ShapeDtypeStruct((t, d), F32),
        grid=(t // tm, d // tn),
        in_specs=[
            pl.BlockSpec((tm, rw), lambda i, j: (i, 0)),
            pl.BlockSpec((tm, aw), lambda i, j: (i, 0)),
            pl.BlockSpec((rw + aw, tn), lambda i, j: (0, j)),
            pl.BlockSpec((tm, d), lambda i, j: (i, 0)),
            pl.BlockSpec((1, d), lambda i, j: (0, 0)),
            pl.BlockSpec((None, 1, d), bmap),
        ],
        out_specs=pl.BlockSpec((tm, d), lambda i, j: (i, 0)),
        scratch_shapes=[pltpu.VMEM((tm, d), F32)],
        compiler_params=_cparams("parallel", "arbitrary"),
        name="out_proj",
    )(y_r, y_a, w, x2, gain, gate)


def _rank(vals):
    n = vals.shape[0]
    idx = lax.broadcasted_iota(jnp.int32, vals.shape, 0)
    rank = jnp.zeros(vals.shape, jnp.int32)
    for e in range(n):
        row = vals[e:e + 1, :]
        beats = (row > vals) | ((row == vals) & (e < idx))
        rank = rank + beats.astype(jnp.int32)
    return rank


def _router_kernel(x_ref, gain_ref, sh_ref, sc_ref, rwt_ref, rb_ref,
                   u_ref, idx_ref, pos_ref, gate_ref, cnt_ref, carry_ref):
    @pl.when(pl.program_id(0) == 0)
    def _():
        carry_ref[...] = jnp.zeros_like(carry_ref)

    u = _norm_mod(x_ref[...], gain_ref[...], sh_ref[...], sc_ref[...])
    u_ref[...] = u
    logits = lax.dot_general(rwt_ref[...], u, (((1,), (1,)), ((), ())),
                             preferred_element_type=F32, precision=HIGHEST)
    scores = _sigmoid(logits)
    sel = scores + rb_ref[:, 0:1]
    ne, tm = sel.shape
    gsz = ne // N_EXPERT_GROUPS
    sub = lax.broadcasted_iota(jnp.int32, (gsz, tm), 0)
    gs_rows = []
    for gi in range(N_EXPERT_GROUPS):
        grp = sel[gi * gsz:(gi + 1) * gsz, :]
        m1 = jnp.max(grp, axis=0, keepdims=True)
        i1 = jnp.min(jnp.where(grp == m1, sub, gsz), axis=0, keepdims=True)
        m2 = jnp.max(jnp.where(sub == i1, -jnp.inf, grp), axis=0, keepdims=True)
        gs_rows.append(m1 + m2)
    gkeep = _rank(jnp.concatenate(gs_rows, axis=0)) < TOPK_GROUPS
    ekeep = jnp.concatenate(
        [jnp.broadcast_to(gkeep[gi:gi + 1, :], (gsz, tm)) for gi in range(N_EXPERT_GROUPS)], axis=0)
    rank = _rank(jnp.where(ekeep, sel, -jnp.inf))
    chosen = rank < TOP_K
    wts = jnp.where(chosen, scores, 0.0)
    gates_t = wts / jnp.sum(wts, axis=0, keepdims=True) * ROUTED_SCALE

    cf = jnp.where(chosen, 1.0, 0.0)
    before = (lax.broadcasted_iota(jnp.int32, (tm, tm), 0) < lax.broadcasted_iota(jnp.int32, (tm, tm), 1))
    carry = carry_ref[:, 0:1]
    pos = carry + _dot(cf.astype(BF16), jnp.where(before, 1.0, 0.0).astype(BF16))
    carry_ref[...] = jnp.broadcast_to(carry + jnp.sum(cf, axis=1, keepdims=True), carry_ref.shape)
    cnt_ref[...] = carry_ref[...]

    eidx = lax.broadcasted_iota(jnp.int32, (ne, tm), 0)
    rows_i, rows_p, rows_g = [], [], []
    for k in range(TOP_K):
        mk = rank == k
        rows_i.append(jnp.sum(jnp.where(mk, eidx, 0), axis=0, keepdims=True))
        rows_p.append(jnp.sum(jnp.where(mk, pos, 0.0), axis=0, keepdims=True))
        rows_g.append(jnp.sum(jnp.where(mk, gates_t, 0.0), axis=0, keepdims=True))
    idx_ref[...] = jnp.concatenate(rows_i, axis=0)
    pos_ref[...] = jnp.concatenate(rows_p, axis=0).astype(jnp.int32)
    gate_ref[...] = jnp.concatenate(rows_g, axis=0)


def _router(x2, gain, shift, scale, rw_t, rb, seq, tm=256):
    t, d = x2.shape
    ne = rw_t.shape[0]
    bmap = lambda i: ((i * tm) // seq, 0, 0)
    pick = lambda dt: jax.ShapeDtypeStruct((TOP_K, t), dt)
    pspec = pl.BlockSpec((TOP_K, tm), lambda i: (0, i))
    return pl.pallas_call(
        _router_kernel,
        out_shape=[jax.ShapeDtypeStruct((t, d), F32), pick(jnp.int32), pick(jnp.int32), pick(F32),
                   jax.ShapeDtypeStruct((ne, LANES), F32)],
        grid=(t // tm,),
        in_specs=[
            pl.BlockSpec((tm, d), lambda i: (i, 0)),
            pl.BlockSpec((1, d), lambda i: (0, 0)),
            pl.BlockSpec((None, 1, d), bmap),
            pl.BlockSpec((None, 1, d), bmap),
            pl.BlockSpec((ne, d), lambda i: (0, 0)),
            pl.BlockSpec((ne, 1), lambda i: (0, 0)),
        ],
        out_specs=[pl.BlockSpec((tm, d), lambda i: (i, 0)), pspec, pspec, pspec,
                   pl.BlockSpec((ne, LANES), lambda i: (0, 0))],
        scratch_shapes=[pltpu.VMEM((ne, LANES), F32)],
        compiler_params=_cparams("arbitrary"),
        name="router",
    )(x2, gain, shift, scale, rw_t, rb)


def _row_copy(src, src_row, dst, dst_row, sem):
    return pltpu.make_async_copy(src.at[pl.ds(src_row, 1), :], dst.at[pl.ds(dst_row, 1), :], sem)


def _dispatch_kernel(rows_ref, u_ref, x_hbm, sem, *, n_pick):
    td = u_ref.shape[0]

    def body(tok, carry):
        for k in range(n_pick):
            _row_copy(u_ref, tok, x_hbm, rows_ref[0, k * td + tok], sem).start()
        return carry

    lax.fori_loop(0, td, body, 0)
    for k in range(n_pick):
        pltpu.make_async_copy(u_ref, x_hbm.at[pl.ds(0, td), :], sem).wait()


def _dispatch(u, rows_tiled, n_rows, td):
    t, d = u.shape
    n_pick = rows_tiled.shape[2] // td
    return pl.pallas_call(
        functools.partial(_dispatch_kernel, n_pick=n_pick),
        out_shape=jax.ShapeDtypeStruct((n_rows, d), F32),
        grid=(t // td,),
        in_specs=[
            pl.BlockSpec((None, 1, n_pick * td), lambda i: (i, 0, 0), memory_space=pltpu.SMEM),
            pl.BlockSpec((td, d), lambda i: (i, 0)),
        ],
        out_specs=pl.BlockSpec(memory_space=pl.ANY),
        scratch_shapes=[pltpu.SemaphoreType.DMA(())],
        compiler_params=_cparams("arbitrary"),
        name="moe_dispatch",
    )(rows_tiled, u)


def _experts_kernel(tile_ref, exp_ref, lo_ref, hi_ref, first_ref, n_ref, x_ref, wgu_ref, wd_ref, y_ref):
    p = pl.program_id(0)

    @pl.when(p < n_ref[0])
    def _():
        ff = wd_ref.shape[0]
        h = _dot(x_ref[...].astype(BF16), wgu_ref[...])
        hg = h[:, :ff]
        hu = h[:, ff:]
        act = (hg * _sigmoid(hg) * hu).astype(BF16)
        row = lax.broadcasted_iota(jnp.int32, (x_ref.shape[0], 1), 0)
        y = jnp.where((row >= lo_ref[p]) & (row < hi_ref[p]), _dot(act, wd_ref[...]), 0.0)

        @pl.when(first_ref[p] == 1)
        def _():
            y_ref[...] = y

        @pl.when(first_ref[p] == 0)
        def _():
            y_ref[...] += y


def _experts(x_sorted, work, w_gu, w_down, tm):
    n_rows, d = x_sorted.shape
    ne1, _, ff2 = w_gu.shape
    ff = ff2 // 2
    n_work_max = n_rows // tm + ne1 - 1
    return pl.pallas_call(
        _experts_kernel,
        out_shape=jax.ShapeDtypeStruct((n_rows, d), F32),
        grid_spec=pltpu.PrefetchScalarGridSpec(
            num_scalar_prefetch=6,
            grid=(n_work_max,),
            in_specs=[
                pl.BlockSpec((tm, d), lambda p, tile, exp, *_: (tile[p], 0)),
                pl.BlockSpec((None, d, ff2), lambda p, tile, exp, *_: (exp[p], 0, 0)),
                pl.BlockSpec((None, ff, d), lambda p, tile, exp, *_: (exp[p], 0, 0)),
            ],
            out_specs=pl.BlockSpec((tm, d), lambda p, tile, exp, *_: (tile[p], 0)),
        ),
        compiler_params=_cparams("arbitrary"),
        name="moe_experts",
    )(*work, x_sorted, w_gu, w_down)


def _expert_work_list(counts, n_rows, tm):
    ne1 = counts.shape[0]
    n_work_max = n_rows // tm + ne1 - 1
    ends = jnp.cumsum(counts)
    starts = ends - counts
    first_tile = starts // tm
    n_tile = jnp.where(counts > 0, (ends - 1) // tm - first_tile + 1, 0)
    work_end = jnp.cumsum(n_tile)
    n_work = work_end[-1:]
    p = jnp.minimum(jnp.arange(n_work_max, dtype=jnp.int32), n_work - 1)
    exp = jnp.searchsorted(work_end, p, side="right").astype(jnp.int32)
    tile = first_tile[exp] + p - (work_end - n_tile)[exp]
    lo = jnp.maximum(starts[exp] - tile * tm, 0)
    hi = jnp.minimum(ends[exp] - tile * tm, tm)
    first = jnp.concatenate([jnp.ones((1,), jnp.int32), (tile[1:] != tile[:-1]).astype(jnp.int32)])
    as_i32 = lambda a: a.astype(jnp.int32)
    return tuple(map(as_i32, (tile, exp, lo, hi, first, n_work))), starts


def _combine_kernel(rows_ref, nrows_ref, g_ref, x_ref, ys_ref, gain_ref, gate_ref, y_hbm, o_ref, buf, sems,
                    *, n_pick):
    i = pl.program_id(0)
    tt = x_ref.shape[0]
    slot = i % 2

    def gather(table_ref, dst_slot):
        def body(tok, carry):
            for k in range(n_pick):
                _row_copy(y_hbm, table_ref[0, k * tt + tok], buf.at[dst_slot, k], tok, sems.at[dst_slot]).start()
            return carry

        lax.fori_loop(0, tt, body, 0)

    @pl.when(i == 0)
    def _():
        gather(rows_ref, 0)

    @pl.when(i + 1 < pl.num_programs(0))
    def _():
        gather(nrows_ref, 1 - slot)

    for k in range(n_pick):
        pltpu.make_async_copy(y_hbm.at[pl.ds(0, tt), :], buf.at[slot, k], sems.at[slot]).wait()
    g = g_ref[...]
    acc = ys_ref[...]
    for k in range(n_pick):
        acc = acc + buf[slot, k] * g[:, k:k + 1]
    yn = acc * lax.rsqrt(jnp.mean(acc * acc, axis=-1, keepdims=True) + RMS_EPS) * gain_ref[...]
    o_ref[...] = x_ref[...] + gate_ref[...] * yn


def _combine(y_routed, y_shared, rows_tiled, gates, x2, gain, gate, seq, tt):
    t, d = x2.shape
    n_pick = rows_tiled.shape[2] // tt
    n_tiles = t // tt
    bmap = lambda i: ((i * tt) // seq, 0, 0)
    row = pl.BlockSpec((tt, d), lambda i: (i, 0))
    table = lambda imap: pl.BlockSpec((None, 1, n_pick * tt), imap, memory_space=pltpu.SMEM)
    return pl.pallas_call(
        functools.partial(_combine_kernel, n_pick=n_pick),
        out_shape=jax.ShapeDtypeStruct((t, d), F32),
        grid=(n_tiles,),
        in_specs=[
            table(lambda i: (i, 0, 0)),
            table(lambda i: (jnp.minimum(i + 1, n_tiles - 1), 0, 0)),
            pl.BlockSpec((tt, gates.shape[1]), lambda i: (i, 0)),
            row,
            row,
            pl.BlockSpec((1, d), lambda i: (0, 0)),
            pl.BlockSpec((None, 1, d), bmap),
            pl.BlockSpec(memory_space=pl.ANY),
        ],
        out_specs=row,
        scratch_shapes=[pltpu.VMEM((2, n_pick, tt, d), F32), pltpu.SemaphoreType.DMA((2,))],
        compiler_params=_cparams("arbitrary"),
        name="moe_combine",
    )(rows_tiled, rows_tiled, gates, x2, y_shared, gain, gate, y_routed)


def _tile_rows(rows, tile):
    n_pick, t = rows.shape
    return rows.reshape(n_pick, t // tile, tile).transpose(1, 0, 2).reshape(t // tile, 1, n_pick * tile)


def _moe_block(x2, gain_in, shift, scale, rw_t, rb, w_gu, w_down, ws_gu, ws_down, gain_out, gate, seq,
               tm=256, td=128, tt=64):
    t, d = x2.shape
    ne = rw_t.shape[0]
    u, idx, pos, gate_k, counts = _router(x2, gain_in, shift, scale, rw_t, rb, seq)
    work, starts = _expert_work_list(counts[:, 0].astype(jnp.int32), TOP_K * t, tm)
    hit = idx[:, :, None] == jnp.arange(ne, dtype=jnp.int32)[None, None, :]
    rows = jnp.sum(jnp.where(hit, starts[None, None, :], 0), axis=-1) + pos
    x_sorted = _dispatch(u, _tile_rows(rows, td), TOP_K * t, td)
    y_routed = _experts(x_sorted, work, w_gu, w_down, tm)
    shared_work, _ = _expert_work_list(jnp.full((1,), t, jnp.int32), t, tm)
    y_shared = _experts(u, shared_work, ws_gu, ws_down, tm)
    return _combine(y_routed, y_shared, _tile_rows(rows, tt), gate_k.T, x2, gain_out, gate, seq, tt)


def _pad_cols(w, n):
    return jnp.pad(w, ((0, 0), (0, n - w.shape[1])))


def _pad_rows(w, n):
    return jnp.pad(w, ((0, n - w.shape[0]), (0, 0)))


def _layout_rwkv_cols(w, rw, dl, al, gl, lp, glp):
    o = 3 * rw
    parts = [w[:, :o], _pad_cols(w[:, o:o + dl], lp), _pad_cols(w[:, o + dl:o + dl + al], lp),
             _pad_cols(w[:, o + dl + al:o + dl + al + gl], glp)]
    out = jnp.concatenate(parts, axis=1)
    return _pad_cols(out, _round_up(out.shape[1], 512))


def _layout_in_cols(w, rw, dl, al, gl, aw, n_kv, lp, glp):
    parts = [_layout_rwkv_cols(w, rw, dl, al, gl, lp, glp)]
    a0 = 3 * rw + dl + al + gl
    parts.append(w[:, a0:a0 + aw])
    for base in (a0 + aw, a0 + aw + n_kv * HEAD_DIM):
        for j in range(n_kv):
            kv = w[:, base + j * HEAD_DIM:base + (j + 1) * HEAD_DIM]
            parts += [kv, kv]
    out = jnp.concatenate(parts, axis=1)
    return _pad_cols(out, _round_up(out.shape[1], 512))


def kernel(x, c, rel_bias, ada_w, ada_b, norm_gains, w_in, shift_mu, rwkv_w0, rwkv_w2, rwkv_a0, rwkv_a2, rwkv_g2,
           rwkv_k_k, rwkv_k_a, rwkv_r_k, rwkv_ln_w, rwkv_ln_b, attn_sinks, w_out, router_w, router_bias,
           expert_w_gu, expert_w_down, shared_w_gu, shared_w_down):
    batch, seq, d = x.shape
    depth = ada_w.shape[0]
    t = batch * seq
    rw = rwkv_w0.shape[1]
    dl, al, gl = rwkv_w2.shape[1], rwkv_a2.shape[1], rwkv_g2.shape[1]
    aw = w_out.shape[1] - rw
    n_q = aw // HEAD_DIM
    n_kv = (w_in.shape[2] - (3 * rw + dl + al + gl) - aw) // (2 * HEAD_DIM)
    q_per_kv = n_q // n_kv
    ne = router_w.shape[2]
    lp = _round_up(max(dl, al), LANES)
    glp = _round_up(gl, LANES)
    q_off = _round_up(3 * rw + 2 * lp + glp, 512)

    c_pad = jnp.pad(c, ((0, 8 - batch % 8 if batch % 8 else 0), (0, 0)))
    ada = _ada(c_pad, ada_w, ada_b)[:, :batch]
    bias = _bias_table(rel_bias)

    x2 = x.reshape(t, d)
    for l in range(depth):
        sh_m, sc_m, gt_m, sh_f, sc_f, gt_f = [a.reshape(batch, 1, d) for a in jnp.split(ada[l], 6, axis=-1)]
        gains = norm_gains[l].reshape(4, 1, d)

        w_in_l = _layout_in_cols(w_in[l], rw, dl, al, gl, aw, n_kv, lp, glp).astype(BF16)
        mu_l = _layout_rwkv_cols(shift_mu[l][None, :], rw, dl, al, gl, lp, glp)
        proj = _in_proj(x2, gains[0], sh_m, sc_m, w_in_l, seq)

        r, k, v, a, g, lw = _rwkv_prep(
            proj, mu_l, rwkv_w0[l][None, :], _pad_rows(rwkv_w2[l], lp).astype(BF16), rwkv_a0[l][None, :],
            _pad_rows(rwkv_a2[l], lp).astype(BF16), _pad_rows(rwkv_g2[l], glp).astype(BF16), rw, lp, glp, seq)
        y_r = _rwkv_chunk(r, k, v, a, g, lw, rwkv_k_k[l][None, :], rwkv_k_a[l][None, :],
                          rwkv_r_k[l].reshape(1, rw), rwkv_ln_w[l][None, :], rwkv_ln_b[l][None, :], batch, seq)
        y_a = _swa(proj, attn_sinks[l], bias, batch, seq, q_off, n_kv, q_per_kv)
        x2 = _out_proj(y_r, y_a, w_out[l].astype(BF16), x2, gains[1], gt_m, seq)

        x2 = _moe_block(x2, gains[2], sh_f, sc_f, router_w[l].T, router_bias[l][:, None],
                        expert_w_gu[l].astype(BF16), expert_w_down[l].astype(BF16),
                        shared_w_gu[l][None].astype(BF16), shared_w_down[l][None].astype(BF16),
                        gains[3], gt_f, seq)
    return x2.reshape(batch, seq, d)
```

```python
import functools
import math

import numpy as np
import jax
import jax.numpy as jnp
from jax import lax
from jax.experimental import pallas as pl
from jax.experimental.pallas import tpu as pltpu

F32 = jnp.float32
BF16 = jnp.bfloat16

HEAD_DIM = 64
LANES = 128
WINDOW = 128
N_BUCKETS = 32
MAX_DISTANCE = 128
N_EXPERT_GROUPS = 8
TOPK_GROUPS = 4
TOP_K = 8
ROUTED_SCALE = 2.5
RMS_EPS = 1e-6
GN_EPS = 64e-5
CHUNK = 64
HEADS_PER_GROUP = 4
GROUP_LANES = HEADS_PER_GROUP * HEAD_DIM
VMEM_LIMIT = 56 * 1024 * 1024

HIGHEST = lax.Precision.HIGHEST


def _round_up(n, m):
    return (n + m - 1) // m * m


def _cparams(*sem):
    return pltpu.CompilerParams(dimension_semantics=sem, vmem_limit_bytes=VMEM_LIMIT)


def _dot(a, b):
    return jnp.dot(a, b, preferred_element_type=F32)


def _dot_nt(a, b):
    return lax.dot_general(a, b, (((1,), (1,)), ((), ())), preferred_element_type=F32)


def _dot_tn(a, b):
    return lax.dot_general(a, b, (((0,), (0,)), ((), ())), preferred_element_type=F32)


def _bdot(a, b):
    return lax.dot_general(a, b, (((2,), (1,)), ((0,), (0,))), preferred_element_type=F32)


def _bdot_nt(a, b):
    return lax.dot_general(a, b, (((2,), (2,)), ((0,), (0,))), preferred_element_type=F32)


def _sigmoid(x):
    return 1.0 / (1.0 + jnp.exp(-x))


def _norm_mod(x, gain, shift, scale):
    y = x * lax.rsqrt(jnp.mean(x * x, axis=-1, keepdims=True) + RMS_EPS) * gain
    return y * (1.0 + scale) + shift


def _ada_kernel(c_ref, w_ref, b_ref, o_ref):
    c = c_ref[...]
    cond = c * _sigmoid(c)
    o_ref[...] = _dot(cond.astype(BF16), w_ref[...].astype(BF16)) + b_ref[...]


def _ada(c_pad, ada_w, ada_b):
    nl, d, n = ada_w.shape
    tn = 512
    return pl.pallas_call(
        _ada_kernel,
        out_shape=jax.ShapeDtypeStruct((nl, c_pad.shape[0], n), F32),
        grid=(nl, n // tn),
        in_specs=[
            pl.BlockSpec((c_pad.shape[0], d), lambda l, j: (0, 0)),
            pl.BlockSpec((None, d, tn), lambda l, j: (l, 0, j)),
            pl.BlockSpec((None, 1, tn), lambda l, j: (l, 0, j)),
        ],
        out_specs=pl.BlockSpec((None, c_pad.shape[0], tn), lambda l, j: (l, 0, j)),
        compiler_params=_cparams("parallel", "parallel"),
        name="ada",
    )(c_pad, ada_w, ada_b.reshape(nl, 1, n))


def _bias_kernel(rbt_ref, oh_ref, o_ref):
    o_ref[...] = jnp.dot(rbt_ref[...], oh_ref[...], preferred_element_type=F32, precision=HIGHEST)


def _bucket_onehot():
    qi = np.arange(WINDOW)[:, None]
    kj = np.arange(2 * WINDOW)[None, :]
    dist = qi + WINDOW - kj
    max_exact = N_BUCKETS // 2
    d = np.maximum(dist, 0)
    far = max_exact + (np.log(np.maximum(d, max_exact).astype(np.float32) / np.float32(max_exact))
                       / np.float32(math.log(MAX_DISTANCE / max_exact))
                       * np.float32(N_BUCKETS - max_exact)).astype(np.int32)
    bucket = np.where(d < max_exact, d, np.minimum(far, N_BUCKETS - 1)).reshape(-1)
    return (np.arange(N_BUCKETS)[:, None] == bucket[None, :]).astype(np.float32)


def _bias_table(rel_bias):
    nb, nh = rel_bias.shape
    onehot = jnp.asarray(_bucket_onehot())
    out = pl.pallas_call(
        _bias_kernel,
        out_shape=jax.ShapeDtypeStruct((nh, WINDOW * 2 * WINDOW), F32),
        name="bias_table",
        compiler_params=pltpu.CompilerParams(vmem_limit_bytes=VMEM_LIMIT),
    )(rel_bias.T, onehot)
    return out.reshape(nh, WINDOW, 2 * WINDOW)


def _norm_mod_kernel(x_ref, gain_ref, sh_ref, sc_ref, u_ref):
    u_ref[...] = _norm_mod(x_ref[...], gain_ref[...], sh_ref[...], sc_ref[...]).astype(u_ref.dtype)


def _norm_mod_bf16(x2, gain, shift, scale, seq, tm=512):
    t, d = x2.shape
    bmap = lambda i: ((i * tm) // seq, 0, 0)
    row = pl.BlockSpec((tm, d), lambda i: (i, 0))
    return pl.pallas_call(
        _norm_mod_kernel,
        out_shape=jax.ShapeDtypeStruct((t, d), BF16),
        grid=(t // tm,),
        in_specs=[row, pl.BlockSpec((1, d), lambda i: (0, 0)), pl.BlockSpec((None, 1, d), bmap),
                  pl.BlockSpec((None, 1, d), bmap)],
        out_specs=row,
        compiler_params=_cparams("parallel"),
        name="norm_mod",
    )(x2, gain, shift, scale)


def _in_proj_kernel(u_ref, w_ref, o_ref, wb_ref):
    @pl.when(pl.program_id(1) == 0)
    def _():
        wb_ref[...] = w_ref[...].astype(BF16)

    o_ref[...] = _dot(u_ref[...], wb_ref[...])


def _in_proj(u, w_all, layer, tm=1024, tn=512):
    t, d = u.shape
    n = w_all.shape[2]
    tm = min(tm, t)
    return pl.pallas_call(
        _in_proj_kernel,
        out_shape=jax.ShapeDtypeStruct((t, n), F32),
        grid=(pl.cdiv(n, tn), t // tm),
        in_specs=[
            pl.BlockSpec((tm, d), lambda j, i: (i, 0)),
            pl.BlockSpec((None, d, tn), lambda j, i: (layer, 0, j)),
        ],
        out_specs=pl.BlockSpec((tm, tn), lambda j, i: (i, j)),
        scratch_shapes=[pltpu.VMEM((d, tn), BF16)],
        compiler_params=_cparams("parallel", "arbitrary"),
        name="in_proj",
    )(u, w_all)


def _lane_window(start, width):
    lo = start // LANES * LANES
    return lo, _round_up(start + width, LANES), start - lo


def _window_rows(w, start):
    lo, hi, off = _lane_window(start, w.shape[0])
    return jnp.pad(w, ((off, hi - lo - off - w.shape[0]), (0, 0))).astype(BF16)


def _rwkv_prep_kernel(rw, xw_win, xa_win, xg_win, p_ref, prev_ref, mu_ref, w0_ref, w2_ref, a0_ref, a2_ref, g2_ref,
                      r_ref, k_ref, v_ref, a_ref, g_ref, lw_ref, *, blocks_per_seq):
    i = pl.program_id(0)
    cur = p_ref[...]
    first = (i % blocks_per_seq) == 0
    prev_row = jnp.where(first, 0.0, prev_ref[7:8, :])
    rolled = pltpu.roll(cur, 1, axis=0)
    row = lax.broadcasted_iota(jnp.int32, cur.shape, 0)
    shifted = jnp.where(row == 0, prev_row, rolled)
    x = cur + (shifted - cur) * mu_ref[...]
    r_ref[...] = x[:, 0:rw].astype(r_ref.dtype)
    k_ref[...] = x[:, rw:2 * rw].astype(k_ref.dtype)
    v_ref[...] = x[:, 2 * rw:3 * rw].astype(v_ref.dtype)
    xw = x[:, xw_win[0]:xw_win[1]]
    xa = x[:, xa_win[0]:xa_win[1]]
    xg = x[:, xg_win[0]:xg_win[1]]
    z = w0_ref[...] + _dot(jnp.tanh(xw).astype(BF16), w2_ref[...])
    lw_ref[...] = -math.exp(-0.5) * _sigmoid(z)
    a_ref[...] = _sigmoid(a0_ref[...] + _dot(xa.astype(BF16), a2_ref[...])).astype(a_ref.dtype)
    g_ref[...] = _dot(_sigmoid(xg).astype(BF16), g2_ref[...]).astype(g_ref.dtype)


def _rwkv_prep(proj, mu, w0, w2, a0, a2, g2, seq, ts=256):
    t = proj.shape[0]
    rw = w0.shape[1]
    dl, al, gl = w2.shape[0], a2.shape[0], g2.shape[0]
    starts = (3 * rw, 3 * rw + dl, 3 * rw + dl + al)
    wins = [_lane_window(s, w)[:2] for s, w in zip(starts, (dl, al, gl))]
    wblock = wins[2][1]
    w2p, a2p, g2p = (_window_rows(w, s) for w, s in zip((w2, a2, g2), starts))
    mu_p = jnp.pad(mu, ((0, 0), (0, wblock - mu.shape[1])))
    full = lambda i: (0, 0)
    outs = [jax.ShapeDtypeStruct((t, rw), BF16)] * 5 + [jax.ShapeDtypeStruct((t, rw), F32)]
    ospec = pl.BlockSpec((ts, rw), lambda i: (i, 0))
    return pl.pallas_call(
        functools.partial(_rwkv_prep_kernel, rw, *wins, blocks_per_seq=seq // ts),
        out_shape=outs,
        grid=(t // ts,),
        in_specs=[
            pl.BlockSpec((ts, wblock), lambda i: (i, 0)),
            pl.BlockSpec((8, wblock), lambda i: (jnp.maximum(i * (ts // 8) - 1, 0), 0)),
            pl.BlockSpec((1, wblock), full),
            pl.BlockSpec((1, rw), full),
            pl.BlockSpec(w2p.shape, full),
            pl.BlockSpec((1, rw), full),
            pl.BlockSpec(a2p.shape, full),
            pl.BlockSpec(g2p.shape, full),
        ],
        out_specs=[ospec] * 6,
        compiler_params=_cparams("parallel"),
        name="rwkv_prep",
    )(proj, proj, mu_p, w0, w2p, a0, a2p, g2p)


CHUNK_LEVELS = int(math.log2(CHUNK))


def _chunk_constants():
    n = GROUP_LANES
    i = np.arange(n)[:, None]
    j = np.arange(n)[None, :]
    same_head = (i // CHUNK) == (j // CHUNK)
    code = np.zeros((n, n), np.int32)
    for lvl in range(CHUNK_LEVELS, 0, -1):
        code = np.where(same_head & (i > j) & ((i >> lvl) == (j >> lvl)), lvl, code)
    planes = [code == lvl for lvl in range(1, CHUNK_LEVELS + 1)] + [code > 0, same_head]
    return np.stack(planes).astype(np.float32)


def _seg_sum(x, ones_bd):
    return _dot(x.astype(BF16), ones_bd)


def _split3(x):
    hi = x.astype(BF16)
    r1 = x - hi.astype(F32)
    mid = r1.astype(BF16)
    return hi, mid, (r1 - mid.astype(F32)).astype(BF16)


def _rwkv_chunk_kernel(r_ref, k_ref, v_ref, a_ref, g_ref, lw_ref,
                       kk_ref, ka_ref, rk_ref, lnw_ref, lnb_ref, planes_ref,
                       o_ref, s_ref):
    @pl.when(pl.program_id(2) == 0)
    def _():
        s_ref[...] = jnp.zeros_like(s_ref)

    n = GROUP_LANES
    lvl_mask = [planes_ref[l] for l in range(CHUNK_LEVELS)]
    low_mask = planes_ref[CHUNK_LEVELS]
    bd = planes_ref[CHUNK_LEVELS + 1]
    tb = r_ref.shape[0]
    nch = tb // CHUNK
    tri_incl = (lax.broadcasted_iota(jnp.int32, (1, CHUNK, n), 1)
                >= lax.broadcasted_iota(jnp.int32, (1, CHUNK, n), 2) % CHUNK)
    ltri = jnp.where(lax.broadcasted_iota(jnp.int32, (nch, CHUNK, CHUNK), 1)
                     >= lax.broadcasted_iota(jnp.int32, (nch, CHUNK, CHUNK), 2), 1.0, 0.0).astype(BF16)
    eye = jnp.where(lax.broadcasted_iota(jnp.int32, (1, n, n), 1)
                    == lax.broadcasted_iota(jnp.int32, (1, n, n), 2), 1.0, 0.0).astype(BF16)

    def chunks(x):
        return x.reshape(nch, CHUNK, n)

    def tile_heads(x):
        return jnp.concatenate([x] * HEADS_PER_GROUP, axis=1)

    def stack(x):
        return tile_heads(x.astype(BF16)) * bd[None]

    r = r_ref[...].astype(F32)
    k = k_ref[...].astype(F32)
    v = v_ref[...].astype(F32)
    a = a_ref[...].astype(F32)
    lw = lw_ref[...]
    kkp = k * kk_ref[...]
    kk = kkp / jnp.maximum(jnp.sqrt(_seg_sum(kkp * kkp, bd)), 1e-12)
    k2 = k * (1.0 + (a - 1.0) * ka_ref[...])
    kb = kk * a
    bonus = _seg_sum(r * k2 * rk_ref[...], bd) * v

    cum = sum(_bdot(ltri, chunks(piece)) for piece in _split3(lw))
    cum_end = cum[:, CHUNK - 1:CHUNK, :]
    e_in = jnp.exp(cum)
    e_ex = jnp.exp(cum - chunks(lw))
    e_neg = jnp.exp(-cum)
    e_dec = jnp.exp(cum_end - cum)
    decay = jnp.exp(cum_end)
    r_t = chunks(r) * e_in
    a_t = -chunks(kk) * e_ex
    kb3 = chunks(kb)
    k23 = chunks(k2)

    ar = jnp.concatenate([a_t, r_t], axis=1).astype(BF16)
    v_s = stack(chunks(v))
    ab_rb = _bdot_nt(ar, stack(kb3 * e_neg))
    ak_rk = _bdot_nt(ar, stack(k23 * e_neg))
    rb_w = jnp.where(tri_incl, ab_rb[:, CHUNK:], 0.0).astype(BF16)
    rk_w = jnp.where(tri_incl, ak_rk[:, CHUNK:], 0.0).astype(BF16)
    amat = tile_heads(ab_rb[:, :CHUNK].astype(BF16))
    akmat = tile_heads(ak_rk[:, :CHUNK].astype(BF16)) * low_mask[None]

    tinv = eye + amat * lvl_mask[0][None]
    for lvl in range(1, CHUNK_LEVELS):
        tinv = tinv + _bdot(tinv, _bdot(amat * lvl_mask[lvl][None], tinv).astype(BF16)).astype(BF16)

    w_s = _bdot(tinv, stack(a_t)).astype(BF16)
    z_s = _bdot(tinv, _bdot(akmat, v_s).astype(BF16)).astype(BF16)
    bh_s = stack(kb3 * e_dec)
    kh_s = stack(k23 * e_dec)
    r_eff = (r_t + _bdot(rb_w, w_s)).astype(BF16)
    y0 = _bdot(rb_w, z_s) + _bdot(rk_w, v_s)

    s = s_ref[...]
    ys = []
    for ci in range(nch):
        p_mat = _dot_tn(w_s[ci], bh_s[ci]).astype(BF16)
        q_mat = _dot_tn(z_s[ci], bh_s[ci]) + _dot_tn(v_s[ci], kh_s[ci])
        s_b = s.astype(BF16)
        ys.append(_dot_nt(r_eff[ci], s_b) + y0[ci])
        s = s * decay[ci] + _dot(s_b, p_mat) + q_mat
    s_ref[...] = s

    y = jnp.concatenate(ys, axis=0)
    inv_n = 1.0 / HEAD_DIM
    mean = _seg_sum(y, bd) * inv_n
    dlt = y - mean
    var = _seg_sum(dlt * dlt, bd) * inv_n
    yn = dlt * lax.rsqrt(var + GN_EPS) * lnw_ref[...] + lnb_ref[...]
    o_ref[...] = ((yn + bonus) * g_ref[...].astype(F32)).astype(o_ref.dtype)


def _rwkv_chunk(r, k, v, a, g, lw, k_k, k_a, r_k, ln_w, ln_b, batch, seq, tb=512):
    t, rw = r.shape
    n = GROUP_LANES
    planes = jnp.asarray(_chunk_constants(), dtype=BF16)
    nblk = seq // tb
    tmap = lambda b, h, c: (b * nblk + c, h)
    pmap = lambda b, h, c: (0, h)
    tspec = pl.BlockSpec((tb, n), tmap)
    pspec = pl.BlockSpec((1, n), pmap)
    return pl.pallas_call(
        _rwkv_chunk_kernel,
        out_shape=jax.ShapeDtypeStruct((t, rw), BF16),
        grid=(batch, rw // n, nblk),
        in_specs=[tspec] * 6 + [pspec] * 5 + [pl.BlockSpec(planes.shape, lambda b, h, c: (0, 0, 0))],
        out_specs=pl.BlockSpec((tb, n), tmap),
        scratch_shapes=[pltpu.VMEM((n, n), F32)],
        compiler_params=_cparams("parallel", "parallel", "arbitrary"),
        name="rwkv_chunk",
    )(r, k, v, a, g, lw, k_k, k_a, r_k, ln_w, ln_b, planes)


def _swa_kernel(sink_ref, *refs, q_per_kv, q_off, k_off, v_off, n_q_tiles):
    q_refs = refs[:n_q_tiles]
    kp_ref, kc_ref, vp_ref, vc_ref, bias_ref, o_ref = refs[n_q_tiles:]
    j = pl.program_id(1)
    nblk = pl.program_id(2)
    blk = WINDOW
    upper = lax.broadcasted_iota(jnp.int32, (1, LANES), 1) >= HEAD_DIM

    def both_halves(x, start_col):
        in_upper = (start_col // HEAD_DIM) % 2 == 1
        sel = jnp.where(upper == in_upper, x, 0.0)
        return sel + pltpu.roll(sel, HEAD_DIM, axis=1)

    kw = both_halves(jnp.concatenate([kp_ref[...], kc_ref[...]], axis=0), k_off + HEAD_DIM * j).astype(BF16)
    vw = both_halves(jnp.concatenate([vp_ref[...], vc_ref[...]], axis=0), v_off + HEAD_DIM * j)
    v_half = (jnp.where(upper, 0.0, vw).astype(BF16), jnp.where(upper, vw, 0.0).astype(BF16))
    qi = lax.broadcasted_iota(jnp.int32, (blk, 2 * blk), 0)
    kj = lax.broadcasted_iota(jnp.int32, (blk, 2 * blk), 1)
    valid = (kj > qi) & (kj <= qi + blk) & ((nblk > 0) | (kj >= blk))
    scale = HEAD_DIM ** -0.5
    for p in range(q_per_kv // 2):
        acc = None
        for half in range(2):
            g = 2 * p + half
            col = q_off % LANES + HEAD_DIM * g
            q_tile = q_refs[col // LANES][...] * scale
            q_upper = (col // HEAD_DIM) % 2 == 1
            qm = jnp.where(upper == q_upper, q_tile, 0.0).astype(BF16)
            sink = sink_ref[j * q_per_kv + g]
            s = _dot_nt(qm, kw) + bias_ref[g]
            s = jnp.where(valid, s, -jnp.inf)
            m = jnp.maximum(jnp.max(s, axis=-1, keepdims=True), sink)
            pr = jnp.exp(s - m)
            denom = jnp.sum(pr, axis=-1, keepdims=True) + jnp.exp(sink - m)
            o = _dot(pr.astype(BF16), v_half[half]) / denom
            acc = o if acc is None else acc + o
        o_ref[:, p * LANES:(p + 1) * LANES] = acc.astype(o_ref.dtype)


def _swa(proj, sinks, bias, batch, seq, q_off, n_kv, q_per_kv):
    t = proj.shape[0]
    blk = WINDOW
    nb = seq // blk
    qw = q_per_kv * HEAD_DIM
    k_off = q_off + n_kv * qw
    v_off = k_off + n_kv * HEAD_DIM
    n_q_tiles = (q_off % LANES + qw + LANES - 1) // LANES
    cur = lambda b, j, n: b * nb + n
    prev = lambda b, j, n: b * nb + jnp.maximum(n - 1, 0)
    tile = lambda rowf, colf: pl.BlockSpec((blk, LANES), lambda b, j, n: (rowf(b, j, n), colf(j)))
    q_specs = [tile(cur, lambda j, i=i: q_off // LANES + (qw // LANES) * j + i) for i in range(n_q_tiles)]
    k_col = lambda j: (k_off + HEAD_DIM * j) // LANES
    v_col = lambda j: (v_off + HEAD_DIM * j) // LANES
    return pl.pallas_call(
        functools.partial(_swa_kernel, q_per_kv=q_per_kv, q_off=q_off, k_off=k_off, v_off=v_off,
                          n_q_tiles=n_q_tiles),
        out_shape=jax.ShapeDtypeStruct((t, n_kv * qw), BF16),
        grid=(batch, n_kv, nb),
        in_specs=[pl.BlockSpec(memory_space=pltpu.SMEM)] + q_specs + [
            tile(prev, k_col), tile(cur, k_col), tile(prev, v_col), tile(cur, v_col),
            pl.BlockSpec((q_per_kv, blk, 2 * blk), lambda b, j, n: (j, 0, 0)),
        ],
        out_specs=pl.BlockSpec((blk, qw), lambda b, j, n: (cur(b, j, n), j)),
        compiler_params=_cparams("parallel", "parallel", "parallel"),
        name="swa",
    )(sinks, *([proj] * (n_q_tiles + 4)), bias)


def _out_proj_kernel(yr_ref, ya_ref, w_ref, x_ref, gain_ref, gate_ref, o_ref, y_ref, *, tn):
    j = pl.program_id(1)
    rw = yr_ref.shape[1]
    y = _dot(yr_ref[...], w_ref[0:rw, :]) + _dot(ya_ref[...], w_ref[rw:, :])
    y_ref[:, pl.ds(pl.multiple_of(j * tn, tn), tn)] = y

    @pl.when(j == pl.num_programs(1) - 1)
    def _():
        yy = y_ref[...]
        yn = yy * lax.rsqrt(jnp.mean(yy * yy, axis=-1, keepdims=True) + RMS_EPS) * gain_ref[...]
        o_ref[...] = x_ref[...] + gate_ref[...] * yn


def _out_proj(y_r, y_a, w, x2, gain, gate, seq, tm=256, tn=512):
    t, d = x2.shape
    rw = y_r.shape[1]
    aw = y_a.shape[1]
    bmap = lambda i, j: ((i * tm) // seq, 0, 0)
    return pl.pallas_call(
        functools.partial(_out_proj_kernel, tn=tn),
        out_shape=jax.ShapeDtypeStruct((t, d), F32),
        grid=(t // tm, d // tn),
        in_specs=[
            pl.BlockSpec((tm, rw), lambda i, j: (i, 0)),
            pl.BlockSpec((tm, aw), lambda i, j: (i, 0)),
            pl.BlockSpec((rw + aw, tn), lambda i, j: (0, j)),
            pl.BlockSpec((tm, d), lambda i, j: (i, 0)),
            pl.BlockSpec((1, d), lambda i, j: (0, 0)),
            pl.BlockSpec((None, 1, d), bmap),
        ],
        out_specs=pl.BlockSpec((tm, d), lambda i, j: (i, 0)),
        scratch_shapes=[pltpu.VMEM((tm, d), F32)],
        compiler_params=_cparams("parallel", "arbitrary"),
        name="out_proj",
    )(y_r, y_a, w, x2, gain, gate)


def _rank(vals):
    n = vals.shape[0]
    idx = lax.broadcasted_iota(jnp.int32, vals.shape, 0)
    rank = jnp.zeros(vals.shape, jnp.int32)
    for e in range(n):
        row = vals[e:e + 1, :]
        beats = (row > vals) | ((row == vals) & (e < idx))
        rank = rank + beats.astype(jnp.int32)
    return rank


def _router_kernel(x_ref, gain_ref, sh_ref, sc_ref, rwt_ref, rb_ref,
                   u_ref, idx_ref, pos_ref, gate_ref, cnt_ref, carry_ref):
    @pl.when(pl.program_id(0) == 0)
    def _():
        carry_ref[...] = jnp.zeros_like(carry_ref)

    u = _norm_mod(x_ref[...], gain_ref[...], sh_ref[...], sc_ref[...])
    u_ref[...] = u
    logits = lax.dot_general(rwt_ref[...], u, (((1,), (1,)), ((), ())),
                             preferred_element_type=F32, precision=HIGHEST)
    scores = _sigmoid(logits)
    sel = scores + rb_ref[:, 0:1]
    ne, tm = sel.shape
    gsz = ne // N_EXPERT_GROUPS
    sub = lax.broadcasted_iota(jnp.int32, (gsz, tm), 0)
    gs_rows = []
    for gi in range(N_EXPERT_GROUPS):
        grp = sel[gi * gsz:(gi + 1) * gsz, :]
        m1 = jnp.max(grp, axis=0, keepdims=True)
        i1 = jnp.min(jnp.where(grp == m1, sub, gsz), axis=0, keepdims=True)
        m2 = jnp.max(jnp.where(sub == i1, -jnp.inf, grp), axis=0, keepdims=True)
        gs_rows.append(m1 + m2)
    gkeep = _rank(jnp.concatenate(gs_rows, axis=0)) < TOPK_GROUPS
    ekeep = jnp.concatenate(
        [jnp.broadcast_to(gkeep[gi:gi + 1, :], (gsz, tm)) for gi in range(N_EXPERT_GROUPS)], axis=0)
    rank = _rank(jnp.where(ekeep, sel, -jnp.inf))
    chosen = rank < TOP_K
    wts = jnp.where(chosen, scores, 0.0)
    gates_t = wts / jnp.sum(wts, axis=0, keepdims=True) * ROUTED_SCALE

    cf = jnp.where(chosen, 1.0, 0.0)
    before = (lax.broadcasted_iota(jnp.int32, (tm, tm), 0) < lax.broadcasted_iota(jnp.int32, (tm, tm), 1))
    carry = carry_ref[:, 0:1]
    pos = carry + _dot(cf.astype(BF16), jnp.where(before, 1.0, 0.0).astype(BF16))
    carry_ref[...] = jnp.broadcast_to(carry + jnp.sum(cf, axis=1, keepdims=True), carry_ref.shape)
    cnt_ref[...] = carry_ref[...]

    eidx = lax.broadcasted_iota(jnp.int32, (ne, tm), 0)
    rows_i, rows_p, rows_g = [], [], []
    for k in range(TOP_K):
        mk = rank == k
        rows_i.append(jnp.sum(jnp.where(mk, eidx, 0), axis=0, keepdims=True))
        rows_p.append(jnp.sum(jnp.where(mk, pos, 0.0), axis=0, keepdims=True))
        rows_g.append(jnp.sum(jnp.where(mk, gates_t, 0.0), axis=0, keepdims=True))
    idx_ref[...] = jnp.concatenate(rows_i, axis=0)
    pos_ref[...] = jnp.concatenate(rows_p, axis=0).astype(jnp.int32)
    gate_ref[...] = jnp.concatenate(rows_g, axis=0)


def _router(x2, gain, shift, scale, rw_t, rb, seq, tm=256):
    t, d = x2.shape
    ne = rw_t.shape[0]
    bmap = lambda i: ((i * tm) // seq, 0, 0)
    pick = lambda dt: jax.ShapeDtypeStruct((TOP_K, t), dt)
    pspec = pl.BlockSpec((TOP_K, tm), lambda i: (0, i))
    return pl.pallas_call(
        _router_kernel,
        out_shape=[jax.ShapeDtypeStruct((t, d), F32), pick(jnp.int32), pick(jnp.int32), pick(F32),
                   jax.ShapeDtypeStruct((ne, LANES), F32)],
        grid=(t // tm,),
        in_specs=[
            pl.BlockSpec((tm, d), lambda i: (i, 0)),
            pl.BlockSpec((1, d), lambda i: (0, 0)),
            pl.BlockSpec((None, 1, d), bmap),
            pl.BlockSpec((None, 1, d), bmap),
            pl.BlockSpec((ne, d), lambda i: (0, 0)),
            pl.BlockSpec((ne, 1), lambda i: (0, 0)),
        ],
        out_specs=[pl.BlockSpec((tm, d), lambda i: (i, 0)), pspec, pspec, pspec,
                   pl.BlockSpec((ne, LANES), lambda i: (0, 0))],
        scratch_shapes=[pltpu.VMEM((ne, LANES), F32)],
        compiler_params=_cparams("arbitrary"),
        name="router",
    )(x2, gain, shift, scale, rw_t, rb)


def _row_copy(src, src_row, dst, dst_row, sem):
    return pltpu.make_async_copy(src.at[pl.ds(src_row, 1), :], dst.at[pl.ds(dst_row, 1), :], sem)


def _dispatch_kernel(rows_ref, u_ref, x_hbm, sem, *, n_pick):
    td = u_ref.shape[0]

    def body(tok, carry):
        for k in range(n_pick):
            _row_copy(u_ref, tok, x_hbm, rows_ref[0, k * td + tok], sem).start()
        return carry

    lax.fori_loop(0, td, body, 0)
    for k in range(n_pick):
        pltpu.make_async_copy(u_ref, x_hbm.at[pl.ds(0, td), :], sem).wait()


def _dispatch(u, rows_tiled, n_rows, td):
    t, d = u.shape
    n_pick = rows_tiled.shape[2] // td
    return pl.pallas_call(
        functools.partial(_dispatch_kernel, n_pick=n_pick),
        out_shape=jax.ShapeDtypeStruct((n_rows, d), F32),
        grid=(t // td,),
        in_specs=[
            pl.BlockSpec((None, 1, n_pick * td), lambda i: (i, 0, 0), memory_space=pltpu.SMEM),
            pl.BlockSpec((td, d), lambda i: (i, 0)),
        ],
        out_specs=pl.BlockSpec(memory_space=pl.ANY),
        scratch_shapes=[pltpu.SemaphoreType.DMA(())],
        compiler_params=_cparams("arbitrary"),
        name="moe_dispatch",
    )(rows_tiled, u)


def _experts_kernel(tile_ref, exp_ref, lo_ref, hi_ref, first_ref, n_ref, x_ref, wgu_ref, wd_ref, y_ref):
    p = pl.program_id(0)

    @pl.when(p < n_ref[0])
    def _():
        ff = wd_ref.shape[0]
        h = _dot(x_ref[...].astype(BF16), wgu_ref[...])
        hg = h[:, :ff]
        hu = h[:, ff:]
        act = (hg * _sigmoid(hg) * hu).astype(BF16)
        row = lax.broadcasted_iota(jnp.int32, (x_ref.shape[0], 1), 0)
        mine = (row >= lo_ref[p]) & (row < hi_ref[p])

        @pl.when(first_ref[p] == 1)
        def _():
            y_ref[...] = jnp.where(mine, _dot(act, wd_ref[...]), 0.0)

        @pl.when(first_ref[p] == 0)
        def _():
            y_ref[...] += jnp.where(mine, _dot(act, wd_ref[...]), 0.0)


def _experts(x_sorted, work, w_gu, w_down, tm):
    n_rows, d = x_sorted.shape
    ne1, _, ff2 = w_gu.shape
    ff = ff2 // 2
    n_work_max = n_rows // tm + ne1 - 1
    return pl.pallas_call(
        _experts_kernel,
        out_shape=jax.ShapeDtypeStruct((n_rows, d), F32),
        grid_spec=pltpu.PrefetchScalarGridSpec(
            num_scalar_prefetch=6,
            grid=(n_work_max,),
            in_specs=[
                pl.BlockSpec((tm, d), lambda p, tile, exp, *_: (tile[p], 0)),
                pl.BlockSpec((None, d, ff2), lambda p, tile, exp, *_: (exp[p], 0, 0)),
                pl.BlockSpec((None, ff, d), lambda p, tile, exp, *_: (exp[p], 0, 0)),
            ],
            out_specs=pl.BlockSpec((tm, d), lambda p, tile, exp, *_: (tile[p], 0)),
        ),
        compiler_params=_cparams("arbitrary"),
        name="moe_experts",
    )(*work, x_sorted, w_gu, w_down)


def _expert_work_list(counts, n_rows, tm):
    ne1 = counts.shape[0]
    n_work_max = n_rows // tm + ne1 - 1
    ends = jnp.cumsum(counts)
    starts = ends - counts
    first_tile = starts // tm
    n_tile = jnp.where(counts > 0, (ends - 1) // tm - first_tile + 1, 0)
    work_end = jnp.cumsum(n_tile)
    n_work = work_end[-1:]
    p = jnp.minimum(jnp.arange(n_work_max, dtype=jnp.int32), n_work - 1)
    exp = jnp.searchsorted(work_end, p, side="right").astype(jnp.int32)
    tile = first_tile[exp] + p - (work_end - n_tile)[exp]
    lo = jnp.maximum(starts[exp] - tile * tm, 0)
    hi = jnp.minimum(ends[exp] - tile * tm, tm)
    first = jnp.concatenate([jnp.ones((1,), jnp.int32), (tile[1:] != tile[:-1]).astype(jnp.int32)])
    as_i32 = lambda a: a.astype(jnp.int32)
    return tuple(map(as_i32, (tile, exp, lo, hi, first, n_work))), starts


def _combine_kernel(rows_ref, nrows_ref, g_ref, x_ref, ys_ref, gain_ref, gate_ref, y_hbm, o_ref, buf, sems,
                    *, n_pick):
    i = pl.program_id(0)
    tt = x_ref.shape[0]
    slot = i % 2

    def gather(table_ref, dst_slot):
        def body(tok, carry):
            for k in range(n_pick):
                _row_copy(y_hbm, table_ref[0, k * tt + tok], buf.at[dst_slot, k], tok, sems.at[dst_slot]).start()
            return carry

        lax.fori_loop(0, tt, body, 0)

    @pl.when(i == 0)
    def _():
        gather(rows_ref, 0)

    @pl.when(i + 1 < pl.num_programs(0))
    def _():
        gather(nrows_ref, 1 - slot)

    for k in range(n_pick):
        pltpu.make_async_copy(y_hbm.at[pl.ds(0, tt), :], buf.at[slot, k], sems.at[slot]).wait()
    g = g_ref[...]
    acc = ys_ref[...]
    for k in range(n_pick):
        acc = acc + buf[slot, k] * g[:, k:k + 1]
    yn = acc * lax.rsqrt(jnp.mean(acc * acc, axis=-1, keepdims=True) + RMS_EPS) * gain_ref[...]
    o_ref[...] = x_ref[...] + gate_ref[...] * yn


def _combine(y_routed, y_shared, rows_tiled, gates, x2, gain, gate, seq, tt):
    t, d = x2.shape
    n_pick = rows_tiled.shape[2] // tt
    n_tiles = t // tt
    bmap = lambda i: ((i * tt) // seq, 0, 0)
    row = pl.BlockSpec((tt, d), lambda i: (i, 0))
    table = lambda imap: pl.BlockSpec((None, 1, n_pick * tt), imap, memory_space=pltpu.SMEM)
    return pl.pallas_call(
        functools.partial(_combine_kernel, n_pick=n_pick),
        out_shape=jax.ShapeDtypeStruct((t, d), F32),
        grid=(n_tiles,),
        in_specs=[
            table(lambda i: (i, 0, 0)),
            table(lambda i: (jnp.minimum(i + 1, n_tiles - 1), 0, 0)),
            pl.BlockSpec((tt, gates.shape[1]), lambda i: (i, 0)),
            row,
            row,
            pl.BlockSpec((1, d), lambda i: (0, 0)),
            pl.BlockSpec((None, 1, d), bmap),
            pl.BlockSpec(memory_space=pl.ANY),
        ],
        out_specs=row,
        scratch_shapes=[pltpu.VMEM((2, n_pick, tt, d), F32), pltpu.SemaphoreType.DMA((2,))],
        compiler_params=_cparams("arbitrary"),
        name="moe_combine",
    )(rows_tiled, rows_tiled, gates, x2, y_shared, gain, gate, y_routed)


def _tile_rows(rows, tile):
    n_pick, t = rows.shape
    return rows.reshape(n_pick, t // tile, tile).transpose(1, 0, 2).reshape(t // tile, 1, n_pick * tile)


def _moe_block(x2, gain_in, shift, scale, rw_t, rb, w_gu, w_down, ws_gu, ws_down, gain_out, gate, seq,
               tm=256, td=128, tt=64):
    t, d = x2.shape
    ne = rw_t.shape[0]
    u, idx, pos, gate_k, counts = _router(x2, gain_in, shift, scale, rw_t, rb, seq)
    work, starts = _expert_work_list(counts[:, 0].astype(jnp.int32), TOP_K * t, tm)
    hit = idx[:, :, None] == jnp.arange(ne, dtype=jnp.int32)[None, None, :]
    rows = jnp.sum(jnp.where(hit, starts[None, None, :], 0), axis=-1) + pos
    x_sorted = _dispatch(u, _tile_rows(rows, td), TOP_K * t, td)
    y_routed = _experts(x_sorted, work, w_gu, w_down, tm)
    shared_work, _ = _expert_work_list(jnp.full((1,), t, jnp.int32), t, tm)
    y_shared = _experts(u, shared_work, ws_gu, ws_down, tm)
    return _combine(y_routed, y_shared, _tile_rows(rows, tt), gate_k.T, x2, gain_out, gate, seq, tt)


def kernel(x, c, rel_bias, ada_w, ada_b, norm_gains, w_in, shift_mu, rwkv_w0, rwkv_w2, rwkv_a0, rwkv_a2, rwkv_g2,
           rwkv_k_k, rwkv_k_a, rwkv_r_k, rwkv_ln_w, rwkv_ln_b, attn_sinks, w_out, router_w, router_bias,
           expert_w_gu, expert_w_down, shared_w_gu, shared_w_down):
    batch, seq, d = x.shape
    depth = ada_w.shape[0]
    t = batch * seq
    rw = rwkv_w0.shape[1]
    dl, al, gl = rwkv_w2.shape[1], rwkv_a2.shape[1], rwkv_g2.shape[1]
    aw = w_out.shape[1] - rw
    n_q = aw // HEAD_DIM
    n_kv = (w_in.shape[2] - (3 * rw + dl + al + gl) - aw) // (2 * HEAD_DIM)
    q_per_kv = n_q // n_kv
    q_off = 3 * rw + dl + al + gl
    assert q_off % HEAD_DIM == 0, "attention heads must start at a multiple of the head width"

    c_pad = jnp.pad(c, ((0, 8 - batch % 8 if batch % 8 else 0), (0, 0)))
    ada = _ada(c_pad, ada_w, ada_b)[:, :batch]
    bias = _bias_table(rel_bias)

    x2 = x.reshape(t, d)
    for l in range(depth):
        sh_m, sc_m, gt_m, sh_f, sc_f, gt_f = [a.reshape(batch, 1, d) for a in jnp.split(ada[l], 6, axis=-1)]
        gains = norm_gains[l].reshape(4, 1, d)

        proj = _in_proj(_norm_mod_bf16(x2, gains[0], sh_m, sc_m, seq), w_in, l)

        r, k, v, a, g, lw = _rwkv_prep(proj, shift_mu[l][None, :], rwkv_w0[l][None, :], rwkv_w2[l],
                                       rwkv_a0[l][None, :], rwkv_a2[l], rwkv_g2[l], seq)
        y_r = _rwkv_chunk(r, k, v, a, g, lw, rwkv_k_k[l][None, :], rwkv_k_a[l][None, :],
                          rwkv_r_k[l].reshape(1, rw), rwkv_ln_w[l][None, :], rwkv_ln_b[l][None, :], batch, seq)
        y_a = _swa(proj, attn_sinks[l], bias, batch, seq, q_off, n_kv, q_per_kv)
        x2 = _out_proj(y_r, y_a, w_out[l].astype(BF16), x2, gains[1], gt_m, seq)

        x2 = _moe_block(x2, gains[2], sh_f, sc_f, router_w[l].T, router_bias[l][:, None],
                        expert_w_gu[l].astype(BF16), expert_w_down[l].astype(BF16),
                        shared_w_gu[l][None].astype(BF16), shared_w_down[l][None].astype(BF16),
                        gains[3], gt_f, seq)
    return x2.reshape(batch, seq, d)
```

```python
import functools
import math

import numpy as np
import jax
import jax.numpy as jnp
from jax import lax
from jax.experimental import pallas as pl
from jax.experimental.pallas import tpu as pltpu

F32 = jnp.float32
BF16 = jnp.bfloat16

HEAD_DIM = 64
LANES = 128
WINDOW = 128
N_BUCKETS = 32
MAX_DISTANCE = 128
N_EXPERT_GROUPS = 8
TOPK_GROUPS = 4
TOP_K = 8
ROUTED_SCALE = 2.5
RMS_EPS = 1e-6
GN_EPS = 64e-5
CHUNK = 64
HEADS_PER_GROUP = 4
GROUP_LANES = HEADS_PER_GROUP * HEAD_DIM
VMEM_LIMIT = 56 * 1024 * 1024

HIGHEST = lax.Precision.HIGHEST


def _round_up(n, m):
    return (n + m - 1) // m * m


def _cparams(*sem):
    return pltpu.CompilerParams(dimension_semantics=sem, vmem_limit_bytes=VMEM_LIMIT)


def _dot(a, b):
    return jnp.dot(a, b, preferred_element_type=F32)


def _dot_nt(a, b):
    return lax.dot_general(a, b, (((1,), (1,)), ((), ())), preferred_element_type=F32)


def _dot_tn(a, b):
    return lax.dot_general(a, b, (((0,), (0,)), ((), ())), preferred_element_type=F32)


def _bdot(a, b):
    return lax.dot_general(a, b, (((2,), (1,)), ((0,), (0,))), preferred_element_type=F32)


def _bdot_nt(a, b):
    return lax.dot_general(a, b, (((2,), (2,)), ((0,), (0,))), preferred_element_type=F32)


def _sigmoid(x):
    return 1.0 / (1.0 + jnp.exp(-x))


def _norm_mod(x, gain, shift, scale):
    y = x * lax.rsqrt(jnp.mean(x * x, axis=-1, keepdims=True) + RMS_EPS) * gain
    return y * (1.0 + scale) + shift


def _ada_kernel(c_ref, w_ref, b_ref, o_ref):
    c = c_ref[...]
    cond = c * _sigmoid(c)
    o_ref[...] = _dot(cond.astype(BF16), w_ref[...].astype(BF16)) + b_ref[...]


def _ada(c_pad, ada_w, ada_b):
    nl, d, n = ada_w.shape
    tn = 512
    return pl.pallas_call(
        _ada_kernel,
        out_shape=jax.ShapeDtypeStruct((nl, c_pad.shape[0], n), F32),
        grid=(nl, n // tn),
        in_specs=[
            pl.BlockSpec((c_pad.shape[0], d), lambda l, j: (0, 0)),
            pl.BlockSpec((None, d, tn), lambda l, j: (l, 0, j)),
            pl.BlockSpec((None, 1, tn), lambda l, j: (l, 0, j)),
        ],
        out_specs=pl.BlockSpec((None, c_pad.shape[0], tn), lambda l, j: (l, 0, j)),
        compiler_params=_cparams("parallel", "parallel"),
        name="ada",
    )(c_pad, ada_w, ada_b.reshape(nl, 1, n))


def _bias_kernel(rbt_ref, oh_ref, o_ref):
    o_ref[...] = jnp.dot(rbt_ref[...], oh_ref[...], preferred_element_type=F32, precision=HIGHEST)


def _bucket_onehot():
    qi = np.arange(WINDOW)[:, None]
    kj = np.arange(2 * WINDOW)[None, :]
    dist = qi + WINDOW - kj
    max_exact = N_BUCKETS // 2
    d = np.maximum(dist, 0)
    far = max_exact + (np.log(np.maximum(d, max_exact).astype(np.float32) / np.float32(max_exact))
                       / np.float32(math.log(MAX_DISTANCE / max_exact))
                       * np.float32(N_BUCKETS - max_exact)).astype(np.int32)
    bucket = np.where(d < max_exact, d, np.minimum(far, N_BUCKETS - 1)).reshape(-1)
    return (np.arange(N_BUCKETS)[:, None] == bucket[None, :]).astype(np.float32)


def _bias_table(rel_bias):
    nb, nh = rel_bias.shape
    onehot = jnp.asarray(_bucket_onehot())
    out = pl.pallas_call(
        _bias_kernel,
        out_shape=jax.ShapeDtypeStruct((nh, WINDOW * 2 * WINDOW), F32),
        name="bias_table",
        compiler_params=pltpu.CompilerParams(vmem_limit_bytes=VMEM_LIMIT),
    )(rel_bias.T, onehot)
    return out.reshape(nh, WINDOW, 2 * WINDOW)


def _norm_mod_kernel(x_ref, gain_ref, sh_ref, sc_ref, u_ref):
    u_ref[...] = _norm_mod(x_ref[...], gain_ref[...], sh_ref[...], sc_ref[...]).astype(u_ref.dtype)


def _norm_mod_bf16(x2, gain, shift, scale, seq, tm=512):
    t, d = x2.shape
    bmap = lambda i: ((i * tm) // seq, 0, 0)
    row = pl.BlockSpec((tm, d), lambda i: (i, 0))
    return pl.pallas_call(
        _norm_mod_kernel,
        out_shape=jax.ShapeDtypeStruct((t, d), BF16),
        grid=(t // tm,),
        in_specs=[row, pl.BlockSpec((1, d), lambda i: (0, 0)), pl.BlockSpec((None, 1, d), bmap),
                  pl.BlockSpec((None, 1, d), bmap)],
        out_specs=row,
        compiler_params=_cparams("parallel"),
        name="norm_mod",
    )(x2, gain, shift, scale)


def _in_proj_kernel(u_ref, wt_ref, o_ref, wb_ref):
    @pl.when(pl.program_id(1) == 0)
    def _():
        wb_ref[...] = wt_ref[...].astype(BF16)

    o_ref[...] = _dot_nt(u_ref[...], wb_ref[...])


def _in_proj(u, wt_all, layer, tm=1024, tn=512):
    t, d = u.shape
    n = wt_all.shape[1]
    tm = min(tm, t)
    return pl.pallas_call(
        _in_proj_kernel,
        out_shape=jax.ShapeDtypeStruct((t, n), F32),
        grid=(pl.cdiv(n, tn), t // tm),
        in_specs=[
            pl.BlockSpec((tm, d), lambda j, i: (i, 0)),
            pl.BlockSpec((None, tn, d), lambda j, i: (layer, j, 0)),
        ],
        out_specs=pl.BlockSpec((tm, tn), lambda j, i: (i, j)),
        scratch_shapes=[pltpu.VMEM((tn, d), BF16)],
        compiler_params=_cparams("parallel", "arbitrary"),
        name="in_proj",
    )(u, wt_all)


def _lane_window(start, width):
    lo = start // LANES * LANES
    return lo, _round_up(start + width, LANES), start - lo


def _window_rows(w, start):
    lo, hi, off = _lane_window(start, w.shape[0])
    return jnp.pad(w, ((off, hi - lo - off - w.shape[0]), (0, 0))).astype(BF16)


def _rwkv_prep_kernel(rw, xw_win, xa_win, xg_win, p_ref, prev_ref, mu_ref, w0_ref, w2_ref, a0_ref, a2_ref, g2_ref,
                      r_ref, k_ref, v_ref, a_ref, g_ref, lw_ref, *, blocks_per_seq):
    i = pl.program_id(0)
    cur = p_ref[...]
    first = (i % blocks_per_seq) == 0
    prev_row = jnp.where(first, 0.0, prev_ref[7:8, :])
    rolled = pltpu.roll(cur, 1, axis=0)
    row = lax.broadcasted_iota(jnp.int32, cur.shape, 0)
    shifted = jnp.where(row == 0, prev_row, rolled)
    x = cur + (shifted - cur) * mu_ref[...]
    r_ref[...] = x[:, 0:rw].astype(r_ref.dtype)
    k_ref[...] = x[:, rw:2 * rw].astype(k_ref.dtype)
    v_ref[...] = x[:, 2 * rw:3 * rw].astype(v_ref.dtype)
    xw = x[:, xw_win[0]:xw_win[1]]
    xa = x[:, xa_win[0]:xa_win[1]]
    xg = x[:, xg_win[0]:xg_win[1]]
    z = w0_ref[...] + _dot(jnp.tanh(xw).astype(BF16), w2_ref[...])
    lw_ref[...] = -math.exp(-0.5) * _sigmoid(z)
    a_ref[...] = _sigmoid(a0_ref[...] + _dot(xa.astype(BF16), a2_ref[...])).astype(a_ref.dtype)
    g_ref[...] = _dot(_sigmoid(xg).astype(BF16), g2_ref[...]).astype(g_ref.dtype)


def _rwkv_prep(proj, mu, w0, w2, a0, a2, g2, seq, ts=256):
    t = proj.shape[0]
    rw = w0.shape[1]
    dl, al, gl = w2.shape[0], a2.shape[0], g2.shape[0]
    starts = (3 * rw, 3 * rw + dl, 3 * rw + dl + al)
    wins = [_lane_window(s, w)[:2] for s, w in zip(starts, (dl, al, gl))]
    wblock = wins[2][1]
    w2p, a2p, g2p = (_window_rows(w, s) for w, s in zip((w2, a2, g2), starts))
    mu_p = jnp.pad(mu, ((0, 0), (0, wblock - mu.shape[1])))
    full = lambda i: (0, 0)
    outs = [jax.ShapeDtypeStruct((t, rw), BF16)] * 5 + [jax.ShapeDtypeStruct((t, rw), F32)]
    ospec = pl.BlockSpec((ts, rw), lambda i: (i, 0))
    return pl.pallas_call(
        functools.partial(_rwkv_prep_kernel, rw, *wins, blocks_per_seq=seq // ts),
        out_shape=outs,
        grid=(t // ts,),
        in_specs=[
            pl.BlockSpec((ts, wblock), lambda i: (i, 0)),
            pl.BlockSpec((8, wblock), lambda i: (jnp.maximum(i * (ts // 8) - 1, 0), 0)),
            pl.BlockSpec((1, wblock), full),
            pl.BlockSpec((1, rw), full),
            pl.BlockSpec(w2p.shape, full),
            pl.BlockSpec((1, rw), full),
            pl.BlockSpec(a2p.shape, full),
            pl.BlockSpec(g2p.shape, full),
        ],
        out_specs=[ospec] * 6,
        compiler_params=_cparams("parallel"),
        name="rwkv_prep",
    )(proj, proj, mu_p, w0, w2p, a0, a2p, g2p)


CHUNK_LEVELS = int(math.log2(CHUNK))


def _chunk_constants():
    n = GROUP_LANES
    i = np.arange(n)[:, None]
    j = np.arange(n)[None, :]
    same_head = (i // CHUNK) == (j // CHUNK)
    code = np.zeros((n, n), np.int32)
    for lvl in range(CHUNK_LEVELS, 0, -1):
        code = np.where(same_head & (i > j) & ((i >> lvl) == (j >> lvl)), lvl, code)
    planes = [code == lvl for lvl in range(1, CHUNK_LEVELS + 1)] + [code > 0, same_head]
    return np.stack(planes).astype(np.float32)


def _seg_sum(x, ones_bd):
    return _dot(x.astype(BF16), ones_bd)


def _split3(x):
    hi = x.astype(BF16)
    r1 = x - hi.astype(F32)
    mid = r1.astype(BF16)
    return hi, mid, (r1 - mid.astype(F32)).astype(BF16)


def _rwkv_chunk_kernel(r_ref, k_ref, v_ref, a_ref, g_ref, lw_ref,
                       kk_ref, ka_ref, rk_ref, lnw_ref, lnb_ref, planes_ref,
                       o_ref, s_ref):
    @pl.when(pl.program_id(2) == 0)
    def _():
        s_ref[...] = jnp.zeros_like(s_ref)

    n = GROUP_LANES
    lvl_mask = [planes_ref[l] for l in range(CHUNK_LEVELS)]
    low_mask = planes_ref[CHUNK_LEVELS]
    bd = planes_ref[CHUNK_LEVELS + 1]
    tb = r_ref.shape[0]
    nch = tb // CHUNK
    tri_incl = (lax.broadcasted_iota(jnp.int32, (1, CHUNK, n), 1)
                >= lax.broadcasted_iota(jnp.int32, (1, CHUNK, n), 2) % CHUNK)
    ltri = jnp.where(lax.broadcasted_iota(jnp.int32, (nch, CHUNK, CHUNK), 1)
                     >= lax.broadcasted_iota(jnp.int32, (nch, CHUNK, CHUNK), 2), 1.0, 0.0).astype(BF16)
    eye = jnp.where(lax.broadcasted_iota(jnp.int32, (1, n, n), 1)
                    == lax.broadcasted_iota(jnp.int32, (1, n, n), 2), 1.0, 0.0).astype(BF16)

    def chunks(x):
        return x.reshape(nch, CHUNK, n)

    def tile_heads(x):
        return jnp.concatenate([x] * HEADS_PER_GROUP, axis=1)

    def stack(x):
        return tile_heads(x.astype(BF16)) * bd[None]

    r = r_ref[...].astype(F32)
    k = k_ref[...].astype(F32)
    v = v_ref[...].astype(F32)
    a = a_ref[...].astype(F32)
    lw = lw_ref[...]
    kkp = k * kk_ref[...]
    kk = kkp / jnp.maximum(jnp.sqrt(_seg_sum(kkp * kkp, bd)), 1e-12)
    k2 = k * (1.0 + (a - 1.0) * ka_ref[...])
    kb = kk * a
    bonus = _seg_sum(r * k2 * rk_ref[...], bd) * v

    cum = sum(_bdot(ltri, chunks(piece)) for piece in _split3(lw))
    cum_end = cum[:, CHUNK - 1:CHUNK, :]
    e_in = jnp.exp(cum)
    e_ex = jnp.exp(cum - chunks(lw))
    e_neg = jnp.exp(-cum)
    e_dec = jnp.exp(cum_end - cum)
    decay = jnp.exp(cum_end)
    r_t = chunks(r) * e_in
    a_t = -chunks(kk) * e_ex
    kb3 = chunks(kb)
    k23 = chunks(k2)

    ar = jnp.concatenate([a_t, r_t], axis=1).astype(BF16)
    v_s = stack(chunks(v))
    ab_rb = _bdot_nt(ar, stack(kb3 * e_neg))
    ak_rk = _bdot_nt(ar, stack(k23 * e_neg))
    rb_w = jnp.where(tri_incl, ab_rb[:, CHUNK:], 0.0).astype(BF16)
    rk_w = jnp.where(tri_incl, ak_rk[:, CHUNK:], 0.0).astype(BF16)
    amat = tile_heads(ab_rb[:, :CHUNK].astype(BF16))
    akmat = tile_heads(ak_rk[:, :CHUNK].astype(BF16)) * low_mask[None]

    tinv = eye + amat * lvl_mask[0][None]
    for lvl in range(1, CHUNK_LEVELS):
        tinv = tinv + _bdot(tinv, _bdot(amat * lvl_mask[lvl][None], tinv).astype(BF16)).astype(BF16)

    w_s = _bdot(tinv, stack(a_t)).astype(BF16)
    z_s = _bdot(tinv, _bdot(akmat, v_s).astype(BF16)).astype(BF16)
    bh_s = stack(kb3 * e_dec)
    kh_s = stack(k23 * e_dec)
    r_eff = (r_t + _bdot(rb_w, w_s)).astype(BF16)
    y0 = _bdot(rb_w, z_s) + _bdot(rk_w, v_s)

    s = s_ref[...]
    ys = []
    for ci in range(nch):
        p_mat = _dot_tn(w_s[ci], bh_s[ci]).astype(BF16)
        q_mat = _dot_tn(z_s[ci], bh_s[ci]) + _dot_tn(v_s[ci], kh_s[ci])
        s_b = s.astype(BF16)
        ys.append(_dot_nt(r_eff[ci], s_b) + y0[ci])
        s = s * decay[ci] + _dot(s_b, p_mat) + q_mat
    s_ref[...] = s

    y = jnp.concatenate(ys, axis=0)
    inv_n = 1.0 / HEAD_DIM
    mean = _seg_sum(y, bd) * inv_n
    dlt = y - mean
    var = _seg_sum(dlt * dlt, bd) * inv_n
    yn = dlt * lax.rsqrt(var + GN_EPS) * lnw_ref[...] + lnb_ref[...]
    o_ref[...] = ((yn + bonus) * g_ref[...].astype(F32)).astype(o_ref.dtype)


def _rwkv_chunk(r, k, v, a, g, lw, k_k, k_a, r_k, ln_w, ln_b, batch, seq, tb=512):
    t, rw = r.shape
    n = GROUP_LANES
    planes = jnp.asarray(_chunk_constants(), dtype=BF16)
    nblk = seq // tb
    tmap = lambda b, h, c: (b * nblk + c, h)
    pmap = lambda b, h, c: (0, h)
    tspec = pl.BlockSpec((tb, n), tmap)
    pspec = pl.BlockSpec((1, n), pmap)
    return pl.pallas_call(
        _rwkv_chunk_kernel,
        out_shape=jax.ShapeDtypeStruct((t, rw), BF16),
        grid=(batch, rw // n, nblk),
        in_specs=[tspec] * 6 + [pspec] * 5 + [pl.BlockSpec(planes.shape, lambda b, h, c: (0, 0, 0))],
        out_specs=pl.BlockSpec((tb, n), tmap),
        scratch_shapes=[pltpu.VMEM((n, n), F32)],
        compiler_params=_cparams("parallel", "parallel", "arbitrary"),
        name="rwkv_chunk",
    )(r, k, v, a, g, lw, k_k, k_a, r_k, ln_w, ln_b, planes)


def _swa_kernel(sink_ref, *refs, q_per_kv, q_off, k_off, v_off, n_q_tiles):
    q_refs = refs[:n_q_tiles]
    kp_ref, kc_ref, vp_ref, vc_ref, bias_ref, o_ref = refs[n_q_tiles:]
    j = pl.program_id(1)
    nblk = pl.program_id(2)
    blk = WINDOW
    upper = lax.broadcasted_iota(jnp.int32, (1, LANES), 1) >= HEAD_DIM

    def both_halves(x, start_col):
        in_upper = (start_col // HEAD_DIM) % 2 == 1
        sel = jnp.where(upper == in_upper, x, 0.0)
        return sel + pltpu.roll(sel, HEAD_DIM, axis=1)

    kw = both_halves(jnp.concatenate([kp_ref[...], kc_ref[...]], axis=0), k_off + HEAD_DIM * j).astype(BF16)
    vw = both_halves(jnp.concatenate([vp_ref[...], vc_ref[...]], axis=0), v_off + HEAD_DIM * j)
    v_half = (jnp.where(upper, 0.0, vw).astype(BF16), jnp.where(upper, vw, 0.0).astype(BF16))
    qi = lax.broadcasted_iota(jnp.int32, (blk, 2 * blk), 0)
    kj = lax.broadcasted_iota(jnp.int32, (blk, 2 * blk), 1)
    valid = (kj > qi) & (kj <= qi + blk) & ((nblk > 0) | (kj >= blk))
    scale = HEAD_DIM ** -0.5
    for p in range(q_per_kv // 2):
        acc = None
        for half in range(2):
            g = 2 * p + half
            col = q_off % LANES + HEAD_DIM * g
            q_tile = q_refs[col // LANES][...] * scale
            q_upper = (col // HEAD_DIM) % 2 == 1
            qm = jnp.where(upper == q_upper, q_tile, 0.0).astype(BF16)
            sink = sink_ref[j * q_per_kv + g]
            s = _dot_nt(qm, kw) + bias_ref[g]
            s = jnp.where(valid, s, -jnp.inf)
            m = jnp.maximum(jnp.max(s, axis=-1, keepdims=True), sink)
            pr = jnp.exp(s - m)
            denom = jnp.sum(pr, axis=-1, keepdims=True) + jnp.exp(sink - m)
            o = _dot(pr.astype(BF16), v_half[half]) / denom
            acc = o if acc is None else acc + o
        o_ref[:, p * LANES:(p + 1) * LANES] = acc.astype(o_ref.dtype)


def _swa(proj, sinks, bias, batch, seq, q_off, n_kv, q_per_kv):
    t = proj.shape[0]
    blk = WINDOW
    nb = seq // blk
    qw = q_per_kv * HEAD_DIM
    k_off = q_off + n_kv * qw
    v_off = k_off + n_kv * HEAD_DIM
    n_q_tiles = (q_off % LANES + qw + LANES - 1) // LANES
    cur = lambda b, j, n: b * nb + n
    prev = lambda b, j, n: b * nb + jnp.maximum(n - 1, 0)
    tile = lambda rowf, colf: pl.BlockSpec((blk, LANES), lambda b, j, n: (rowf(b, j, n), colf(j)))
    q_specs = [tile(cur, lambda j, i=i: q_off // LANES + (qw // LANES) * j + i) for i in range(n_q_tiles)]
    k_col = lambda j: (k_off + HEAD_DIM * j) // LANES
    v_col = lambda j: (v_off + HEAD_DIM * j) // LANES
    return pl.pallas_call(
        functools.partial(_swa_kernel, q_per_kv=q_per_kv, q_off=q_off, k_off=k_off, v_off=v_off,
                          n_q_tiles=n_q_tiles),
        out_shape=jax.ShapeDtypeStruct((t, n_kv * qw), BF16),
        grid=(batch, n_kv, nb),
        in_specs=[pl.BlockSpec(memory_space=pltpu.SMEM)] + q_specs + [
            tile(prev, k_col), tile(cur, k_col), tile(prev, v_col), tile(cur, v_col),
            pl.BlockSpec((q_per_kv, blk, 2 * blk), lambda b, j, n: (j, 0, 0)),
        ],
        out_specs=pl.BlockSpec((blk, qw), lambda b, j, n: (cur(b, j, n), j)),
        compiler_params=_cparams("parallel", "parallel", "parallel"),
        name="swa",
    )(sinks, *([proj] * (n_q_tiles + 4)), bias)


def _out_proj_kernel(yr_ref, ya_ref, w_ref, o_ref, wb_ref):
    @pl.when(pl.program_id(1) == 0)
    def _():
        wb_ref[...] = w_ref[...].astype(BF16)

    rw = yr_ref.shape[1]
    o_ref[...] = _dot(yr_ref[...], wb_ref[0:rw, :]) + _dot(ya_ref[...], wb_ref[rw:, :])


def _out_proj(y_r, y_a, w_all, layer, tm=1024, tn=512):
    t, rw = y_r.shape
    aw = y_a.shape[1]
    d = w_all.shape[2]
    tm = min(tm, t)
    return pl.pallas_call(
        _out_proj_kernel,
        out_shape=jax.ShapeDtypeStruct((t, d), F32),
        grid=(d // tn, t // tm),
        in_specs=[
            pl.BlockSpec((tm, rw), lambda j, i: (i, 0)),
            pl.BlockSpec((tm, aw), lambda j, i: (i, 0)),
            pl.BlockSpec((None, rw + aw, tn), lambda j, i: (layer, 0, j)),
        ],
        out_specs=pl.BlockSpec((tm, tn), lambda j, i: (i, j)),
        scratch_shapes=[pltpu.VMEM((rw + aw, tn), BF16)],
        compiler_params=_cparams("parallel", "arbitrary"),
        name="out_proj",
    )(y_r, y_a, w_all)


def _rank(vals):
    n = vals.shape[0]
    idx = lax.broadcasted_iota(jnp.int32, vals.shape, 0)
    rank = jnp.zeros(vals.shape, jnp.int32)
    for e in range(n):
        row = vals[e:e + 1, :]
        beats = (row > vals) | ((row == vals) & (e < idx))
        rank = rank + beats.astype(jnp.int32)
    return rank


def _router_kernel(x_ref, y_ref, gain_y_ref, gate_y_ref, gain_ref, sh_ref, sc_ref, rwt_ref, rb_ref,
                   xn_ref, u_ref, idx_ref, pos_ref, gate_ref, cnt_ref, carry_ref):
    @pl.when(pl.program_id(0) == 0)
    def _():
        carry_ref[...] = jnp.zeros_like(carry_ref)

    y = y_ref[...]
    yn = y * lax.rsqrt(jnp.mean(y * y, axis=-1, keepdims=True) + RMS_EPS) * gain_y_ref[...]
    xn = x_ref[...] + gate_y_ref[...] * yn
    xn_ref[...] = xn
    u = _norm_mod(xn, gain_ref[...], sh_ref[...], sc_ref[...])
    u_ref[...] = u
    logits = lax.dot_general(rwt_ref[...], u, (((1,), (1,)), ((), ())),
                             preferred_element_type=F32, precision=HIGHEST)
    scores = _sigmoid(logits)
    sel = scores + rb_ref[:, 0:1]
    ne, tm = sel.shape
    gsz = ne // N_EXPERT_GROUPS
    sub = lax.broadcasted_iota(jnp.int32, (gsz, tm), 0)
    gs_rows = []
    for gi in range(N_EXPERT_GROUPS):
        grp = sel[gi * gsz:(gi + 1) * gsz, :]
        m1 = jnp.max(grp, axis=0, keepdims=True)
        i1 = jnp.min(jnp.where(grp == m1, sub, gsz), axis=0, keepdims=True)
        m2 = jnp.max(jnp.where(sub == i1, -jnp.inf, grp), axis=0, keepdims=True)
        gs_rows.append(m1 + m2)
    gkeep = _rank(jnp.concatenate(gs_rows, axis=0)) < TOPK_GROUPS
    ekeep = jnp.concatenate(
        [jnp.broadcast_to(gkeep[gi:gi + 1, :], (gsz, tm)) for gi in range(N_EXPERT_GROUPS)], axis=0)
    rank = _rank(jnp.where(ekeep, sel, -jnp.inf))
    chosen = rank < TOP_K
    wts = jnp.where(chosen, scores, 0.0)
    gates_t = wts / jnp.sum(wts, axis=0, keepdims=True) * ROUTED_SCALE

    cf = jnp.where(chosen, 1.0, 0.0)
    before = (lax.broadcasted_iota(jnp.int32, (tm, tm), 0) < lax.broadcasted_iota(jnp.int32, (tm, tm), 1))
    carry = carry_ref[:, 0:1]
    pos = carry + _dot(cf.astype(BF16), jnp.where(before, 1.0, 0.0).astype(BF16))
    carry_ref[...] = jnp.broadcast_to(carry + jnp.sum(cf, axis=1, keepdims=True), carry_ref.shape)
    cnt_ref[...] = carry_ref[...]

    eidx = lax.broadcasted_iota(jnp.int32, (ne, tm), 0)
    rows_i, rows_p, rows_g = [], [], []
    for k in range(TOP_K):
        mk = rank == k
        rows_i.append(jnp.sum(jnp.where(mk, eidx, 0), axis=0, keepdims=True))
        rows_p.append(jnp.sum(jnp.where(mk, pos, 0.0), axis=0, keepdims=True))
        rows_g.append(jnp.sum(jnp.where(mk, gates_t, 0.0), axis=0, keepdims=True))
    idx_ref[...] = jnp.concatenate(rows_i, axis=0)
    pos_ref[...] = jnp.concatenate(rows_p, axis=0).astype(jnp.int32)
    gate_ref[...] = jnp.concatenate(rows_g, axis=0)


def _router(x2, y, gain_y, gate_y, gain, shift, scale, rw_t, rb, seq, tm=256):
    t, d = x2.shape
    ne = rw_t.shape[0]
    bmap = lambda i: ((i * tm) // seq, 0, 0)
    pick = lambda dt: jax.ShapeDtypeStruct((TOP_K, t), dt)
    pspec = pl.BlockSpec((TOP_K, tm), lambda i: (0, i))
    row = pl.BlockSpec((tm, d), lambda i: (i, 0))
    vec = pl.BlockSpec((1, d), lambda i: (0, 0))
    per_batch = pl.BlockSpec((None, 1, d), bmap)
    return pl.pallas_call(
        _router_kernel,
        out_shape=[jax.ShapeDtypeStruct((t, d), F32), jax.ShapeDtypeStruct((t, d), F32),
                   pick(jnp.int32), pick(jnp.int32), pick(F32), jax.ShapeDtypeStruct((ne, LANES), F32)],
        grid=(t // tm,),
        in_specs=[row, row, vec, per_batch, vec, per_batch, per_batch,
                  pl.BlockSpec((ne, d), lambda i: (0, 0)), pl.BlockSpec((ne, 1), lambda i: (0, 0))],
        out_specs=[row, row, pspec, pspec, pspec, pl.BlockSpec((ne, LANES), lambda i: (0, 0))],
        scratch_shapes=[pltpu.VMEM((ne, LANES), F32)],
        compiler_params=_cparams("arbitrary"),
        name="router",
    )(x2, y, gain_y, gate_y, gain, shift, scale, rw_t, rb)


def _row_copy(src, src_row, dst, dst_row, sem):
    return pltpu.make_async_copy(src.at[pl.ds(src_row, 1), :], dst.at[pl.ds(dst_row, 1), :], sem)


def _dispatch_kernel(rows_ref, u_ref, x_hbm, sem, *, n_pick):
    td = u_ref.shape[0]

    def body(tok, carry):
        for k in range(n_pick):
            _row_copy(u_ref, tok, x_hbm, rows_ref[0, k * td + tok], sem).start()
        return carry

    lax.fori_loop(0, td, body, 0)
    for k in range(n_pick):
        pltpu.make_async_copy(u_ref, x_hbm.at[pl.ds(0, td), :], sem).wait()


def _dispatch(u, rows_tiled, n_rows, td):
    t, d = u.shape
    n_pick = rows_tiled.shape[2] // td
    return pl.pallas_call(
        functools.partial(_dispatch_kernel, n_pick=n_pick),
        out_shape=jax.ShapeDtypeStruct((n_rows, d), F32),
        grid=(t // td,),
        in_specs=[
            pl.BlockSpec((None, 1, n_pick * td), lambda i: (i, 0, 0), memory_space=pltpu.SMEM),
            pl.BlockSpec((td, d), lambda i: (i, 0)),
        ],
        out_specs=pl.BlockSpec(memory_space=pl.ANY),
        scratch_shapes=[pltpu.SemaphoreType.DMA(())],
        compiler_params=_cparams("arbitrary"),
        name="moe_dispatch",
    )(rows_tiled, u)


def _experts_kernel(tile_ref, exp_ref, lo_ref, hi_ref, first_ref, n_ref, x_ref, wgu_ref, wd_ref, y_ref):
    p = pl.program_id(0)

    @pl.when(p < n_ref[0])
    def _():
        ff = wd_ref.shape[0]
        h = _dot(x_ref[...].astype(BF16), wgu_ref[...])
        hg = h[:, :ff]
        hu = h[:, ff:]
        act = (hg * _sigmoid(hg) * hu).astype(BF16)
        row = lax.broadcasted_iota(jnp.int32, (x_ref.shape[0], 1), 0)
        mine = (row >= lo_ref[p]) & (row < hi_ref[p])

        @pl.when(first_ref[p] == 1)
        def _():
            y_ref[...] = jnp.where(mine, _dot(act, wd_ref[...].astype(BF16)), 0.0)

        @pl.when(first_ref[p] == 0)
        def _():
            y_ref[...] += jnp.where(mine, _dot(act, wd_ref[...].astype(BF16)), 0.0)


def _experts(x_sorted, work, w_gu, w_down, layer, tm):
    n_rows, d = x_sorted.shape
    _, ne1, _, ff2 = w_gu.shape
    ff = ff2 // 2
    n_work_max = n_rows // tm + ne1 - 1
    return pl.pallas_call(
        _experts_kernel,
        out_shape=jax.ShapeDtypeStruct((n_rows, d), F32),
        grid_spec=pltpu.PrefetchScalarGridSpec(
            num_scalar_prefetch=6,
            grid=(n_work_max,),
            in_specs=[
                pl.BlockSpec((tm, d), lambda p, tile, exp, *_: (tile[p], 0)),
                pl.BlockSpec((None, None, d, ff2), lambda p, tile, exp, *_: (layer, exp[p], 0, 0)),
                pl.BlockSpec((None, None, ff, d), lambda p, tile, exp, *_: (layer, exp[p], 0, 0)),
            ],
            out_specs=pl.BlockSpec((tm, d), lambda p, tile, exp, *_: (tile[p], 0)),
        ),
        compiler_params=_cparams("arbitrary"),
        name="moe_experts",
    )(*work, x_sorted, w_gu, w_down)


def _expert_work_list(counts, n_rows, tm):
    ne1 = counts.shape[0]
    n_work_max = n_rows // tm + ne1 - 1
    ends = jnp.cumsum(counts)
    starts = ends - counts
    first_tile = starts // tm
    n_tile = jnp.where(counts > 0, (ends - 1) // tm - first_tile + 1, 0)
    work_end = jnp.cumsum(n_tile)
    n_work = work_end[-1:]
    p = jnp.minimum(jnp.arange(n_work_max, dtype=jnp.int32), n_work - 1)
    exp = jnp.searchsorted(work_end, p, side="right").astype(jnp.int32)
    tile = first_tile[exp] + p - (work_end - n_tile)[exp]
    lo = jnp.maximum(starts[exp] - tile * tm, 0)
    hi = jnp.minimum(ends[exp] - tile * tm, tm)
    first = jnp.concatenate([jnp.ones((1,), jnp.int32), (tile[1:] != tile[:-1]).astype(jnp.int32)])
    as_i32 = lambda a: a.astype(jnp.int32)
    return tuple(map(as_i32, (tile, exp, lo, hi, first, n_work))), starts


def _combine_kernel(rows_ref, nrows_ref, g_ref, x_ref, ys_ref, gain_ref, gate_ref, y_hbm, o_ref, buf, sems,
                    *, n_pick):
    i = pl.program_id(0)
    tt = x_ref.shape[0]
    slot = i % 2

    def gather(table_ref, dst_slot):
        def body(tok, carry):
            for k in range(n_pick):
                _row_copy(y_hbm, table_ref[0, k * tt + tok], buf.at[dst_slot, k], tok, sems.at[dst_slot]).start()
            return carry

        lax.fori_loop(0, tt, body, 0)

    @pl.when(i == 0)
    def _():
        gather(rows_ref, 0)

    @pl.when(i + 1 < pl.num_programs(0))
    def _():
        gather(nrows_ref, 1 - slot)

    for k in range(n_pick):
        pltpu.make_async_copy(y_hbm.at[pl.ds(0, tt), :], buf.at[slot, k], sems.at[slot]).wait()
    g = g_ref[...]
    acc = ys_ref[...]
    for k in range(n_pick):
        acc = acc + buf[slot, k] * g[:, k:k + 1]
    yn = acc * lax.rsqrt(jnp.mean(acc * acc, axis=-1, keepdims=True) + RMS_EPS) * gain_ref[...]
    o_ref[...] = x_ref[...] + gate_ref[...] * yn


def _combine(y_routed, y_shared, rows_tiled, gates, x2, gain, gate, seq, tt):
    t, d = x2.shape
    n_pick = rows_tiled.shape[2] // tt
    n_tiles = t // tt
    bmap = lambda i: ((i * tt) // seq, 0, 0)
    row = pl.BlockSpec((tt, d), lambda i: (i, 0))
    table = lambda imap: pl.BlockSpec((None, 1, n_pick * tt), imap, memory_space=pltpu.SMEM)
    return pl.pallas_call(
        functools.partial(_combine_kernel, n_pick=n_pick),
        out_shape=jax.ShapeDtypeStruct((t, d), F32),
        grid=(n_tiles,),
        in_specs=[
            table(lambda i: (i, 0, 0)),
            table(lambda i: (jnp.minimum(i + 1, n_tiles - 1), 0, 0)),
            pl.BlockSpec((tt, gates.shape[1]), lambda i: (i, 0)),
            row,
            row,
            pl.BlockSpec((1, d), lambda i: (0, 0)),
            pl.BlockSpec((None, 1, d), bmap),
            pl.BlockSpec(memory_space=pl.ANY),
        ],
        out_specs=row,
        scratch_shapes=[pltpu.VMEM((2, n_pick, tt, d), F32), pltpu.SemaphoreType.DMA((2,))],
        compiler_params=_cparams("arbitrary"),
        name="moe_combine",
    )(rows_tiled, rows_tiled, gates, x2, y_shared, gain, gate, y_routed)


def _tile_rows(rows, tile):
    n_pick, t = rows.shape
    return rows.reshape(n_pick, t // tile, tile).transpose(1, 0, 2).reshape(t // tile, 1, n_pick * tile)


def _moe_block(x2, y_mix, gain_mix, gate_mix, gain_in, shift, scale, rw_t, rb, w_gu, w_down, ws_gu, ws_down, layer,
               gain_out, gate, seq, tm=256, td=128, tt=64):
    t, d = x2.shape
    ne = rw_t.shape[0]
    x2, u, idx, pos, gate_k, counts = _router(x2, y_mix, gain_mix, gate_mix, gain_in, shift, scale, rw_t, rb, seq)
    work, starts = _expert_work_list(counts[:, 0].astype(jnp.int32), TOP_K * t, tm)
    hit = idx[:, :, None] == jnp.arange(ne, dtype=jnp.int32)[None, None, :]
    rows = jnp.sum(jnp.where(hit, starts[None, None, :], 0), axis=-1) + pos
    x_sorted = _dispatch(u, _tile_rows(rows, td), TOP_K * t, td)
    y_routed = _experts(x_sorted, work, w_gu, w_down, layer, tm)
    shared_work, _ = _expert_work_list(jnp.full((1,), t, jnp.int32), t, tm)
    y_shared = _experts(u, shared_work, ws_gu, ws_down, layer, tm)
    return _combine(y_routed, y_shared, _tile_rows(rows, tt), gate_k.T, x2, gain_out, gate, seq, tt)


def kernel(x, c, rel_bias, ada_w, ada_b, norm_gains, w_in, shift_mu, rwkv_w0, rwkv_w2, rwkv_a0, rwkv_a2, rwkv_g2,
           rwkv_k_k, rwkv_k_a, rwkv_r_k, rwkv_ln_w, rwkv_ln_b, attn_sinks, w_out, router_w, router_bias,
           expert_w_gu, expert_w_down, shared_w_gu, shared_w_down):
    batch, seq, d = x.shape
    depth = ada_w.shape[0]
    t = batch * seq
    rw = rwkv_w0.shape[1]
    dl, al, gl = rwkv_w2.shape[1], rwkv_a2.shape[1], rwkv_g2.shape[1]
    aw = w_out.shape[1] - rw
    n_q = aw // HEAD_DIM
    n_kv = (w_in.shape[2] - (3 * rw + dl + al + gl) - aw) // (2 * HEAD_DIM)
    q_per_kv = n_q // n_kv
    q_off = 3 * rw + dl + al + gl
    assert q_off % HEAD_DIM == 0, "attention heads must start at a multiple of the head width"

    c_pad = jnp.pad(c, ((0, 8 - batch % 8 if batch % 8 else 0), (0, 0)))
    ada = _ada(c_pad, ada_w, ada_b)[:, :batch]
    bias = _bias_table(rel_bias)
    w_in_t = jnp.swapaxes(w_in, 1, 2)
    w_gu_b = expert_w_gu.astype(BF16)
    ws_gu_b = shared_w_gu[:, None].astype(BF16)

    x2 = x.reshape(t, d)
    for l in range(depth):
        sh_m, sc_m, gt_m, sh_f, sc_f, gt_f = [a.reshape(batch, 1, d) for a in jnp.split(ada[l], 6, axis=-1)]
        gains = norm_gains[l].reshape(4, 1, d)

        proj = _in_proj(_norm_mod_bf16(x2, gains[0], sh_m, sc_m, seq), w_in_t, l)

        r, k, v, a, g, lw = _rwkv_prep(proj, shift_mu[l][None, :], rwkv_w0[l][None, :], rwkv_w2[l],
                                       rwkv_a0[l][None, :], rwkv_a2[l], rwkv_g2[l], seq)
        y_r = _rwkv_chunk(r, k, v, a, g, lw, rwkv_k_k[l][None, :], rwkv_k_a[l][None, :],
                          rwkv_r_k[l].reshape(1, rw), rwkv_ln_w[l][None, :], rwkv_ln_b[l][None, :], batch, seq)
        y_a = _swa(proj, attn_sinks[l], bias, batch, seq, q_off, n_kv, q_per_kv)
        y_mix = _out_proj(y_r, y_a, w_out, l)
        x2 = _moe_block(x2, y_mix, gains[1], gt_m, gains[2], sh_f, sc_f, router_w[l].T, router_bias[l][:, None],
                        w_gu_b, expert_w_down, ws_gu_b, shared_w_down[:, None], l, gains[3], gt_f, seq)
    return x2.reshape(batch, seq, d)
```

```python
import functools
import math

import numpy as np
import jax
import jax.numpy as jnp
from jax import lax
from jax.experimental import pallas as pl
from jax.experimental.pallas import tpu as pltpu

F32 = jnp.float32
BF16 = jnp.bfloat16

HEAD_DIM = 64
LANES = 128
WINDOW = 128
N_BUCKETS = 32
MAX_DISTANCE = 128
N_EXPERT_GROUPS = 8
TOPK_GROUPS = 4
TOP_K = 8
ROUTED_SCALE = 2.5
RMS_EPS = 1e-6
GN_EPS = 64e-5
CHUNK = 64
HEADS_PER_GROUP = 4
GROUP_LANES = HEADS_PER_GROUP * HEAD_DIM
VMEM_LIMIT = 60 * 1024 * 1024

HIGHEST = lax.Precision.HIGHEST


def _round_up(n, m):
    return (n + m - 1) // m * m


def _cparams(*sem):
    return pltpu.CompilerParams(dimension_semantics=sem, vmem_limit_bytes=VMEM_LIMIT)


def _dot(a, b):
    return jnp.dot(a, b, preferred_element_type=F32)


def _dot_nt(a, b):
    return lax.dot_general(a, b, (((1,), (1,)), ((), ())), preferred_element_type=F32)


def _dot_tn(a, b):
    return lax.dot_general(a, b, (((0,), (0,)), ((), ())), preferred_element_type=F32)


def _bdot(a, b):
    return lax.dot_general(a, b, (((2,), (1,)), ((0,), (0,))), preferred_element_type=F32)


def _bdot_nt(a, b):
    return lax.dot_general(a, b, (((2,), (2,)), ((0,), (0,))), preferred_element_type=F32)


def _sigmoid(x):
    return 1.0 / (1.0 + jnp.exp(-x))


def _norm_mod(x, gain, shift, scale):
    y = x * lax.rsqrt(jnp.mean(x * x, axis=-1, keepdims=True) + RMS_EPS) * gain
    return y * (1.0 + scale) + shift


def _ada_kernel(c_ref, w_ref, b_ref, o_ref):
    c = c_ref[...]
    cond = c * _sigmoid(c)
    o_ref[...] = _dot(cond.astype(BF16), w_ref[...].astype(BF16)) + b_ref[...]


def _ada(c_pad, ada_w, ada_b):
    nl, d, n = ada_w.shape
    tn = 512
    return pl.pallas_call(
        _ada_kernel,
        out_shape=jax.ShapeDtypeStruct((nl, c_pad.shape[0], n), F32),
        grid=(nl, n // tn),
        in_specs=[
            pl.BlockSpec((c_pad.shape[0], d), lambda l, j: (0, 0)),
            pl.BlockSpec((None, d, tn), lambda l, j: (l, 0, j)),
            pl.BlockSpec((None, 1, tn), lambda l, j: (l, 0, j)),
        ],
        out_specs=pl.BlockSpec((None, c_pad.shape[0], tn), lambda l, j: (l, 0, j)),
        compiler_params=_cparams("parallel", "parallel"),
        name="ada",
    )(c_pad, ada_w, ada_b.reshape(nl, 1, n))


def _bias_kernel(rbt_ref, oh_ref, o_ref):
    o_ref[...] = jnp.dot(rbt_ref[...], oh_ref[...], preferred_element_type=F32, precision=HIGHEST)


def _bucket_onehot():
    qi = np.arange(WINDOW)[:, None]
    kj = np.arange(2 * WINDOW)[None, :]
    dist = qi + WINDOW - kj
    max_exact = N_BUCKETS // 2
    d = np.maximum(dist, 0)
    far = max_exact + (np.log(np.maximum(d, max_exact).astype(np.float32) / np.float32(max_exact))
                       / np.float32(math.log(MAX_DISTANCE / max_exact))
                       * np.float32(N_BUCKETS - max_exact)).astype(np.int32)
    bucket = np.where(d < max_exact, d, np.minimum(far, N_BUCKETS - 1)).reshape(-1)
    return (np.arange(N_BUCKETS)[:, None] == bucket[None, :]).astype(np.float32)


def _bias_table(rel_bias):
    nb, nh = rel_bias.shape
    onehot = jnp.asarray(_bucket_onehot())
    out = pl.pallas_call(
        _bias_kernel,
        out_shape=jax.ShapeDtypeStruct((nh, WINDOW * 2 * WINDOW), F32),
        name="bias_table",
        compiler_params=pltpu.CompilerParams(vmem_limit_bytes=VMEM_LIMIT),
    )(rel_bias.T, onehot)
    return out.reshape(nh, WINDOW, 2 * WINDOW)


def _norm_mod_kernel(x_ref, gain_ref, sh_ref, sc_ref, u_ref):
    u_ref[...] = _norm_mod(x_ref[...], gain_ref[...], sh_ref[...], sc_ref[...]).astype(u_ref.dtype)


def _norm_mod_bf16(x2, gain, shift, scale, seq, tm=512):
    t, d = x2.shape
    bmap = lambda i: ((i * tm) // seq, 0, 0)
    row = pl.BlockSpec((tm, d), lambda i: (i, 0))
    return pl.pallas_call(
        _norm_mod_kernel,
        out_shape=jax.ShapeDtypeStruct((t, d), BF16),
        grid=(t // tm,),
        in_specs=[row, pl.BlockSpec((1, d), lambda i: (0, 0)), pl.BlockSpec((None, 1, d), bmap),
                  pl.BlockSpec((None, 1, d), bmap)],
        out_specs=row,
        compiler_params=_cparams("parallel"),
        name="norm_mod",
    )(x2, gain, shift, scale)


def _in_proj_kernel(u_ref, wt_ref, o_ref, wb_ref):
    @pl.when(pl.program_id(1) == 0)
    def _():
        wb_ref[...] = wt_ref[...].astype(BF16)

    o_ref[...] = _dot_nt(u_ref[...], wb_ref[...])


def _in_proj(u, wt_all, layer, tm=1024, tn=512):
    t, d = u.shape
    n = wt_all.shape[1]
    tm = min(tm, t)
    return pl.pallas_call(
        _in_proj_kernel,
        out_shape=jax.ShapeDtypeStruct((t, n), F32),
        grid=(pl.cdiv(n, tn), t // tm),
        in_specs=[
            pl.BlockSpec((tm, d), lambda j, i: (i, 0)),
            pl.BlockSpec((None, tn, d), lambda j, i: (layer, j, 0)),
        ],
        out_specs=pl.BlockSpec((tm, tn), lambda j, i: (i, j)),
        scratch_shapes=[pltpu.VMEM((tn, d), BF16)],
        compiler_params=_cparams("parallel", "arbitrary"),
        name="in_proj",
    )(u, wt_all)


def _lane_window(start, width):
    lo = start // LANES * LANES
    return lo, _round_up(start + width, LANES), start - lo


def _window_rows(w, start):
    lo, hi, off = _lane_window(start, w.shape[0])
    return jnp.pad(w, ((off, hi - lo - off - w.shape[0]), (0, 0))).astype(BF16)


def _rwkv_prep_kernel(rw, xw_win, xa_win, xg_win, p_ref, prev_ref, mu_ref, w0_ref, w2_ref, a0_ref, a2_ref, g2_ref,
                      r_ref, k_ref, v_ref, a_ref, g_ref, lw_ref, *, blocks_per_seq):
    i = pl.program_id(0)
    cur = p_ref[...]
    first = (i % blocks_per_seq) == 0
    prev_row = jnp.where(first, 0.0, prev_ref[7:8, :])
    rolled = pltpu.roll(cur, 1, axis=0)
    row = lax.broadcasted_iota(jnp.int32, cur.shape, 0)
    shifted = jnp.where(row == 0, prev_row, rolled)
    x = cur + (shifted - cur) * mu_ref[...]
    r_ref[...] = x[:, 0:rw].astype(r_ref.dtype)
    k_ref[...] = x[:, rw:2 * rw].astype(k_ref.dtype)
    v_ref[...] = x[:, 2 * rw:3 * rw].astype(v_ref.dtype)
    xw = x[:, xw_win[0]:xw_win[1]]
    xa = x[:, xa_win[0]:xa_win[1]]
    xg = x[:, xg_win[0]:xg_win[1]]
    z = w0_ref[...] + _dot(jnp.tanh(xw).astype(BF16), w2_ref[...])
    lw_ref[...] = -math.exp(-0.5) * _sigmoid(z)
    a_ref[...] = _sigmoid(a0_ref[...] + _dot(xa.astype(BF16), a2_ref[...])).astype(a_ref.dtype)
    g_ref[...] = _dot(_sigmoid(xg).astype(BF16), g2_ref[...]).astype(g_ref.dtype)


def _rwkv_prep(proj, mu, w0, w2, a0, a2, g2, seq, ts=256):
    t = proj.shape[0]
    rw = w0.shape[1]
    dl, al, gl = w2.shape[0], a2.shape[0], g2.shape[0]
    starts = (3 * rw, 3 * rw + dl, 3 * rw + dl + al)
    wins = [_lane_window(s, w)[:2] for s, w in zip(starts, (dl, al, gl))]
    wblock = wins[2][1]
    w2p, a2p, g2p = (_window_rows(w, s) for w, s in zip((w2, a2, g2), starts))
    mu_p = jnp.pad(mu, ((0, 0), (0, wblock - mu.shape[1])))
    full = lambda i: (0, 0)
    outs = [jax.ShapeDtypeStruct((t, rw), BF16)] * 5 + [jax.ShapeDtypeStruct((t, rw), F32)]
    ospec = pl.BlockSpec((ts, rw), lambda i: (i, 0))
    return pl.pallas_call(
        functools.partial(_rwkv_prep_kernel, rw, *wins, blocks_per_seq=seq // ts),
        out_shape=outs,
        grid=(t // ts,),
        in_specs=[
            pl.BlockSpec((ts, wblock), lambda i: (i, 0)),
            pl.BlockSpec((8, wblock), lambda i: (jnp.maximum(i * (ts // 8) - 1, 0), 0)),
            pl.BlockSpec((1, wblock), full),
            pl.BlockSpec((1, rw), full),
            pl.BlockSpec(w2p.shape, full),
            pl.BlockSpec((1, rw), full),
            pl.BlockSpec(a2p.shape, full),
            pl.BlockSpec(g2p.shape, full),
        ],
        out_specs=[ospec] * 6,
        compiler_params=_cparams("parallel"),
        name="rwkv_prep",
    )(proj, proj, mu_p, w0, w2p, a0, a2p, g2p)


CHUNK_LEVELS = int(math.log2(CHUNK))


def _chunk_constants():
    n = GROUP_LANES
    i = np.arange(n)[:, None]
    j = np.arange(n)[None, :]
    same_head = (i // CHUNK) == (j // CHUNK)
    code = np.zeros((n, n), np.int32)
    for lvl in range(CHUNK_LEVELS, 0, -1):
        code = np.where(same_head & (i > j) & ((i >> lvl) == (j >> lvl)), lvl, code)
    planes = [code == lvl for lvl in range(1, CHUNK_LEVELS + 1)] + [code > 0, same_head]
    return np.stack(planes).astype(np.float32)


def _seg_sum(x, ones_bd):
    return _dot(x.astype(BF16), ones_bd)


def _split3(x):
    hi = x.astype(BF16)
    r1 = x - hi.astype(F32)
    mid = r1.astype(BF16)
    return hi, mid, (r1 - mid.astype(F32)).astype(BF16)


def _rwkv_chunk_kernel(r_ref, k_ref, v_ref, a_ref, g_ref, lw_ref,
                       kk_ref, ka_ref, rk_ref, lnw_ref, lnb_ref, planes_ref,
                       o_ref, s_ref):
    @pl.when(pl.program_id(2) == 0)
    def _():
        s_ref[...] = jnp.zeros_like(s_ref)

    n = GROUP_LANES
    lvl_mask = [planes_ref[l] for l in range(CHUNK_LEVELS)]
    low_mask = planes_ref[CHUNK_LEVELS]
    bd = planes_ref[CHUNK_LEVELS + 1]
    tb = r_ref.shape[0]
    nch = tb // CHUNK
    tri_incl = (lax.broadcasted_iota(jnp.int32, (1, CHUNK, n), 1)
                >= lax.broadcasted_iota(jnp.int32, (1, CHUNK, n), 2) % CHUNK)
    ltri = jnp.where(lax.broadcasted_iota(jnp.int32, (nch, CHUNK, CHUNK), 1)
                     >= lax.broadcasted_iota(jnp.int32, (nch, CHUNK, CHUNK), 2), 1.0, 0.0).astype(BF16)
    eye = jnp.where(lax.broadcasted_iota(jnp.int32, (1, n, n), 1)
                    == lax.broadcasted_iota(jnp.int32, (1, n, n), 2), 1.0, 0.0).astype(BF16)

    def chunks(x):
        return x.reshape(nch, CHUNK, n)

    def tile_heads(x):
        return jnp.concatenate([x] * HEADS_PER_GROUP, axis=1)

    def stack(x):
        return tile_heads(x.astype(BF16)) * bd[None]

    r = r_ref[...].astype(F32)
    k = k_ref[...].astype(F32)
    v = v_ref[...].astype(F32)
    a = a_ref[...].astype(F32)
    lw = lw_ref[...]
    kkp = k * kk_ref[...]
    kk = kkp / jnp.maximum(jnp.sqrt(_seg_sum(kkp * kkp, bd)), 1e-12)
    k2 = k * (1.0 + (a - 1.0) * ka_ref[...])
    kb = kk * a
    bonus = _seg_sum(r * k2 * rk_ref[...], bd) * v

    cum = sum(_bdot(ltri, chunks(piece)) for piece in _split3(lw))
    cum_end = cum[:, CHUNK - 1:CHUNK, :]
    e_in = jnp.exp(cum)
    e_ex = jnp.exp(cum - chunks(lw))
    e_neg = jnp.exp(-cum)
    e_dec = jnp.exp(cum_end - cum)
    decay = jnp.exp(cum_end)
    r_t = chunks(r) * e_in
    a_t = -chunks(kk) * e_ex
    kb3 = chunks(kb)
    k23 = chunks(k2)

    ar = jnp.concatenate([a_t, r_t], axis=1).astype(BF16)
    v_s = stack(chunks(v))
    ab_rb = _bdot_nt(ar, stack(kb3 * e_neg))
    ak_rk = _bdot_nt(ar, stack(k23 * e_neg))
    rb_w = jnp.where(tri_incl, ab_rb[:, CHUNK:], 0.0).astype(BF16)
    rk_w = jnp.where(tri_incl, ak_rk[:, CHUNK:], 0.0).astype(BF16)
    amat = tile_heads(ab_rb[:, :CHUNK].astype(BF16))
    akmat = tile_heads(ak_rk[:, :CHUNK].astype(BF16)) * low_mask[None]

    tinv = eye + amat * lvl_mask[0][None]
    for lvl in range(1, CHUNK_LEVELS):
        tinv = tinv + _bdot(tinv, _bdot(amat * lvl_mask[lvl][None], tinv).astype(BF16)).astype(BF16)

    w_s = _bdot(tinv, stack(a_t)).astype(BF16)
    z_s = _bdot(tinv, _bdot(akmat, v_s).astype(BF16)).astype(BF16)
    bh_s = stack(kb3 * e_dec)
    kh_s = stack(k23 * e_dec)
    r_eff = (r_t + _bdot(rb_w, w_s)).astype(BF16)
    y0 = _bdot(rb_w, z_s) + _bdot(rk_w, v_s)

    s = s_ref[...]
    ys = []
    for ci in range(nch):
        p_mat = _dot_tn(w_s[ci], bh_s[ci]).astype(BF16)
        q_mat = _dot_tn(z_s[ci], bh_s[ci]) + _dot_tn(v_s[ci], kh_s[ci])
        s_b = s.astype(BF16)
        ys.append(_dot_nt(r_eff[ci], s_b) + y0[ci])
        s = s * decay[ci] + _dot(s_b, p_mat) + q_mat
    s_ref[...] = s

    y = jnp.concatenate(ys, axis=0)
    inv_n = 1.0 / HEAD_DIM
    mean = _seg_sum(y, bd) * inv_n
    dlt = y - mean
    var = _seg_sum(dlt * dlt, bd) * inv_n
    yn = dlt * lax.rsqrt(var + GN_EPS) * lnw_ref[...] + lnb_ref[...]
    o_ref[...] = ((yn + bonus) * g_ref[...].astype(F32)).astype(o_ref.dtype)


def _rwkv_chunk(r, k, v, a, g, lw, k_k, k_a, r_k, ln_w, ln_b, batch, seq, tb=512):
    t, rw = r.shape
    n = GROUP_LANES
    planes = jnp.asarray(_chunk_constants(), dtype=BF16)
    nblk = seq // tb
    tmap = lambda b, h, c: (b * nblk + c, h)
    pmap = lambda b, h, c: (0, h)
    tspec = pl.BlockSpec((tb, n), tmap)
    pspec = pl.BlockSpec((1, n), pmap)
    return pl.pallas_call(
        _rwkv_chunk_kernel,
        out_shape=jax.ShapeDtypeStruct((t, rw), BF16),
        grid=(batch, rw // n, nblk),
        in_specs=[tspec] * 6 + [pspec] * 5 + [pl.BlockSpec(planes.shape, lambda b, h, c: (0, 0, 0))],
        out_specs=pl.BlockSpec((tb, n), tmap),
        scratch_shapes=[pltpu.VMEM((n, n), F32)],
        compiler_params=_cparams("parallel", "parallel", "arbitrary"),
        name="rwkv_chunk",
    )(r, k, v, a, g, lw, k_k, k_a, r_k, ln_w, ln_b, planes)


def _swa_kernel(sink_ref, *refs, q_per_kv, q_off, k_off, v_off, n_q_tiles):
    q_refs = refs[:n_q_tiles]
    kp_ref, kc_ref, vp_ref, vc_ref, bias_ref, o_ref = refs[n_q_tiles:]
    j = pl.program_id(1)
    nblk = pl.program_id(2)
    blk = WINDOW
    upper = lax.broadcasted_iota(jnp.int32, (1, LANES), 1) >= HEAD_DIM

    def both_halves(x, start_col):
        in_upper = (start_col // HEAD_DIM) % 2 == 1
        sel = jnp.where(upper == in_upper, x, 0.0)
        return sel + pltpu.roll(sel, HEAD_DIM, axis=1)

    kw = both_halves(jnp.concatenate([kp_ref[...], kc_ref[...]], axis=0), k_off + HEAD_DIM * j).astype(BF16)
    vw = both_halves(jnp.concatenate([vp_ref[...], vc_ref[...]], axis=0), v_off + HEAD_DIM * j)
    v_half = (jnp.where(upper, 0.0, vw).astype(BF16), jnp.where(upper, vw, 0.0).astype(BF16))
    qi = lax.broadcasted_iota(jnp.int32, (blk, 2 * blk), 0)
    kj = lax.broadcasted_iota(jnp.int32, (blk, 2 * blk), 1)
    valid = (kj > qi) & (kj <= qi + blk) & ((nblk > 0) | (kj >= blk))
    scale = HEAD_DIM ** -0.5
    for p in range(q_per_kv // 2):
        acc = None
        for half in range(2):
            g = 2 * p + half
            col = q_off % LANES + HEAD_DIM * g
            q_tile = q_refs[col // LANES][...] * scale
            q_upper = (col // HEAD_DIM) % 2 == 1
            qm = jnp.where(upper == q_upper, q_tile, 0.0).astype(BF16)
            sink = sink_ref[j * q_per_kv + g]
            s = _dot_nt(qm, kw) + bias_ref[g]
            s = jnp.where(valid, s, -jnp.inf)
            m = jnp.maximum(jnp.max(s, axis=-1, keepdims=True), sink)
            pr = jnp.exp(s - m)
            denom = jnp.sum(pr, axis=-1, keepdims=True) + jnp.exp(sink - m)
            o = _dot(pr.astype(BF16), v_half[half]) / denom
            acc = o if acc is None else acc + o
        o_ref[:, p * LANES:(p + 1) * LANES] = acc.astype(o_ref.dtype)


def _swa(proj, sinks, bias, batch, seq, q_off, n_kv, q_per_kv):
    t = proj.shape[0]
    blk = WINDOW
    nb = seq // blk
    qw = q_per_kv * HEAD_DIM
    k_off = q_off + n_kv * qw
    v_off = k_off + n_kv * HEAD_DIM
    n_q_tiles = (q_off % LANES + qw + LANES - 1) // LANES
    cur = lambda b, j, n: b * nb + n
    prev = lambda b, j, n: b * nb + jnp.maximum(n - 1, 0)
    tile = lambda rowf, colf: pl.BlockSpec((blk, LANES), lambda b, j, n: (rowf(b, j, n), colf(j)))
    q_specs = [tile(cur, lambda j, i=i: q_off // LANES + (qw // LANES) * j + i) for i in range(n_q_tiles)]
    k_col = lambda j: (k_off + HEAD_DIM * j) // LANES
    v_col = lambda j: (v_off + HEAD_DIM * j) // LANES
    return pl.pallas_call(
        functools.partial(_swa_kernel, q_per_kv=q_per_kv, q_off=q_off, k_off=k_off, v_off=v_off,
                          n_q_tiles=n_q_tiles),
        out_shape=jax.ShapeDtypeStruct((t, n_kv * qw), BF16),
        grid=(batch, n_kv, nb),
        in_specs=[pl.BlockSpec(memory_space=pltpu.SMEM)] + q_specs + [
            tile(prev, k_col), tile(cur, k_col), tile(prev, v_col), tile(cur, v_col),
            pl.BlockSpec((q_per_kv, blk, 2 * blk), lambda b, j, n: (j, 0, 0)),
        ],
        out_specs=pl.BlockSpec((blk, qw), lambda b, j, n: (cur(b, j, n), j)),
        compiler_params=_cparams("parallel", "parallel", "parallel"),
        name="swa",
    )(sinks, *([proj] * (n_q_tiles + 4)), bias)


def _out_proj_kernel(yr_ref, ya_ref, w_ref, o_ref, wb_ref):
    @pl.when(pl.program_id(1) == 0)
    def _():
        wb_ref[...] = w_ref[...].astype(BF16)

    rw = yr_ref.shape[1]
    o_ref[...] = _dot(yr_ref[...], wb_ref[0:rw, :]) + _dot(ya_ref[...], wb_ref[rw:, :])


def _out_proj(y_r, y_a, w_all, layer, tm=1024, tn=512):
    t, rw = y_r.shape
    aw = y_a.shape[1]
    d = w_all.shape[2]
    tm = min(tm, t)
    return pl.pallas_call(
        _out_proj_kernel,
        out_shape=jax.ShapeDtypeStruct((t, d), F32),
        grid=(d // tn, t // tm),
        in_specs=[
            pl.BlockSpec((tm, rw), lambda j, i: (i, 0)),
            pl.BlockSpec((tm, aw), lambda j, i: (i, 0)),
            pl.BlockSpec((None, rw + aw, tn), lambda j, i: (layer, 0, j)),
        ],
        out_specs=pl.BlockSpec((tm, tn), lambda j, i: (i, j)),
        scratch_shapes=[pltpu.VMEM((rw + aw, tn), BF16)],
        compiler_params=_cparams("parallel", "arbitrary"),
        name="out_proj",
    )(y_r, y_a, w_all)


def _rank(vals):
    n = vals.shape[0]
    idx = lax.broadcasted_iota(jnp.int32, vals.shape, 0)
    rank = jnp.zeros(vals.shape, jnp.int32)
    for e in range(n):
        row = vals[e:e + 1, :]
        beats = (row > vals) | ((row == vals) & (e < idx))
        rank = rank + beats.astype(jnp.int32)
    return rank


def _router_kernel(x_ref, y_ref, gain_y_ref, gate_y_ref, gain_ref, sh_ref, sc_ref, rwt_ref, rb_ref,
                   xn_ref, u_ref, idx_ref, pos_ref, gate_ref, cnt_ref, carry_ref):
    @pl.when(pl.program_id(0) == 0)
    def _():
        carry_ref[...] = jnp.zeros_like(carry_ref)

    y = y_ref[...]
    yn = y * lax.rsqrt(jnp.mean(y * y, axis=-1, keepdims=True) + RMS_EPS) * gain_y_ref[...]
    xn = x_ref[...] + gate_y_ref[...] * yn
    xn_ref[...] = xn
    u = _norm_mod(xn, gain_ref[...], sh_ref[...], sc_ref[...])
    u_ref[...] = u
    logits = lax.dot_general(rwt_ref[...], u, (((1,), (1,)), ((), ())),
                             preferred_element_type=F32, precision=HIGHEST)
    scores = _sigmoid(logits)
    sel = scores + rb_ref[:, 0:1]
    ne, tm = sel.shape
    gsz = ne // N_EXPERT_GROUPS
    sub = lax.broadcasted_iota(jnp.int32, (gsz, tm), 0)
    gs_rows = []
    for gi in range(N_EXPERT_GROUPS):
        grp = sel[gi * gsz:(gi + 1) * gsz, :]
        m1 = jnp.max(grp, axis=0, keepdims=True)
        i1 = jnp.min(jnp.where(grp == m1, sub, gsz), axis=0, keepdims=True)
        m2 = jnp.max(jnp.where(sub == i1, -jnp.inf, grp), axis=0, keepdims=True)
        gs_rows.append(m1 + m2)
    gkeep = _rank(jnp.concatenate(gs_rows, axis=0)) < TOPK_GROUPS
    ekeep = jnp.concatenate(
        [jnp.broadcast_to(gkeep[gi:gi + 1, :], (gsz, tm)) for gi in range(N_EXPERT_GROUPS)], axis=0)
    rank = _rank(jnp.where(ekeep, sel, -jnp.inf))
    chosen = rank < TOP_K
    wts = jnp.where(chosen, scores, 0.0)
    gates_t = wts / jnp.sum(wts, axis=0, keepdims=True) * ROUTED_SCALE

    cf = jnp.where(chosen, 1.0, 0.0)
    before = (lax.broadcasted_iota(jnp.int32, (tm, tm), 0) < lax.broadcasted_iota(jnp.int32, (tm, tm), 1))
    carry = carry_ref[:, 0:1]
    pos = carry + _dot(cf.astype(BF16), jnp.where(before, 1.0, 0.0).astype(BF16))
    carry_ref[...] = jnp.broadcast_to(carry + jnp.sum(cf, axis=1, keepdims=True), carry_ref.shape)
    cnt_ref[...] = carry_ref[...]

    eidx = lax.broadcasted_iota(jnp.int32, (ne, tm), 0)
    rows_i, rows_p, rows_g = [], [], []
    for k in range(TOP_K):
        mk = rank == k
        rows_i.append(jnp.sum(jnp.where(mk, eidx, 0), axis=0, keepdims=True))
        rows_p.append(jnp.sum(jnp.where(mk, pos, 0.0), axis=0, keepdims=True))
        rows_g.append(jnp.sum(jnp.where(mk, gates_t, 0.0), axis=0, keepdims=True))
    idx_ref[...] = jnp.concatenate(rows_i, axis=0)
    pos_ref[...] = jnp.concatenate(rows_p, axis=0).astype(jnp.int32)
    gate_ref[...] = jnp.concatenate(rows_g, axis=0)


def _router(x2, y, gain_y, gate_y, gain, shift, scale, rw_t, rb, seq, tm=256):
    t, d = x2.shape
    ne = rw_t.shape[0]
    bmap = lambda i: ((i * tm) // seq, 0, 0)
    pick = lambda dt: jax.ShapeDtypeStruct((TOP_K, t), dt)
    pspec = pl.BlockSpec((TOP_K, tm), lambda i: (0, i))
    row = pl.BlockSpec((tm, d), lambda i: (i, 0))
    vec = pl.BlockSpec((1, d), lambda i: (0, 0))
    per_batch = pl.BlockSpec((None, 1, d), bmap)
    return pl.pallas_call(
        _router_kernel,
        out_shape=[jax.ShapeDtypeStruct((t, d), F32), jax.ShapeDtypeStruct((t, d), F32),
                   pick(jnp.int32), pick(jnp.int32), pick(F32), jax.ShapeDtypeStruct((ne, LANES), F32)],
        grid=(t // tm,),
        in_specs=[row, row, vec, per_batch, vec, per_batch, per_batch,
                  pl.BlockSpec((ne, d), lambda i: (0, 0)), pl.BlockSpec((ne, 1), lambda i: (0, 0))],
        out_specs=[row, row, pspec, pspec, pspec, pl.BlockSpec((ne, LANES), lambda i: (0, 0))],
        scratch_shapes=[pltpu.VMEM((ne, LANES), F32)],
        compiler_params=_cparams("arbitrary"),
        name="router",
    )(x2, y, gain_y, gate_y, gain, shift, scale, rw_t, rb)


def _row_copy(src, src_row, dst, dst_row, sem):
    return pltpu.make_async_copy(src.at[pl.ds(src_row, 1), :], dst.at[pl.ds(dst_row, 1), :], sem)


def _dispatch_kernel(rows_ref, u_ref, x_hbm, sem, *, n_pick):
    td = u_ref.shape[0]

    def body(tok, carry):
        for k in range(n_pick):
            _row_copy(u_ref, tok, x_hbm, rows_ref[0, k * td + tok], sem).start()
        return carry

    lax.fori_loop(0, td, body, 0)
    for k in range(n_pick):
        pltpu.make_async_copy(u_ref, x_hbm.at[pl.ds(0, td), :], sem).wait()


def _dispatch(u, rows_tiled, n_rows, td):
    t, d = u.shape
    n_pick = rows_tiled.shape[2] // td
    return pl.pallas_call(
        functools.partial(_dispatch_kernel, n_pick=n_pick),
        out_shape=jax.ShapeDtypeStruct((n_rows, d), F32),
        grid=(t // td,),
        in_specs=[
            pl.BlockSpec((None, 1, n_pick * td), lambda i: (i, 0, 0), memory_space=pltpu.SMEM),
            pl.BlockSpec((td, d), lambda i: (i, 0)),
        ],
        out_specs=pl.BlockSpec(memory_space=pl.ANY),
        scratch_shapes=[pltpu.SemaphoreType.DMA(())],
        compiler_params=_cparams("arbitrary"),
        name="moe_dispatch",
    )(rows_tiled, u)


def _experts_kernel(tile_ref, exp_ref, lo_ref, hi_ref, first_ref, n_ref, x_ref, wgu_ref, wd_ref, y_ref):
    p = pl.program_id(0)

    @pl.when(p < n_ref[0])
    def _():
        ff = wd_ref.shape[0]
        h = _dot(x_ref[...].astype(BF16), wgu_ref[...].astype(BF16))
        hg = h[:, :ff]
        hu = h[:, ff:]
        act = (hg * _sigmoid(hg) * hu).astype(BF16)
        row = lax.broadcasted_iota(jnp.int32, (x_ref.shape[0], 1), 0)
        mine = (row >= lo_ref[p]) & (row < hi_ref[p])

        @pl.when(first_ref[p] == 1)
        def _():
            y_ref[...] = jnp.where(mine, _dot(act, wd_ref[...].astype(BF16)), 0.0)

        @pl.when(first_ref[p] == 0)
        def _():
            y_ref[...] += jnp.where(mine, _dot(act, wd_ref[...].astype(BF16)), 0.0)


def _experts(x_sorted, work, w_gu, w_down, layer, tm):
    n_rows, d = x_sorted.shape
    _, ne1, _, ff2 = w_gu.shape
    ff = ff2 // 2
    n_work_max = n_rows // tm + ne1 - 1
    return pl.pallas_call(
        _experts_kernel,
        out_shape=jax.ShapeDtypeStruct((n_rows, d), F32),
        grid_spec=pltpu.PrefetchScalarGridSpec(
            num_scalar_prefetch=6,
            grid=(n_work_max,),
            in_specs=[
                pl.BlockSpec((tm, d), lambda p, tile, exp, *_: (tile[p], 0)),
                pl.BlockSpec((None, None, d, ff2), lambda p, tile, exp, *_: (layer, exp[p], 0, 0)),
                pl.BlockSpec((None, None, ff, d), lambda p, tile, exp, *_: (layer, exp[p], 0, 0)),
            ],
            out_specs=pl.BlockSpec((tm, d), lambda p, tile, exp, *_: (tile[p], 0)),
        ),
        compiler_params=_cparams("arbitrary"),
        name="moe_experts",
    )(*work, x_sorted, w_gu, w_down)


def _expert_work_list(counts, n_rows, tm):
    ne1 = counts.shape[0]
    n_work_max = n_rows // tm + ne1 - 1
    ends = jnp.cumsum(counts)
    starts = ends - counts
    first_tile = starts // tm
    n_tile = jnp.where(counts > 0, (ends - 1) // tm - first_tile + 1, 0)
    work_end = jnp.cumsum(n_tile)
    n_work = work_end[-1:]
    p = jnp.minimum(jnp.arange(n_work_max, dtype=jnp.int32), n_work - 1)
    exp = jnp.searchsorted(work_end, p, side="right").astype(jnp.int32)
    tile = first_tile[exp] + p - (work_end - n_tile)[exp]
    lo = jnp.maximum(starts[exp] - tile * tm, 0)
    hi = jnp.minimum(ends[exp] - tile * tm, tm)
    first = jnp.concatenate([jnp.ones((1,), jnp.int32), (tile[1:] != tile[:-1]).astype(jnp.int32)])
    as_i32 = lambda a: a.astype(jnp.int32)
    return tuple(map(as_i32, (tile, exp, lo, hi, first, n_work))), starts


def _combine_kernel(rows_ref, nrows_ref, g_ref, x_ref, ys_ref, gain_ref, gate_ref, y_hbm, o_ref, buf, sems,
                    *, n_pick):
    i = pl.program_id(0)
    tt = x_ref.shape[0]
    slot = i % 2

    def gather(table_ref, dst_slot):
        def body(tok, carry):
            for k in range(n_pick):
                _row_copy(y_hbm, table_ref[0, k * tt + tok], buf.at[dst_slot, k], tok, sems.at[dst_slot]).start()
            return carry

        lax.fori_loop(0, tt, body, 0)

    @pl.when(i == 0)
    def _():
        gather(rows_ref, 0)

    @pl.when(i + 1 < pl.num_programs(0))
    def _():
        gather(nrows_ref, 1 - slot)

    for k in range(n_pick):
        pltpu.make_async_copy(y_hbm.at[pl.ds(0, tt), :], buf.at[slot, k], sems.at[slot]).wait()
    g = g_ref[...]
    acc = ys_ref[...]
    for k in range(n_pick):
        acc = acc + buf[slot, k] * g[:, k:k + 1]
    yn = acc * lax.rsqrt(jnp.mean(acc * acc, axis=-1, keepdims=True) + RMS_EPS) * gain_ref[...]
    o_ref[...] = x_ref[...] + gate_ref[...] * yn


def _combine(y_routed, y_shared, rows_tiled, gates, x2, gain, gate, seq, tt):
    t, d = x2.shape
    n_pick = rows_tiled.shape[2] // tt
    n_tiles = t // tt
    bmap = lambda i: ((i * tt) // seq, 0, 0)
    row = pl.BlockSpec((tt, d), lambda i: (i, 0))
    table = lambda imap: pl.BlockSpec((None, 1, n_pick * tt), imap, memory_space=pltpu.SMEM)
    return pl.pallas_call(
        functools.partial(_combine_kernel, n_pick=n_pick),
        out_shape=jax.ShapeDtypeStruct((t, d), F32),
        grid=(n_tiles,),
        in_specs=[
            table(lambda i: (i, 0, 0)),
            table(lambda i: (jnp.minimum(i + 1, n_tiles - 1), 0, 0)),
            pl.BlockSpec((tt, gates.shape[1]), lambda i: (i, 0)),
            row,
            row,
            pl.BlockSpec((1, d), lambda i: (0, 0)),
            pl.BlockSpec((None, 1, d), bmap),
            pl.BlockSpec(memory_space=pl.ANY),
        ],
        out_specs=row,
        scratch_shapes=[pltpu.VMEM((2, n_pick, tt, d), F32), pltpu.SemaphoreType.DMA((2,))],
        compiler_params=_cparams("arbitrary"),
        name="moe_combine",
    )(rows_tiled, rows_tiled, gates, x2, y_shared, gain, gate, y_routed)


def _tile_rows(rows, tile):
    n_pick, t = rows.shape
    return rows.reshape(n_pick, t // tile, tile).transpose(1, 0, 2).reshape(t // tile, 1, n_pick * tile)


def _moe_block(x2, y_mix, gain_mix, gate_mix, gain_in, shift, scale, rw_t, rb, w_gu, w_down, ws_gu, ws_down, layer,
               gain_out, gate, seq, tm=128, td=128, tt=64):
    t, d = x2.shape
    ne = rw_t.shape[0]
    x2, u, idx, pos, gate_k, counts = _router(x2, y_mix, gain_mix, gate_mix, gain_in, shift, scale, rw_t, rb, seq)
    work, starts = _expert_work_list(counts[:, 0].astype(jnp.int32), TOP_K * t, tm)
    hit = idx[:, :, None] == jnp.arange(ne, dtype=jnp.int32)[None, None, :]
    rows = jnp.sum(jnp.where(hit, starts[None, None, :], 0), axis=-1) + pos
    x_sorted = _dispatch(u, _tile_rows(rows, td), TOP_K * t, td)
    y_routed = _experts(x_sorted, work, w_gu, w_down, layer, tm)
    shared_work, _ = _expert_work_list(jnp.full((1,), t, jnp.int32), t, tm)
    y_shared = _experts(u, shared_work, ws_gu, ws_down, layer, tm)
    return _combine(y_routed, y_shared, _tile_rows(rows, tt), gate_k.T, x2, gain_out, gate, seq, tt)


def kernel(x, c, rel_bias, ada_w, ada_b, norm_gains, w_in, shift_mu, rwkv_w0, rwkv_w2, rwkv_a0, rwkv_a2, rwkv_g2,
           rwkv_k_k, rwkv_k_a, rwkv_r_k, rwkv_ln_w, rwkv_ln_b, attn_sinks, w_out, router_w, router_bias,
           expert_w_gu, expert_w_down, shared_w_gu, shared_w_down):
    batch, seq, d = x.shape
    depth = ada_w.shape[0]
    t = batch * seq
    rw = rwkv_w0.shape[1]
    dl, al, gl = rwkv_w2.shape[1], rwkv_a2.shape[1], rwkv_g2.shape[1]
    aw = w_out.shape[1] - rw
    n_q = aw // HEAD_DIM
    n_kv = (w_in.shape[2] - (3 * rw + dl + al + gl) - aw) // (2 * HEAD_DIM)
    q_per_kv = n_q // n_kv
    q_off = 3 * rw + dl + al + gl
    assert q_off % HEAD_DIM == 0, "attention heads must start at a multiple of the head width"

    c_pad = jnp.pad(c, ((0, 8 - batch % 8 if batch % 8 else 0), (0, 0)))
    ada = _ada(c_pad, ada_w, ada_b)[:, :batch]
    bias = _bias_table(rel_bias)
    w_in_t = jnp.swapaxes(w_in, 1, 2)
    w_gu_b = expert_w_gu
    ws_gu_b = shared_w_gu[:, None]

    x2 = x.reshape(t, d)
    for l in range(depth):
        sh_m, sc_m, gt_m, sh_f, sc_f, gt_f = [a.reshape(batch, 1, d) for a in jnp.split(ada[l], 6, axis=-1)]
        gains = norm_gains[l].reshape(4, 1, d)

        proj = _in_proj(_norm_mod_bf16(x2, gains[0], sh_m, sc_m, seq), w_in_t, l)

        r, k, v, a, g, lw = _rwkv_prep(proj, shift_mu[l][None, :], rwkv_w0[l][None, :], rwkv_w2[l],
                                       rwkv_a0[l][None, :], rwkv_a2[l], rwkv_g2[l], seq)
        y_r = _rwkv_chunk(r, k, v, a, g, lw, rwkv_k_k[l][None, :], rwkv_k_a[l][None, :],
                          rwkv_r_k[l].reshape(1, rw), rwkv_ln_w[l][None, :], rwkv_ln_b[l][None, :], batch, seq)
        y_a = _swa(proj, attn_sinks[l], bias, batch, seq, q_off, n_kv, q_per_kv)
        y_mix = _out_proj(y_r, y_a, w_out, l)
        x2 = _moe_block(x2, y_mix, gains[1], gt_m, gains[2], sh_f, sc_f, router_w[l].T, router_bias[l][:, None],
                        w_gu_b, expert_w_down, ws_gu_b, shared_w_down[:, None], l, gains[3], gt_f, seq)
    return x2.reshape(batch, seq, d)
```

```python
import functools
import math

import numpy as np
import jax
import jax.numpy as jnp
from jax import lax
from jax.experimental import pallas as pl
from jax.experimental.pallas import tpu as pltpu

F32 = jnp.float32
BF16 = jnp.bfloat16

HEAD_DIM = 64
LANES = 128
WINDOW = 128
N_BUCKETS = 32
MAX_DISTANCE = 128
N_EXPERT_GROUPS = 8
TOPK_GROUPS = 4
TOP_K = 8
ROUTED_SCALE = 2.5
RMS_EPS = 1e-6
GN_EPS = 64e-5
CHUNK = 64
HEADS_PER_GROUP = 4
GROUP_LANES = HEADS_PER_GROUP * HEAD_DIM
VMEM_LIMIT = 60 * 1024 * 1024

HIGHEST = lax.Precision.HIGHEST


def _round_up(n, m):
    return (n + m - 1) // m * m


def _cparams(*sem):
    return pltpu.CompilerParams(dimension_semantics=sem, vmem_limit_bytes=VMEM_LIMIT)


def _dot(a, b):
    return jnp.dot(a, b, preferred_element_type=F32)


def _dot_nt(a, b):
    return lax.dot_general(a, b, (((1,), (1,)), ((), ())), preferred_element_type=F32)


def _dot_tn(a, b):
    return lax.dot_general(a, b, (((0,), (0,)), ((), ())), preferred_element_type=F32)


def _bdot(a, b):
    return lax.dot_general(a, b, (((2,), (1,)), ((0,), (0,))), preferred_element_type=F32)


def _bdot_nt(a, b):
    return lax.dot_general(a, b, (((2,), (2,)), ((0,), (0,))), preferred_element_type=F32)


def _sigmoid(x):
    return 1.0 / (1.0 + jnp.exp(-x))


def _norm_mod(x, gain, shift, scale):
    y = x * lax.rsqrt(jnp.mean(x * x, axis=-1, keepdims=True) + RMS_EPS) * gain
    return y * (1.0 + scale) + shift


def _ada_kernel(c_ref, w_ref, b_ref, o_ref):
    c = c_ref[...]
    cond = c * _sigmoid(c)
    o_ref[...] = _dot(cond.astype(BF16), w_ref[...].astype(BF16)) + b_ref[...]


def _ada(c_pad, ada_w, ada_b):
    nl, d, n = ada_w.shape
    tn = 512
    return pl.pallas_call(
        _ada_kernel,
        out_shape=jax.ShapeDtypeStruct((nl, c_pad.shape[0], n), F32),
        grid=(nl, n // tn),
        in_specs=[
            pl.BlockSpec((c_pad.shape[0], d), lambda l, j: (0, 0)),
            pl.BlockSpec((None, d, tn), lambda l, j: (l, 0, j)),
            pl.BlockSpec((None, 1, tn), lambda l, j: (l, 0, j)),
        ],
        out_specs=pl.BlockSpec((None, c_pad.shape[0], tn), lambda l, j: (l, 0, j)),
        compiler_params=_cparams("parallel", "parallel"),
        name="ada",
    )(c_pad, ada_w, ada_b.reshape(nl, 1, n))


def _bias_kernel(rbt_ref, oh_ref, o_ref):
    o_ref[...] = jnp.dot(rbt_ref[...], oh_ref[...], preferred_element_type=F32, precision=HIGHEST)


def _bucket_onehot():
    qi = np.arange(WINDOW)[:, None]
    kj = np.arange(2 * WINDOW)[None, :]
    dist = qi + WINDOW - kj
    max_exact = N_BUCKETS // 2
    d = np.maximum(dist, 0)
    far = max_exact + (np.log(np.maximum(d, max_exact).astype(np.float32) / np.float32(max_exact))
                       / np.float32(math.log(MAX_DISTANCE / max_exact))
                       * np.float32(N_BUCKETS - max_exact)).astype(np.int32)
    bucket = np.where(d < max_exact, d, np.minimum(far, N_BUCKETS - 1)).reshape(-1)
    return (np.arange(N_BUCKETS)[:, None] == bucket[None, :]).astype(np.float32)


def _bias_table(rel_bias):
    nb, nh = rel_bias.shape
    onehot = jnp.asarray(_bucket_onehot())
    out = pl.pallas_call(
        _bias_kernel,
        out_shape=jax.ShapeDtypeStruct((nh, WINDOW * 2 * WINDOW), F32),
        name="bias_table",
        compiler_params=pltpu.CompilerParams(vmem_limit_bytes=VMEM_LIMIT),
    )(rel_bias.T, onehot)
    return out.reshape(nh, WINDOW, 2 * WINDOW)


def _norm_mod_kernel(x_ref, gain_ref, sh_ref, sc_ref, u_ref):
    u_ref[...] = _norm_mod(x_ref[...], gain_ref[...], sh_ref[...], sc_ref[...]).astype(u_ref.dtype)


def _norm_mod_bf16(x2, gain, shift, scale, seq, tm=512):
    t, d = x2.shape
    bmap = lambda i: ((i * tm) // seq, 0, 0)
    row = pl.BlockSpec((tm, d), lambda i: (i, 0))
    return pl.pallas_call(
        _norm_mod_kernel,
        out_shape=jax.ShapeDtypeStruct((t, d), BF16),
        grid=(t // tm,),
        in_specs=[row, pl.BlockSpec((1, d), lambda i: (0, 0)), pl.BlockSpec((None, 1, d), bmap),
                  pl.BlockSpec((None, 1, d), bmap)],
        out_specs=row,
        compiler_params=_cparams("parallel"),
        name="norm_mod",
    )(x2, gain, shift, scale)


def _in_proj_kernel(u_ref, wt_ref, o_ref, wb_ref):
    @pl.when(pl.program_id(1) == 0)
    def _():
        wb_ref[...] = wt_ref[...].astype(BF16)

    o_ref[...] = _dot_nt(u_ref[...], wb_ref[...])


def _in_proj(u, wt_all, layer, tm=1024, tn=512):
    t, d = u.shape
    n = wt_all.shape[1]
    tm = min(tm, t)
    return pl.pallas_call(
        _in_proj_kernel,
        out_shape=jax.ShapeDtypeStruct((t, n), F32),
        grid=(pl.cdiv(n, tn), t // tm),
        in_specs=[
            pl.BlockSpec((tm, d), lambda j, i: (i, 0)),
            pl.BlockSpec((None, tn, d), lambda j, i: (layer, j, 0)),
        ],
        out_specs=pl.BlockSpec((tm, tn), lambda j, i: (i, j)),
        scratch_shapes=[pltpu.VMEM((tn, d), BF16)],
        compiler_params=_cparams("parallel", "arbitrary"),
        name="in_proj",
    )(u, wt_all)


def _lane_window(start, width):
    lo = start // LANES * LANES
    return lo, _round_up(start + width, LANES), start - lo


def _window_rows(w, start):
    lo, hi, off = _lane_window(start, w.shape[0])
    return jnp.pad(w, ((off, hi - lo - off - w.shape[0]), (0, 0))).astype(BF16)


def _rwkv_prep_kernel(rw, xw_win, xa_win, xg_win, p_ref, prev_ref, mu_ref, w0_ref, w2_ref, a0_ref, a2_ref, g2_ref,
                      r_ref, k_ref, v_ref, a_ref, g_ref, lw_ref, *, blocks_per_seq):
    i = pl.program_id(0)
    cur = p_ref[...]
    first = (i % blocks_per_seq) == 0
    prev_row = jnp.where(first, 0.0, prev_ref[7:8, :])
    rolled = pltpu.roll(cur, 1, axis=0)
    row = lax.broadcasted_iota(jnp.int32, cur.shape, 0)
    shifted = jnp.where(row == 0, prev_row, rolled)
    x = cur + (shifted - cur) * mu_ref[...]
    r_ref[...] = x[:, 0:rw].astype(r_ref.dtype)
    k_ref[...] = x[:, rw:2 * rw].astype(k_ref.dtype)
    v_ref[...] = x[:, 2 * rw:3 * rw].astype(v_ref.dtype)
    xw = x[:, xw_win[0]:xw_win[1]]
    xa = x[:, xa_win[0]:xa_win[1]]
    xg = x[:, xg_win[0]:xg_win[1]]
    z = w0_ref[...] + _dot(jnp.tanh(xw).astype(BF16), w2_ref[...])
    lw_ref[...] = -math.exp(-0.5) * _sigmoid(z)
    a_ref[...] = _sigmoid(a0_ref[...] + _dot(xa.astype(BF16), a2_ref[...])).astype(a_ref.dtype)
    g_ref[...] = _dot(_sigmoid(xg).astype(BF16), g2_ref[...]).astype(g_ref.dtype)


def _rwkv_prep(proj, mu, w0, w2, a0, a2, g2, seq, ts=256):
    t = proj.shape[0]
    rw = w0.shape[1]
    dl, al, gl = w2.shape[0], a2.shape[0], g2.shape[0]
    starts = (3 * rw, 3 * rw + dl, 3 * rw + dl + al)
    wins = [_lane_window(s, w)[:2] for s, w in zip(starts, (dl, al, gl))]
    wblock = wins[2][1]
    w2p, a2p, g2p = (_window_rows(w, s) for w, s in zip((w2, a2, g2), starts))
    mu_p = jnp.pad(mu, ((0, 0), (0, wblock - mu.shape[1])))
    full = lambda i: (0, 0)
    outs = [jax.ShapeDtypeStruct((t, rw), BF16)] * 5 + [jax.ShapeDtypeStruct((t, rw), F32)]
    ospec = pl.BlockSpec((ts, rw), lambda i: (i, 0))
    return pl.pallas_call(
        functools.partial(_rwkv_prep_kernel, rw, *wins, blocks_per_seq=seq // ts),
        out_shape=outs,
        grid=(t // ts,),
        in_specs=[
            pl.BlockSpec((ts, wblock), lambda i: (i, 0)),
            pl.BlockSpec((8, wblock), lambda i: (jnp.maximum(i * (ts // 8) - 1, 0), 0)),
            pl.BlockSpec((1, wblock), full),
            pl.BlockSpec((1, rw), full),
            pl.BlockSpec(w2p.shape, full),
            pl.BlockSpec((1, rw), full),
            pl.BlockSpec(a2p.shape, full),
            pl.BlockSpec(g2p.shape, full),
        ],
        out_specs=[ospec] * 6,
        compiler_params=_cparams("parallel"),
        name="rwkv_prep",
    )(proj, proj, mu_p, w0, w2p, a0, a2p, g2p)


CHUNK_LEVELS = int(math.log2(CHUNK))


def _chunk_constants():
    n = GROUP_LANES
    i = np.arange(n)[:, None]
    j = np.arange(n)[None, :]
    same_head = (i // CHUNK) == (j // CHUNK)
    code = np.zeros((n, n), np.int32)
    for lvl in range(CHUNK_LEVELS, 0, -1):
        code = np.where(same_head & (i > j) & ((i >> lvl) == (j >> lvl)), lvl, code)
    planes = [code == lvl for lvl in range(1, CHUNK_LEVELS + 1)] + [code > 0, same_head]
    return np.stack(planes).astype(np.float32)


def _seg_sum(x, ones_bd):
    return _dot(x.astype(BF16), ones_bd)


def _split3(x):
    hi = x.astype(BF16)
    r1 = x - hi.astype(F32)
    mid = r1.astype(BF16)
    return hi, mid, (r1 - mid.astype(F32)).astype(BF16)


def _rwkv_chunk_kernel(r_ref, k_ref, v_ref, a_ref, g_ref, lw_ref,
                       kk_ref, ka_ref, rk_ref, lnw_ref, lnb_ref, planes_ref,
                       o_ref, s_ref):
    @pl.when(pl.program_id(2) == 0)
    def _():
        s_ref[...] = jnp.zeros_like(s_ref)

    n = GROUP_LANES
    lvl_mask = [planes_ref[l] for l in range(CHUNK_LEVELS)]
    low_mask = planes_ref[CHUNK_LEVELS]
    bd = planes_ref[CHUNK_LEVELS + 1]
    tb = r_ref.shape[0]
    nch = tb // CHUNK
    tri_incl = (lax.broadcasted_iota(jnp.int32, (1, CHUNK, n), 1)
                >= lax.broadcasted_iota(jnp.int32, (1, CHUNK, n), 2) % CHUNK)
    ltri = jnp.where(lax.broadcasted_iota(jnp.int32, (nch, CHUNK, CHUNK), 1)
                     >= lax.broadcasted_iota(jnp.int32, (nch, CHUNK, CHUNK), 2), 1.0, 0.0).astype(BF16)
    eye = jnp.where(lax.broadcasted_iota(jnp.int32, (1, n, n), 1)
                    == lax.broadcasted_iota(jnp.int32, (1, n, n), 2), 1.0, 0.0).astype(BF16)

    def chunks(x):
        return x.reshape(nch, CHUNK, n)

    def tile_heads(x):
        return jnp.concatenate([x] * HEADS_PER_GROUP, axis=1)

    def stack(x):
        return tile_heads(x.astype(BF16)) * bd[None]

    r = r_ref[...].astype(F32)
    k = k_ref[...].astype(F32)
    v = v_ref[...].astype(F32)
    a = a_ref[...].astype(F32)
    lw = lw_ref[...]
    kkp = k * kk_ref[...]
    kk = kkp / jnp.maximum(jnp.sqrt(_seg_sum(kkp * kkp, bd)), 1e-12)
    k2 = k * (1.0 + (a - 1.0) * ka_ref[...])
    kb = kk * a
    bonus = _seg_sum(r * k2 * rk_ref[...], bd) * v

    cum = sum(_bdot(ltri, chunks(piece)) for piece in _split3(lw))
    cum_end = cum[:, CHUNK - 1:CHUNK, :]
    e_in = jnp.exp(cum)
    e_ex = jnp.exp(cum - chunks(lw))
    e_neg = jnp.exp(-cum)
    e_dec = jnp.exp(cum_end - cum)
    decay = jnp.exp(cum_end)
    r_t = chunks(r) * e_in
    a_t = -chunks(kk) * e_ex
    kb3 = chunks(kb)
    k23 = chunks(k2)

    ar = jnp.concatenate([a_t, r_t], axis=1).astype(BF16)
    v_s = stack(chunks(v))
    ab_rb = _bdot_nt(ar, stack(kb3 * e_neg))
    ak_rk = _bdot_nt(ar, stack(k23 * e_neg))
    rb_w = jnp.where(tri_incl, ab_rb[:, CHUNK:], 0.0).astype(BF16)
    rk_w = jnp.where(tri_incl, ak_rk[:, CHUNK:], 0.0).astype(BF16)
    amat = tile_heads(ab_rb[:, :CHUNK].astype(BF16))
    akmat = tile_heads(ak_rk[:, :CHUNK].astype(BF16)) * low_mask[None]

    tinv = eye + amat * lvl_mask[0][None]
    for lvl in range(1, CHUNK_LEVELS):
        tinv = tinv + _bdot(tinv, _bdot(amat * lvl_mask[lvl][None], tinv).astype(BF16)).astype(BF16)

    w_s = _bdot(tinv, stack(a_t)).astype(BF16)
    z_s = _bdot(tinv, _bdot(akmat, v_s).astype(BF16)).astype(BF16)
    bh_s = stack(kb3 * e_dec)
    kh_s = stack(k23 * e_dec)
    r_eff = (r_t + _bdot(rb_w, w_s)).astype(BF16)
    y0 = _bdot(rb_w, z_s) + _bdot(rk_w, v_s)

    s = s_ref[...]
    ys = []
    for ci in range(nch):
        p_mat = _dot_tn(w_s[ci], bh_s[ci]).astype(BF16)
        q_mat = _dot_tn(z_s[ci], bh_s[ci]) + _dot_tn(v_s[ci], kh_s[ci])
        s_b = s.astype(BF16)
        ys.append(_dot_nt(r_eff[ci], s_b) + y0[ci])
        s = s * decay[ci] + _dot(s_b, p_mat) + q_mat
    s_ref[...] = s

    y = jnp.concatenate(ys, axis=0)
    inv_n = 1.0 / HEAD_DIM
    mean = _seg_sum(y, bd) * inv_n
    dlt = y - mean
    var = _seg_sum(dlt * dlt, bd) * inv_n
    yn = dlt * lax.rsqrt(var + GN_EPS) * lnw_ref[...] + lnb_ref[...]
    o_ref[...] = ((yn + bonus) * g_ref[...].astype(F32)).astype(o_ref.dtype)


def _rwkv_chunk(r, k, v, a, g, lw, k_k, k_a, r_k, ln_w, ln_b, batch, seq, tb=512):
    t, rw = r.shape
    n = GROUP_LANES
    planes = jnp.asarray(_chunk_constants(), dtype=BF16)
    nblk = seq // tb
    tmap = lambda b, h, c: (b * nblk + c, h)
    pmap = lambda b, h, c: (0, h)
    tspec = pl.BlockSpec((tb, n), tmap)
    pspec = pl.BlockSpec((1, n), pmap)
    return pl.pallas_call(
        _rwkv_chunk_kernel,
        out_shape=jax.ShapeDtypeStruct((t, rw), BF16),
        grid=(batch, rw // n, nblk),
        in_specs=[tspec] * 6 + [pspec] * 5 + [pl.BlockSpec(planes.shape, lambda b, h, c: (0, 0, 0))],
        out_specs=pl.BlockSpec((tb, n), tmap),
        scratch_shapes=[pltpu.VMEM((n, n), F32)],
        compiler_params=_cparams("parallel", "parallel", "arbitrary"),
        name="rwkv_chunk",
    )(r, k, v, a, g, lw, k_k, k_a, r_k, ln_w, ln_b, planes)


def _swa_kernel(sink_ref, *refs, q_per_kv, q_off, k_off, v_off, n_q_tiles):
    q_refs = refs[:n_q_tiles]
    kp_ref, kc_ref, vp_ref, vc_ref, bias_ref, o_ref = refs[n_q_tiles:]
    j = pl.program_id(1)
    nblk = pl.program_id(2)
    blk = WINDOW
    upper = lax.broadcasted_iota(jnp.int32, (1, LANES), 1) >= HEAD_DIM

    def both_halves(x, start_col):
        in_upper = (start_col // HEAD_DIM) % 2 == 1
        sel = jnp.where(upper == in_upper, x, 0.0)
        return sel + pltpu.roll(sel, HEAD_DIM, axis=1)

    kw = both_halves(jnp.concatenate([kp_ref[...], kc_ref[...]], axis=0), k_off + HEAD_DIM * j).astype(BF16)
    vw = both_halves(jnp.concatenate([vp_ref[...], vc_ref[...]], axis=0), v_off + HEAD_DIM * j)
    v_half = (jnp.where(upper, 0.0, vw).astype(BF16), jnp.where(upper, vw, 0.0).astype(BF16))
    qi = lax.broadcasted_iota(jnp.int32, (blk, 2 * blk), 0)
    kj = lax.broadcasted_iota(jnp.int32, (blk, 2 * blk), 1)
    valid = (kj > qi) & (kj <= qi + blk) & ((nblk > 0) | (kj >= blk))
    scale = HEAD_DIM ** -0.5
    for p in range(q_per_kv // 2):
        acc = None
        for half in range(2):
            g = 2 * p + half
            col = q_off % LANES + HEAD_DIM * g
            q_tile = q_refs[col // LANES][...] * scale
            q_upper = (col // HEAD_DIM) % 2 == 1
            qm = jnp.where(upper == q_upper, q_tile, 0.0).astype(BF16)
            sink = sink_ref[j * q_per_kv + g]
            s = _dot_nt(qm, kw) + bias_ref[g]
            s = jnp.where(valid, s, -jnp.inf)
            m = jnp.maximum(jnp.max(s, axis=-1, keepdims=True), sink)
            pr = jnp.exp(s - m)
            denom = jnp.sum(pr, axis=-1, keepdims=True) + jnp.exp(sink - m)
            o = _dot(pr.astype(BF16), v_half[half]) / denom
            acc = o if acc is None else acc + o
        o_ref[:, p * LANES:(p + 1) * LANES] = acc.astype(o_ref.dtype)


def _swa(proj, sinks, bias, batch, seq, q_off, n_kv, q_per_kv):
    t = proj.shape[0]
    blk = WINDOW
    nb = seq // blk
    qw = q_per_kv * HEAD_DIM
    k_off = q_off + n_kv * qw
    v_off = k_off + n_kv * HEAD_DIM
    n_q_tiles = (q_off % LANES + qw + LANES - 1) // LANES
    cur = lambda b, j, n: b * nb + n
    prev = lambda b, j, n: b * nb + jnp.maximum(n - 1, 0)
    tile = lambda rowf, colf: pl.BlockSpec((blk, LANES), lambda b, j, n: (rowf(b, j, n), colf(j)))
    q_specs = [tile(cur, lambda j, i=i: q_off // LANES + (qw // LANES) * j + i) for i in range(n_q_tiles)]
    k_col = lambda j: (k_off + HEAD_DIM * j) // LANES
    v_col = lambda j: (v_off + HEAD_DIM * j) // LANES
    return pl.pallas_call(
        functools.partial(_swa_kernel, q_per_kv=q_per_kv, q_off=q_off, k_off=k_off, v_off=v_off,
                          n_q_tiles=n_q_tiles),
        out_shape=jax.ShapeDtypeStruct((t, n_kv * qw), BF16),
        grid=(batch, n_kv, nb),
        in_specs=[pl.BlockSpec(memory_space=pltpu.SMEM)] + q_specs + [
            tile(prev, k_col), tile(cur, k_col), tile(prev, v_col), tile(cur, v_col),
            pl.BlockSpec((q_per_kv, blk, 2 * blk), lambda b, j, n: (j, 0, 0)),
        ],
        out_specs=pl.BlockSpec((blk, qw), lambda b, j, n: (cur(b, j, n), j)),
        compiler_params=_cparams("parallel", "parallel", "parallel"),
        name="swa",
    )(sinks, *([proj] * (n_q_tiles + 4)), bias)


def _out_proj_kernel(yr_ref, ya_ref, w_ref, o_ref, wb_ref):
    @pl.when(pl.program_id(1) == 0)
    def _():
        wb_ref[...] = w_ref[...].astype(BF16)

    rw = yr_ref.shape[1]
    o_ref[...] = _dot(yr_ref[...], wb_ref[0:rw, :]) + _dot(ya_ref[...], wb_ref[rw:, :])


def _out_proj(y_r, y_a, w_all, layer, tm=1024, tn=512):
    t, rw = y_r.shape
    aw = y_a.shape[1]
    d = w_all.shape[2]
    tm = min(tm, t)
    return pl.pallas_call(
        _out_proj_kernel,
        out_shape=jax.ShapeDtypeStruct((t, d), F32),
        grid=(d // tn, t // tm),
        in_specs=[
            pl.BlockSpec((tm, rw), lambda j, i: (i, 0)),
            pl.BlockSpec((tm, aw), lambda j, i: (i, 0)),
            pl.BlockSpec((None, rw + aw, tn), lambda j, i: (layer, 0, j)),
        ],
        out_specs=pl.BlockSpec((tm, tn), lambda j, i: (i, j)),
        scratch_shapes=[pltpu.VMEM((rw + aw, tn), BF16)],
        compiler_params=_cparams("parallel", "arbitrary"),
        name="out_proj",
    )(y_r, y_a, w_all)


def _rank(vals):
    n = vals.shape[0]
    idx = lax.broadcasted_iota(jnp.int32, vals.shape, 0)
    rank = jnp.zeros(vals.shape, jnp.int32)
    for e in range(n):
        row = vals[e:e + 1, :]
        beats = (row > vals) | ((row == vals) & (e < idx))
        rank = rank + beats.astype(jnp.int32)
    return rank


def _router_kernel(x_ref, y_ref, gain_y_ref, gate_y_ref, gain_ref, sh_ref, sc_ref, rwt_ref, rb_ref,
                   xn_ref, u_ref, idx_ref, pos_ref, gate_ref, cnt_ref, carry_ref):
    @pl.when(pl.program_id(0) == 0)
    def _():
        carry_ref[...] = jnp.zeros_like(carry_ref)

    y = y_ref[...]
    yn = y * lax.rsqrt(jnp.mean(y * y, axis=-1, keepdims=True) + RMS_EPS) * gain_y_ref[...]
    xn = x_ref[...] + gate_y_ref[...] * yn
    xn_ref[...] = xn
    u = _norm_mod(xn, gain_ref[...], sh_ref[...], sc_ref[...])
    u_ref[...] = u
    logits = lax.dot_general(rwt_ref[...], u, (((1,), (1,)), ((), ())),
                             preferred_element_type=F32, precision=HIGHEST)
    scores = _sigmoid(logits)
    sel = scores + rb_ref[:, 0:1]
    ne, tm = sel.shape
    gsz = ne // N_EXPERT_GROUPS
    sub = lax.broadcasted_iota(jnp.int32, (gsz, tm), 0)
    gs_rows = []
    for gi in range(N_EXPERT_GROUPS):
        grp = sel[gi * gsz:(gi + 1) * gsz, :]
        m1 = jnp.max(grp, axis=0, keepdims=True)
        i1 = jnp.min(jnp.where(grp == m1, sub, gsz), axis=0, keepdims=True)
        m2 = jnp.max(jnp.where(sub == i1, -jnp.inf, grp), axis=0, keepdims=True)
        gs_rows.append(m1 + m2)
    gkeep = _rank(jnp.concatenate(gs_rows, axis=0)) < TOPK_GROUPS
    ekeep = jnp.concatenate(
        [jnp.broadcast_to(gkeep[gi:gi + 1, :], (gsz, tm)) for gi in range(N_EXPERT_GROUPS)], axis=0)
    rank = _rank(jnp.where(ekeep, sel, -jnp.inf))
    chosen = rank < TOP_K
    wts = jnp.where(chosen, scores, 0.0)
    gates_t = wts / jnp.sum(wts, axis=0, keepdims=True) * ROUTED_SCALE

    cf = jnp.where(chosen, 1.0, 0.0)
    before = (lax.broadcasted_iota(jnp.int32, (tm, tm), 0) < lax.broadcasted_iota(jnp.int32, (tm, tm), 1))
    carry = carry_ref[:, 0:1]
    pos = carry + _dot(cf.astype(BF16), jnp.where(before, 1.0, 0.0).astype(BF16))
    carry_ref[...] = jnp.broadcast_to(carry + jnp.sum(cf, axis=1, keepdims=True), carry_ref.shape)
    cnt_ref[...] = carry_ref[...]

    eidx = lax.broadcasted_iota(jnp.int32, (ne, tm), 0)
    rows_i, rows_p, rows_g = [], [], []
    for k in range(TOP_K):
        mk = rank == k
        rows_i.append(jnp.sum(jnp.where(mk, eidx, 0), axis=0, keepdims=True))
        rows_p.append(jnp.sum(jnp.where(mk, pos, 0.0), axis=0, keepdims=True))
        rows_g.append(jnp.sum(jnp.where(mk, gates_t, 0.0), axis=0, keepdims=True))
    idx_ref[...] = jnp.concatenate(rows_i, axis=0)
    pos_ref[...] = jnp.concatenate(rows_p, axis=0).astype(jnp.int32)
    gate_ref[...] = jnp.concatenate(rows_g, axis=0)


def _router(x2, y, gain_y, gate_y, gain, shift, scale, rw_t, rb, seq, tm=256):
    t, d = x2.shape
    ne = rw_t.shape[0]
    bmap = lambda i: ((i * tm) // seq, 0, 0)
    pick = lambda dt: jax.ShapeDtypeStruct((TOP_K, t), dt)
    pspec = pl.BlockSpec((TOP_K, tm), lambda i: (0, i))
    row = pl.BlockSpec((tm, d), lambda i: (i, 0))
    vec = pl.BlockSpec((1, d), lambda i: (0, 0))
    per_batch = pl.BlockSpec((None, 1, d), bmap)
    return pl.pallas_call(
        _router_kernel,
        out_shape=[jax.ShapeDtypeStruct((t, d), F32), jax.ShapeDtypeStruct((t, d), F32),
                   pick(jnp.int32), pick(jnp.int32), pick(F32), jax.ShapeDtypeStruct((ne, LANES), F32)],
        grid=(t // tm,),
        in_specs=[row, row, vec, per_batch, vec, per_batch, per_batch,
                  pl.BlockSpec((ne, d), lambda i: (0, 0)), pl.BlockSpec((ne, 1), lambda i: (0, 0))],
        out_specs=[row, row, pspec, pspec, pspec, pl.BlockSpec((ne, LANES), lambda i: (0, 0))],
        scratch_shapes=[pltpu.VMEM((ne, LANES), F32)],
        compiler_params=_cparams("arbitrary"),
        name="router",
    )(x2, y, gain_y, gate_y, gain, shift, scale, rw_t, rb)


def _row_copy(src, src_row, dst, dst_row, sem):
    return pltpu.make_async_copy(src.at[pl.ds(src_row, 1), :], dst.at[pl.ds(dst_row, 1), :], sem)


def _dispatch_kernel(rows_ref, u_ref, x_hbm, sem, *, n_pick):
    td = u_ref.shape[0]

    def body(tok, carry):
        for k in range(n_pick):
            _row_copy(u_ref, tok, x_hbm, rows_ref[0, k * td + tok], sem).start()
        return carry

    lax.fori_loop(0, td, body, 0)
    for k in range(n_pick):
        pltpu.make_async_copy(u_ref, x_hbm.at[pl.ds(0, td), :], sem).wait()


def _dispatch(u, rows_tiled, n_rows, td):
    t, d = u.shape
    n_pick = rows_tiled.shape[2] // td
    return pl.pallas_call(
        functools.partial(_dispatch_kernel, n_pick=n_pick),
        out_shape=jax.ShapeDtypeStruct((n_rows, d), F32),
        grid=(t // td,),
        in_specs=[
            pl.BlockSpec((None, 1, n_pick * td), lambda i: (i, 0, 0), memory_space=pltpu.SMEM),
            pl.BlockSpec((td, d), lambda i: (i, 0)),
        ],
        out_specs=pl.BlockSpec(memory_space=pl.ANY),
        scratch_shapes=[pltpu.SemaphoreType.DMA(())],
        compiler_params=_cparams("arbitrary"),
        name="moe_dispatch",
    )(rows_tiled, u)


def _experts_kernel(tile_ref, exp_ref, lo_ref, hi_ref, first_ref, n_ref, x_ref, wgu_ref, wd_ref, y_ref):
    p = pl.program_id(0)

    @pl.when(p < n_ref[0])
    def _():
        ff = wd_ref.shape[0]
        h = _dot(x_ref[...].astype(BF16), wgu_ref[...].astype(BF16))
        hg = h[:, :ff]
        hu = h[:, ff:]
        act = (hg * _sigmoid(hg) * hu).astype(BF16)
        row = lax.broadcasted_iota(jnp.int32, (x_ref.shape[0], 1), 0)
        mine = (row >= lo_ref[p]) & (row < hi_ref[p])

        @pl.when(first_ref[p] == 1)
        def _():
            y_ref[...] = jnp.where(mine, _dot(act, wd_ref[...].astype(BF16)), 0.0)

        @pl.when(first_ref[p] == 0)
        def _():
            y_ref[...] += jnp.where(mine, _dot(act, wd_ref[...].astype(BF16)), 0.0)


def _experts(x_sorted, work, w_gu, w_down, layer, tm):
    n_rows, d = x_sorted.shape
    _, ne1, _, ff2 = w_gu.shape
    ff = ff2 // 2
    n_work_max = n_rows // tm + ne1 - 1
    return pl.pallas_call(
        _experts_kernel,
        out_shape=jax.ShapeDtypeStruct((n_rows, d), F32),
        grid_spec=pltpu.PrefetchScalarGridSpec(
            num_scalar_prefetch=6,
            grid=(n_work_max,),
            in_specs=[
                pl.BlockSpec((tm, d), lambda p, tile, exp, *_: (tile[p], 0)),
                pl.BlockSpec((None, None, d, ff2), lambda p, tile, exp, *_: (layer, exp[p], 0, 0)),
                pl.BlockSpec((None, None, ff, d), lambda p, tile, exp, *_: (layer, exp[p], 0, 0)),
            ],
            out_specs=pl.BlockSpec((tm, d), lambda p, tile, exp, *_: (tile[p], 0)),
        ),
        compiler_params=_cparams("arbitrary"),
        name="moe_experts",
    )(*work, x_sorted, w_gu, w_down)


def _expert_work_list(counts, n_rows, tm):
    ne1 = counts.shape[0]
    n_work_max = n_rows // tm + ne1 - 1
    ends = jnp.cumsum(counts)
    starts = ends - counts
    first_tile = starts // tm
    n_tile = jnp.where(counts > 0, (ends - 1) // tm - first_tile + 1, 0)
    work_end = jnp.cumsum(n_tile)
    n_work = work_end[-1:]
    p = jnp.minimum(jnp.arange(n_work_max, dtype=jnp.int32), n_work - 1)
    exp = jnp.sum((work_end[None, :] <= p[:, None]).astype(jnp.int32), axis=1)
    tile = first_tile[exp] + p - (work_end - n_tile)[exp]
    lo = jnp.maximum(starts[exp] - tile * tm, 0)
    hi = jnp.minimum(ends[exp] - tile * tm, tm)
    first = jnp.concatenate([jnp.ones((1,), jnp.int32), (tile[1:] != tile[:-1]).astype(jnp.int32)])
    as_i32 = lambda a: a.astype(jnp.int32)
    return tuple(map(as_i32, (tile, exp, lo, hi, first, n_work))), starts


def _combine_kernel(rows_ref, nrows_ref, g_ref, x_ref, ys_ref, gain_ref, gate_ref, y_hbm, o_ref, buf, sems,
                    *, n_pick):
    i = pl.program_id(0)
    tt = x_ref.shape[0]
    slot = i % 2

    def gather(table_ref, dst_slot):
        def body(tok, carry):
            for k in range(n_pick):
                _row_copy(y_hbm, table_ref[0, k * tt + tok], buf.at[dst_slot, k], tok, sems.at[dst_slot]).start()
            return carry

        lax.fori_loop(0, tt, body, 0)

    @pl.when(i == 0)
    def _():
        gather(rows_ref, 0)

    @pl.when(i + 1 < pl.num_programs(0))
    def _():
        gather(nrows_ref, 1 - slot)

    for k in range(n_pick):
        pltpu.make_async_copy(y_hbm.at[pl.ds(0, tt), :], buf.at[slot, k], sems.at[slot]).wait()
    g = g_ref[...]
    acc = ys_ref[...]
    for k in range(n_pick):
        acc = acc + buf[slot, k] * g[:, k:k + 1]
    yn = acc * lax.rsqrt(jnp.mean(acc * acc, axis=-1, keepdims=True) + RMS_EPS) * gain_ref[...]
    o_ref[...] = x_ref[...] + gate_ref[...] * yn


def _combine(y_routed, y_shared, rows_tiled, gates, x2, gain, gate, seq, tt):
    t, d = x2.shape
    n_pick = rows_tiled.shape[2] // tt
    n_tiles = t // tt
    bmap = lambda i: ((i * tt) // seq, 0, 0)
    row = pl.BlockSpec((tt, d), lambda i: (i, 0))
    table = lambda imap: pl.BlockSpec((None, 1, n_pick * tt), imap, memory_space=pltpu.SMEM)
    return pl.pallas_call(
        functools.partial(_combine_kernel, n_pick=n_pick),
        out_shape=jax.ShapeDtypeStruct((t, d), F32),
        grid=(n_tiles,),
        in_specs=[
            table(lambda i: (i, 0, 0)),
            table(lambda i: (jnp.minimum(i + 1, n_tiles - 1), 0, 0)),
            pl.BlockSpec((tt, gates.shape[1]), lambda i: (i, 0)),
            row,
            row,
            pl.BlockSpec((1, d), lambda i: (0, 0)),
            pl.BlockSpec((None, 1, d), bmap),
            pl.BlockSpec(memory_space=pl.ANY),
        ],
        out_specs=row,
        scratch_shapes=[pltpu.VMEM((2, n_pick, tt, d), F32), pltpu.SemaphoreType.DMA((2,))],
        compiler_params=_cparams("arbitrary"),
        name="moe_combine",
    )(rows_tiled, rows_tiled, gates, x2, y_shared, gain, gate, y_routed)


def _tile_rows(rows, tile):
    n_pick, t = rows.shape
    return rows.reshape(n_pick, t // tile, tile).transpose(1, 0, 2).reshape(t // tile, 1, n_pick * tile)


def _moe_block(x2, y_mix, gain_mix, gate_mix, gain_in, shift, scale, rw_t, rb, w_gu, w_down, ws_gu, ws_down, layer,
               gain_out, gate, seq, tm=256, td=128, tt=64):
    t, d = x2.shape
    ne = rw_t.shape[0]
    x2, u, idx, pos, gate_k, counts = _router(x2, y_mix, gain_mix, gate_mix, gain_in, shift, scale, rw_t, rb, seq)
    work, starts = _expert_work_list(counts[:, 0].astype(jnp.int32), TOP_K * t, tm)
    hit = idx[:, :, None] == jnp.arange(ne, dtype=jnp.int32)[None, None, :]
    rows = jnp.sum(jnp.where(hit, starts[None, None, :], 0), axis=-1) + pos
    x_sorted = _dispatch(u, _tile_rows(rows, td), TOP_K * t, td)
    y_routed = _experts(x_sorted, work, w_gu, w_down, layer, tm)
    shared_work, _ = _expert_work_list(jnp.full((1,), t, jnp.int32), t, tm)
    y_shared = _experts(u, shared_work, ws_gu, ws_down, layer, tm)
    return _combine(y_routed, y_shared, _tile_rows(rows, tt), gate_k.T, x2, gain_out, gate, seq, tt)


def kernel(x, c, rel_bias, ada_w, ada_b, norm_gains, w_in, shift_mu, rwkv_w0, rwkv_w2, rwkv_a0, rwkv_a2, rwkv_g2,
           rwkv_k_k, rwkv_k_a, rwkv_r_k, rwkv_ln_w, rwkv_ln_b, attn_sinks, w_out, router_w, router_bias,
           expert_w_gu, expert_w_down, shared_w_gu, shared_w_down):
    batch, seq, d = x.shape
    depth = ada_w.shape[0]
    t = batch * seq
    rw = rwkv_w0.shape[1]
    dl, al, gl = rwkv_w2.shape[1], rwkv_a2.shape[1], rwkv_g2.shape[1]
    aw = w_out.shape[1] - rw
    n_q = aw // HEAD_DIM
    n_kv = (w_in.shape[2] - (3 * rw + dl + al + gl) - aw) // (2 * HEAD_DIM)
    q_per_kv = n_q // n_kv
    q_off = 3 * rw + dl + al + gl
    assert q_off % HEAD_DIM == 0, "attention heads must start at a multiple of the head width"

    c_pad = jnp.pad(c, ((0, 8 - batch % 8 if batch % 8 else 0), (0, 0)))
    ada = _ada(c_pad, ada_w, ada_b)[:, :batch]
    bias = _bias_table(rel_bias)
    w_in_t = jnp.swapaxes(w_in, 1, 2)
    w_gu_b = expert_w_gu
    ws_gu_b = shared_w_gu[:, None]

    x2 = x.reshape(t, d)
    for l in range(depth):
        sh_m, sc_m, gt_m, sh_f, sc_f, gt_f = [a.reshape(batch, 1, d) for a in jnp.split(ada[l], 6, axis=-1)]
        gains = norm_gains[l].reshape(4, 1, d)

        proj = _in_proj(_norm_mod_bf16(x2, gains[0], sh_m, sc_m, seq), w_in_t, l)

        r, k, v, a, g, lw = _rwkv_prep(proj, shift_mu[l][None, :], rwkv_w0[l][None, :], rwkv_w2[l],
                                       rwkv_a0[l][None, :], rwkv_a2[l], rwkv_g2[l], seq)
        y_r = _rwkv_chunk(r, k, v, a, g, lw, rwkv_k_k[l][None, :], rwkv_k_a[l][None, :],
                          rwkv_r_k[l].reshape(1, rw), rwkv_ln_w[l][None, :], rwkv_ln_b[l][None, :], batch, seq)
        y_a = _swa(proj, attn_sinks[l], bias, batch, seq, q_off, n_kv, q_per_kv)
        y_mix = _out_proj(y_r, y_a, w_out, l)
        x2 = _moe_block(x2, y_mix, gains[1], gt_m, gains[2], sh_f, sc_f, router_w[l].T, router_bias[l][:, None],
                        w_gu_b, expert_w_down, ws_gu_b, shared_w_down[:, None], l, gains[3], gt_f, seq)
    return x2.reshape(batch, seq, d)
```

```python
import functools
import math

import numpy as np
import jax
import jax.numpy as jnp
from jax import lax
from jax.experimental import pallas as pl
from jax.experimental.pallas import tpu as pltpu

F32 = jnp.float32
BF16 = jnp.bfloat16

HEAD_DIM = 64
LANES = 128
WINDOW = 128
N_BUCKETS = 32
MAX_DISTANCE = 128
N_EXPERT_GROUPS = 8
TOPK_GROUPS = 4
TOP_K = 8
ROUTED_SCALE = 2.5
RMS_EPS = 1e-6
GN_EPS = 64e-5
CHUNK = 64
HEADS_PER_GROUP = 4
GROUP_LANES = HEADS_PER_GROUP * HEAD_DIM
VMEM_LIMIT = 60 * 1024 * 1024

HIGHEST = lax.Precision.HIGHEST


def _round_up(n, m):
    return (n + m - 1) // m * m


def _cparams(*sem):
    return pltpu.CompilerParams(dimension_semantics=sem, vmem_limit_bytes=VMEM_LIMIT)


def _dot(a, b):
    return jnp.dot(a, b, preferred_element_type=F32)


def _dot_nt(a, b):
    return lax.dot_general(a, b, (((1,), (1,)), ((), ())), preferred_element_type=F32)


def _dot_tn(a, b):
    return lax.dot_general(a, b, (((0,), (0,)), ((), ())), preferred_element_type=F32)


def _bdot(a, b):
    return lax.dot_general(a, b, (((2,), (1,)), ((0,), (0,))), preferred_element_type=F32)


def _bdot_nt(a, b):
    return lax.dot_general(a, b, (((2,), (2,)), ((0,), (0,))), preferred_element_type=F32)


def _sigmoid(x):
    return 1.0 / (1.0 + jnp.exp(-x))


def _pack_halves(x):
    half = x.shape[1] // 2
    lo = lax.bitcast_convert_type(x[:, :half].astype(BF16).astype(F32), jnp.uint32)
    hi = lax.bitcast_convert_type(x[:, half:].astype(BF16).astype(F32), jnp.uint32)
    return (lo >> 16) | (hi & jnp.uint32(0xFFFF0000))


def _unpack_halves(p):
    lo = lax.bitcast_convert_type(p << 16, F32)
    hi = lax.bitcast_convert_type(p & jnp.uint32(0xFFFF0000), F32)
    return lo, hi


def _norm_mod(x, gain, shift, scale):
    y = x * lax.rsqrt(jnp.mean(x * x, axis=-1, keepdims=True) + RMS_EPS) * gain
    return y * (1.0 + scale) + shift


def _ada_kernel(c_ref, w_ref, b_ref, o_ref):
    c = c_ref[...]
    cond = c * _sigmoid(c)
    o_ref[...] = _dot(cond.astype(BF16), w_ref[...].astype(BF16)) + b_ref[...]


def _ada(c_pad, ada_w, ada_b):
    nl, d, n = ada_w.shape
    tn = 512
    return pl.pallas_call(
        _ada_kernel,
        out_shape=jax.ShapeDtypeStruct((nl, c_pad.shape[0], n), F32),
        grid=(nl, n // tn),
        in_specs=[
            pl.BlockSpec((c_pad.shape[0], d), lambda l, j: (0, 0)),
            pl.BlockSpec((None, d, tn), lambda l, j: (l, 0, j)),
            pl.BlockSpec((None, 1, tn), lambda l, j: (l, 0, j)),
        ],
        out_specs=pl.BlockSpec((None, c_pad.shape[0], tn), lambda l, j: (l, 0, j)),
        compiler_params=_cparams("parallel", "parallel"),
        name="ada",
    )(c_pad, ada_w, ada_b.reshape(nl, 1, n))


def _bias_kernel(rbt_ref, oh_ref, o_ref):
    o_ref[...] = jnp.dot(rbt_ref[...], oh_ref[...], preferred_element_type=F32, precision=HIGHEST)


def _bucket_onehot():
    qi = np.arange(WINDOW)[:, None]
    kj = np.arange(2 * WINDOW)[None, :]
    dist = qi + WINDOW - kj
    max_exact = N_BUCKETS // 2
    d = np.maximum(dist, 0)
    far = max_exact + (np.log(np.maximum(d, max_exact).astype(np.float32) / np.float32(max_exact))
                       / np.float32(math.log(MAX_DISTANCE / max_exact))
                       * np.float32(N_BUCKETS - max_exact)).astype(np.int32)
    bucket = np.where(d < max_exact, d, np.minimum(far, N_BUCKETS - 1)).reshape(-1)
    return (np.arange(N_BUCKETS)[:, None] == bucket[None, :]).astype(np.float32)


def _bias_table(rel_bias):
    nb, nh = rel_bias.shape
    onehot = jnp.asarray(_bucket_onehot())
    out = pl.pallas_call(
        _bias_kernel,
        out_shape=jax.ShapeDtypeStruct((nh, WINDOW * 2 * WINDOW), F32),
        name="bias_table",
        compiler_params=pltpu.CompilerParams(vmem_limit_bytes=VMEM_LIMIT),
    )(rel_bias.T, onehot)
    return out.reshape(nh, WINDOW, 2 * WINDOW)


def _norm_mod_kernel(x_ref, gain_ref, sh_ref, sc_ref, u_ref):
    u_ref[...] = _norm_mod(x_ref[...], gain_ref[...], sh_ref[...], sc_ref[...]).astype(u_ref.dtype)


def _norm_mod_bf16(x2, gain, shift, scale, seq, tm=512):
    t, d = x2.shape
    bmap = lambda i: ((i * tm) // seq, 0, 0)
    row = pl.BlockSpec((tm, d), lambda i: (i, 0))
    return pl.pallas_call(
        _norm_mod_kernel,
        out_shape=jax.ShapeDtypeStruct((t, d), BF16),
        grid=(t // tm,),
        in_specs=[row, pl.BlockSpec((1, d), lambda i: (0, 0)), pl.BlockSpec((None, 1, d), bmap),
                  pl.BlockSpec((None, 1, d), bmap)],
        out_specs=row,
        compiler_params=_cparams("parallel"),
        name="norm_mod",
    )(x2, gain, shift, scale)


def _in_proj_kernel(u_ref, wt_ref, o_ref, wb_ref):
    @pl.when(pl.program_id(1) == 0)
    def _():
        wb_ref[...] = wt_ref[...].astype(BF16)

    o_ref[...] = _dot_nt(u_ref[...], wb_ref[...])


def _in_proj(u, wt_all, layer, tm=1024, tn=512):
    t, d = u.shape
    n = wt_all.shape[1]
    tm = min(tm, t)
    return pl.pallas_call(
        _in_proj_kernel,
        out_shape=jax.ShapeDtypeStruct((t, n), F32),
        grid=(pl.cdiv(n, tn), t // tm),
        in_specs=[
            pl.BlockSpec((tm, d), lambda j, i: (i, 0)),
            pl.BlockSpec((None, tn, d), lambda j, i: (layer, j, 0)),
        ],
        out_specs=pl.BlockSpec((tm, tn), lambda j, i: (i, j)),
        scratch_shapes=[pltpu.VMEM((tn, d), BF16)],
        compiler_params=_cparams("parallel", "arbitrary"),
        name="in_proj",
    )(u, wt_all)


def _lane_window(start, width):
    lo = start // LANES * LANES
    return lo, _round_up(start + width, LANES), start - lo


def _window_rows(w, start):
    lo, hi, off = _lane_window(start, w.shape[0])
    return jnp.pad(w, ((off, hi - lo - off - w.shape[0]), (0, 0))).astype(BF16)


def _rwkv_prep_kernel(rw, xw_win, xa_win, xg_win, p_ref, prev_ref, mu_ref, w0_ref, w2_ref, a0_ref, a2_ref, g2_ref,
                      r_ref, k_ref, v_ref, a_ref, g_ref, lw_ref, *, blocks_per_seq):
    i = pl.program_id(0)
    cur = p_ref[...]
    first = (i % blocks_per_seq) == 0
    prev_row = jnp.where(first, 0.0, prev_ref[7:8, :])
    rolled = pltpu.roll(cur, 1, axis=0)
    row = lax.broadcasted_iota(jnp.int32, cur.shape, 0)
    shifted = jnp.where(row == 0, prev_row, rolled)
    x = cur + (shifted - cur) * mu_ref[...]
    r_ref[...] = x[:, 0:rw].astype(r_ref.dtype)
    k_ref[...] = x[:, rw:2 * rw].astype(k_ref.dtype)
    v_ref[...] = x[:, 2 * rw:3 * rw].astype(v_ref.dtype)
    xw = x[:, xw_win[0]:xw_win[1]]
    xa = x[:, xa_win[0]:xa_win[1]]
    xg = x[:, xg_win[0]:xg_win[1]]
    z = w0_ref[...] + _dot(jnp.tanh(xw).astype(BF16), w2_ref[...])
    lw_ref[...] = -math.exp(-0.5) * _sigmoid(z)
    a_ref[...] = _sigmoid(a0_ref[...] + _dot(xa.astype(BF16), a2_ref[...])).astype(a_ref.dtype)
    g_ref[...] = _dot(_sigmoid(xg).astype(BF16), g2_ref[...]).astype(g_ref.dtype)


def _rwkv_prep(proj, mu, w0, w2, a0, a2, g2, seq, ts=256):
    t = proj.shape[0]
    rw = w0.shape[1]
    dl, al, gl = w2.shape[0], a2.shape[0], g2.shape[0]
    starts = (3 * rw, 3 * rw + dl, 3 * rw + dl + al)
    wins = [_lane_window(s, w)[:2] for s, w in zip(starts, (dl, al, gl))]
    wblock = wins[2][1]
    w2p, a2p, g2p = (_window_rows(w, s) for w, s in zip((w2, a2, g2), starts))
    mu_p = jnp.pad(mu, ((0, 0), (0, wblock - mu.shape[1])))
    full = lambda i: (0, 0)
    outs = [jax.ShapeDtypeStruct((t, rw), BF16)] * 5 + [jax.ShapeDtypeStruct((t, rw), F32)]
    ospec = pl.BlockSpec((ts, rw), lambda i: (i, 0))
    return pl.pallas_call(
        functools.partial(_rwkv_prep_kernel, rw, *wins, blocks_per_seq=seq // ts),
        out_shape=outs,
        grid=(t // ts,),
        in_specs=[
            pl.BlockSpec((ts, wblock), lambda i: (i, 0)),
            pl.BlockSpec((8, wblock), lambda i: (jnp.maximum(i * (ts // 8) - 1, 0), 0)),
            pl.BlockSpec((1, wblock), full),
            pl.BlockSpec((1, rw), full),
            pl.BlockSpec(w2p.shape, full),
            pl.BlockSpec((1, rw), full),
            pl.BlockSpec(a2p.shape, full),
            pl.BlockSpec(g2p.shape, full),
        ],
        out_specs=[ospec] * 6,
        compiler_params=_cparams("parallel"),
        name="rwkv_prep",
    )(proj, proj, mu_p, w0, w2p, a0, a2p, g2p)


CHUNK_LEVELS = int(math.log2(CHUNK))


def _chunk_constants():
    n = GROUP_LANES
    i = np.arange(n)[:, None]
    j = np.arange(n)[None, :]
    same_head = (i // CHUNK) == (j // CHUNK)
    code = np.zeros((n, n), np.int32)
    for lvl in range(CHUNK_LEVELS, 0, -1):
        code = np.where(same_head & (i > j) & ((i >> lvl) == (j >> lvl)), lvl, code)
    planes = [code == lvl for lvl in range(1, CHUNK_LEVELS + 1)] + [code > 0, same_head]
    return np.stack(planes).astype(np.float32)


def _seg_sum(x, ones_bd):
    return _dot(x.astype(BF16), ones_bd)


def _split3(x):
    hi = x.astype(BF16)
    r1 = x - hi.astype(F32)
    mid = r1.astype(BF16)
    return hi, mid, (r1 - mid.astype(F32)).astype(BF16)


def _rwkv_chunk_kernel(r_ref, k_ref, v_ref, a_ref, g_ref, lw_ref,
                       kk_ref, ka_ref, rk_ref, lnw_ref, lnb_ref, planes_ref,
                       o_ref, s_ref):
    @pl.when(pl.program_id(2) == 0)
    def _():
        s_ref[...] = jnp.zeros_like(s_ref)

    n = GROUP_LANES
    lvl_mask = [planes_ref[l] for l in range(CHUNK_LEVELS)]
    low_mask = planes_ref[CHUNK_LEVELS]
    bd = planes_ref[CHUNK_LEVELS + 1]
    tb = r_ref.shape[0]
    nch = tb // CHUNK
    tri_incl = (lax.broadcasted_iota(jnp.int32, (1, CHUNK, n), 1)
                >= lax.broadcasted_iota(jnp.int32, (1, CHUNK, n), 2) % CHUNK)
    ltri = jnp.where(lax.broadcasted_iota(jnp.int32, (nch, CHUNK, CHUNK), 1)
                     >= lax.broadcasted_iota(jnp.int32, (nch, CHUNK, CHUNK), 2), 1.0, 0.0).astype(BF16)
    eye = jnp.where(lax.broadcasted_iota(jnp.int32, (1, n, n), 1)
                    == lax.broadcasted_iota(jnp.int32, (1, n, n), 2), 1.0, 0.0).astype(BF16)

    def chunks(x):
        return x.reshape(nch, CHUNK, n)

    def tile_heads(x):
        return jnp.concatenate([x] * HEADS_PER_GROUP, axis=1)

    def stack(x):
        return tile_heads(x.astype(BF16)) * bd[None]

    r = r_ref[...].astype(F32)
    k = k_ref[...].astype(F32)
    v = v_ref[...].astype(F32)
    a = a_ref[...].astype(F32)
    lw = lw_ref[...]
    kkp = k * kk_ref[...]
    kk = kkp / jnp.maximum(jnp.sqrt(_seg_sum(kkp * kkp, bd)), 1e-12)
    k2 = k * (1.0 + (a - 1.0) * ka_ref[...])
    kb = kk * a
    bonus = _seg_sum(r * k2 * rk_ref[...], bd) * v

    cum = sum(_bdot(ltri, chunks(piece)) for piece in _split3(lw))
    cum_end = cum[:, CHUNK - 1:CHUNK, :]
    e_in = jnp.exp(cum)
    e_ex = jnp.exp(cum - chunks(lw))
    e_neg = jnp.exp(-cum)
    e_dec = jnp.exp(cum_end - cum)
    decay = jnp.exp(cum_end)
    r_t = chunks(r) * e_in
    a_t = -chunks(kk) * e_ex
    kb3 = chunks(kb)
    k23 = chunks(k2)

    ar = jnp.concatenate([a_t, r_t], axis=1).astype(BF16)
    v_s = stack(chunks(v))
    ab_rb = _bdot_nt(ar, stack(kb3 * e_neg))
    ak_rk = _bdot_nt(ar, stack(k23 * e_neg))
    rb_w = jnp.where(tri_incl, ab_rb[:, CHUNK:], 0.0).astype(BF16)
    rk_w = jnp.where(tri_incl, ak_rk[:, CHUNK:], 0.0).astype(BF16)
    amat = tile_heads(ab_rb[:, :CHUNK].astype(BF16))
    akmat = tile_heads(ak_rk[:, :CHUNK].astype(BF16)) * low_mask[None]

    tinv = eye + amat * lvl_mask[0][None]
    for lvl in range(1, CHUNK_LEVELS):
        tinv = tinv + _bdot(tinv, _bdot(amat * lvl_mask[lvl][None], tinv).astype(BF16)).astype(BF16)

    w_s = _bdot(tinv, stack(a_t)).astype(BF16)
    z_s = _bdot(tinv, _bdot(akmat, v_s).astype(BF16)).astype(BF16)
    bh_s = stack(kb3 * e_dec)
    kh_s = stack(k23 * e_dec)
    r_eff = (r_t + _bdot(rb_w, w_s)).astype(BF16)
    y0 = _bdot(rb_w, z_s) + _bdot(rk_w, v_s)

    s = s_ref[...]
    ys = []
    for ci in range(nch):
        p_mat = _dot_tn(w_s[ci], bh_s[ci]).astype(BF16)
        q_mat = _dot_tn(z_s[ci], bh_s[ci]) + _dot_tn(v_s[ci], kh_s[ci])
        s_b = s.astype(BF16)
        ys.append(_dot_nt(r_eff[ci], s_b) + y0[ci])
        s = s * decay[ci] + _dot(s_b, p_mat) + q_mat
    s_ref[...] = s

    y = jnp.concatenate(ys, axis=0)
    inv_n = 1.0 / HEAD_DIM
    mean = _seg_sum(y, bd) * inv_n
    dlt = y - mean
    var = _seg_sum(dlt * dlt, bd) * inv_n
    yn = dlt * lax.rsqrt(var + GN_EPS) * lnw_ref[...] + lnb_ref[...]
    o_ref[...] = ((yn + bonus) * g_ref[...].astype(F32)).astype(o_ref.dtype)


def _rwkv_chunk(r, k, v, a, g, lw, k_k, k_a, r_k, ln_w, ln_b, batch, seq, tb=512):
    t, rw = r.shape
    n = GROUP_LANES
    planes = jnp.asarray(_chunk_constants(), dtype=BF16)
    nblk = seq // tb
    tmap = lambda b, h, c: (b * nblk + c, h)
    pmap = lambda b, h, c: (0, h)
    tspec = pl.BlockSpec((tb, n), tmap)
    pspec = pl.BlockSpec((1, n), pmap)
    return pl.pallas_call(
        _rwkv_chunk_kernel,
        out_shape=jax.ShapeDtypeStruct((t, rw), BF16),
        grid=(batch, rw // n, nblk),
        in_specs=[tspec] * 6 + [pspec] * 5 + [pl.BlockSpec(planes.shape, lambda b, h, c: (0, 0, 0))],
        out_specs=pl.BlockSpec((tb, n), tmap),
        scratch_shapes=[pltpu.VMEM((n, n), F32)],
        compiler_params=_cparams("parallel", "parallel", "arbitrary"),
        name="rwkv_chunk",
    )(r, k, v, a, g, lw, k_k, k_a, r_k, ln_w, ln_b, planes)


def _swa_kernel(sink_ref, *refs, q_per_kv, q_off, k_off, v_off, n_q_tiles):
    q_refs = refs[:n_q_tiles]
    kp_ref, kc_ref, vp_ref, vc_ref, bias_ref, o_ref = refs[n_q_tiles:]
    j = pl.program_id(1)
    nblk = pl.program_id(2)
    blk = WINDOW
    upper = lax.broadcasted_iota(jnp.int32, (1, LANES), 1) >= HEAD_DIM

    def both_halves(x, start_col):
        in_upper = (start_col // HEAD_DIM) % 2 == 1
        sel = jnp.where(upper == in_upper, x, 0.0)
        return sel + pltpu.roll(sel, HEAD_DIM, axis=1)

    kw = both_halves(jnp.concatenate([kp_ref[...], kc_ref[...]], axis=0), k_off + HEAD_DIM * j).astype(BF16)
    vw = both_halves(jnp.concatenate([vp_ref[...], vc_ref[...]], axis=0), v_off + HEAD_DIM * j)
    v_half = (jnp.where(upper, 0.0, vw).astype(BF16), jnp.where(upper, vw, 0.0).astype(BF16))
    qi = lax.broadcasted_iota(jnp.int32, (blk, 2 * blk), 0)
    kj = lax.broadcasted_iota(jnp.int32, (blk, 2 * blk), 1)
    valid = (kj > qi) & (kj <= qi + blk) & ((nblk > 0) | (kj >= blk))
    scale = HEAD_DIM ** -0.5
    for p in range(q_per_kv // 2):
        acc = None
        for half in range(2):
            g = 2 * p + half
            col = q_off % LANES + HEAD_DIM * g
            q_tile = q_refs[col // LANES][...] * scale
            q_upper = (col // HEAD_DIM) % 2 == 1
            qm = jnp.where(upper == q_upper, q_tile, 0.0).astype(BF16)
            sink = sink_ref[j * q_per_kv + g]
            s = _dot_nt(qm, kw) + bias_ref[g]
            s = jnp.where(valid, s, -jnp.inf)
            m = jnp.maximum(jnp.max(s, axis=-1, keepdims=True), sink)
            pr = jnp.exp(s - m)
            denom = jnp.sum(pr, axis=-1, keepdims=True) + jnp.exp(sink - m)
            o = _dot(pr.astype(BF16), v_half[half]) / denom
            acc = o if acc is None else acc + o
        o_ref[:, p * LANES:(p + 1) * LANES] = acc.astype(o_ref.dtype)


def _swa(proj, sinks, bias, batch, seq, q_off, n_kv, q_per_kv):
    t = proj.shape[0]
    blk = WINDOW
    nb = seq // blk
    qw = q_per_kv * HEAD_DIM
    k_off = q_off + n_kv * qw
    v_off = k_off + n_kv * HEAD_DIM
    n_q_tiles = (q_off % LANES + qw + LANES - 1) // LANES
    cur = lambda b, j, n: b * nb + n
    prev = lambda b, j, n: b * nb + jnp.maximum(n - 1, 0)
    tile = lambda rowf, colf: pl.BlockSpec((blk, LANES), lambda b, j, n: (rowf(b, j, n), colf(j)))
    q_specs = [tile(cur, lambda j, i=i: q_off // LANES + (qw // LANES) * j + i) for i in range(n_q_tiles)]
    k_col = lambda j: (k_off + HEAD_DIM * j) // LANES
    v_col = lambda j: (v_off + HEAD_DIM * j) // LANES
    return pl.pallas_call(
        functools.partial(_swa_kernel, q_per_kv=q_per_kv, q_off=q_off, k_off=k_off, v_off=v_off,
                          n_q_tiles=n_q_tiles),
        out_shape=jax.ShapeDtypeStruct((t, n_kv * qw), BF16),
        grid=(batch, n_kv, nb),
        in_specs=[pl.BlockSpec(memory_space=pltpu.SMEM)] + q_specs + [
            tile(prev, k_col), tile(cur, k_col), tile(prev, v_col), tile(cur, v_col),
            pl.BlockSpec((q_per_kv, blk, 2 * blk), lambda b, j, n: (j, 0, 0)),
        ],
        out_specs=pl.BlockSpec((blk, qw), lambda b, j, n: (cur(b, j, n), j)),
        compiler_params=_cparams("parallel", "parallel", "parallel"),
        name="swa",
    )(sinks, *([proj] * (n_q_tiles + 4)), bias)


def _out_proj_kernel(yr_ref, ya_ref, w_ref, o_ref, wb_ref):
    @pl.when(pl.program_id(1) == 0)
    def _():
        wb_ref[...] = w_ref[...].astype(BF16)

    rw = yr_ref.shape[1]
    o_ref[...] = _dot(yr_ref[...], wb_ref[0:rw, :]) + _dot(ya_ref[...], wb_ref[rw:, :])


def _out_proj(y_r, y_a, w_all, layer, tm=1024, tn=512):
    t, rw = y_r.shape
    aw = y_a.shape[1]
    d = w_all.shape[2]
    tm = min(tm, t)
    return pl.pallas_call(
        _out_proj_kernel,
        out_shape=jax.ShapeDtypeStruct((t, d), F32),
        grid=(d // tn, t // tm),
        in_specs=[
            pl.BlockSpec((tm, rw), lambda j, i: (i, 0)),
            pl.BlockSpec((tm, aw), lambda j, i: (i, 0)),
            pl.BlockSpec((None, rw + aw, tn), lambda j, i: (layer, 0, j)),
        ],
        out_specs=pl.BlockSpec((tm, tn), lambda j, i: (i, j)),
        scratch_shapes=[pltpu.VMEM((rw + aw, tn), BF16)],
        compiler_params=_cparams("parallel", "arbitrary"),
        name="out_proj",
    )(y_r, y_a, w_all)


def _rank(vals):
    n = vals.shape[0]
    idx = lax.broadcasted_iota(jnp.int32, vals.shape, 0)
    rank = jnp.zeros(vals.shape, jnp.int32)
    for e in range(n):
        row = vals[e:e + 1, :]
        beats = (row > vals) | ((row == vals) & (e < idx))
        rank = rank + beats.astype(jnp.int32)
    return rank


def _router_kernel(x_ref, y_ref, gain_y_ref, gate_y_ref, gain_ref, sh_ref, sc_ref, rwt_ref, rb_ref,
                   xn_ref, u_ref, idx_ref, pos_ref, gate_ref, cnt_ref, carry_ref):
    @pl.when(pl.program_id(0) == 0)
    def _():
        carry_ref[...] = jnp.zeros_like(carry_ref)

    y = y_ref[...]
    yn = y * lax.rsqrt(jnp.mean(y * y, axis=-1, keepdims=True) + RMS_EPS) * gain_y_ref[...]
    xn = x_ref[...] + gate_y_ref[...] * yn
    xn_ref[...] = xn
    u = _norm_mod(xn, gain_ref[...], sh_ref[...], sc_ref[...])
    u_ref[...] = _pack_halves(u)
    logits = lax.dot_general(rwt_ref[...], u, (((1,), (1,)), ((), ())),
                             preferred_element_type=F32, precision=HIGHEST)
    scores = _sigmoid(logits)
    sel = scores + rb_ref[:, 0:1]
    ne, tm = sel.shape
    gsz = ne // N_EXPERT_GROUPS
    sub = lax.broadcasted_iota(jnp.int32, (gsz, tm), 0)
    gs_rows = []
    for gi in range(N_EXPERT_GROUPS):
        grp = sel[gi * gsz:(gi + 1) * gsz, :]
        m1 = jnp.max(grp, axis=0, keepdims=True)
        i1 = jnp.min(jnp.where(grp == m1, sub, gsz), axis=0, keepdims=True)
        m2 = jnp.max(jnp.where(sub == i1, -jnp.inf, grp), axis=0, keepdims=True)
        gs_rows.append(m1 + m2)
    gkeep = _rank(jnp.concatenate(gs_rows, axis=0)) < TOPK_GROUPS
    ekeep = jnp.concatenate(
        [jnp.broadcast_to(gkeep[gi:gi + 1, :], (gsz, tm)) for gi in range(N_EXPERT_GROUPS)], axis=0)
    rank = _rank(jnp.where(ekeep, sel, -jnp.inf))
    chosen = rank < TOP_K
    wts = jnp.where(chosen, scores, 0.0)
    gates_t = wts / jnp.sum(wts, axis=0, keepdims=True) * ROUTED_SCALE

    cf = jnp.where(chosen, 1.0, 0.0)
    before = (lax.broadcasted_iota(jnp.int32, (tm, tm), 0) < lax.broadcasted_iota(jnp.int32, (tm, tm), 1))
    carry = carry_ref[:, 0:1]
    pos = carry + _dot(cf.astype(BF16), jnp.where(before, 1.0, 0.0).astype(BF16))
    carry_ref[...] = jnp.broadcast_to(carry + jnp.sum(cf, axis=1, keepdims=True), carry_ref.shape)
    cnt_ref[...] = carry_ref[...]

    eidx = lax.broadcasted_iota(jnp.int32, (ne, tm), 0)
    rows_i, rows_p, rows_g = [], [], []
    for k in range(TOP_K):
        mk = rank == k
        rows_i.append(jnp.sum(jnp.where(mk, eidx, 0), axis=0, keepdims=True))
        rows_p.append(jnp.sum(jnp.where(mk, pos, 0.0), axis=0, keepdims=True))
        rows_g.append(jnp.sum(jnp.where(mk, gates_t, 0.0), axis=0, keepdims=True))
    idx_ref[...] = jnp.concatenate(rows_i, axis=0)
    pos_ref[...] = jnp.concatenate(rows_p, axis=0).astype(jnp.int32)
    gate_ref[...] = jnp.concatenate(rows_g, axis=0)


def _router(x2, y, gain_y, gate_y, gain, shift, scale, rw_t, rb, seq, tm=256):
    t, d = x2.shape
    ne = rw_t.shape[0]
    bmap = lambda i: ((i * tm) // seq, 0, 0)
    pick = lambda dt: jax.ShapeDtypeStruct((TOP_K, t), dt)
    pspec = pl.BlockSpec((TOP_K, tm), lambda i: (0, i))
    row = pl.BlockSpec((tm, d), lambda i: (i, 0))
    vec = pl.BlockSpec((1, d), lambda i: (0, 0))
    per_batch = pl.BlockSpec((None, 1, d), bmap)
    return pl.pallas_call(
        _router_kernel,
        out_shape=[jax.ShapeDtypeStruct((t, d), F32), jax.ShapeDtypeStruct((t, d // 2), jnp.uint32),
                   pick(jnp.int32), pick(jnp.int32), pick(F32), jax.ShapeDtypeStruct((ne, LANES), F32)],
        grid=(t // tm,),
        in_specs=[row, row, vec, per_batch, vec, per_batch, per_batch,
                  pl.BlockSpec((ne, d), lambda i: (0, 0)), pl.BlockSpec((ne, 1), lambda i: (0, 0))],
        out_specs=[row, pl.BlockSpec((tm, d // 2), lambda i: (i, 0)), pspec, pspec, pspec,
                   pl.BlockSpec((ne, LANES), lambda i: (0, 0))],
        scratch_shapes=[pltpu.VMEM((ne, LANES), F32)],
        compiler_params=_cparams("arbitrary"),
        name="router",
    )(x2, y, gain_y, gate_y, gain, shift, scale, rw_t, rb)


def _row_copy(src, src_row, dst, dst_row, sem):
    return pltpu.make_async_copy(src.at[pl.ds(src_row, 1), :], dst.at[pl.ds(dst_row, 1), :], sem)


def _dispatch_kernel(rows_ref, u_ref, x_hbm, sem, *, n_pick):
    td = u_ref.shape[0]

    def body(tok, carry):
        for k in range(n_pick):
            _row_copy(u_ref, tok, x_hbm, rows_ref[0, k * td + tok], sem).start()
        return carry

    lax.fori_loop(0, td, body, 0)
    for k in range(n_pick):
        pltpu.make_async_copy(u_ref, x_hbm.at[pl.ds(0, td), :], sem).wait()


def _dispatch(u, rows_tiled, n_rows, td):
    t, d = u.shape
    n_pick = rows_tiled.shape[2] // td
    return pl.pallas_call(
        functools.partial(_dispatch_kernel, n_pick=n_pick),
        out_shape=jax.ShapeDtypeStruct((n_rows, d), u.dtype),
        grid=(t // td,),
        in_specs=[
            pl.BlockSpec((None, 1, n_pick * td), lambda i: (i, 0, 0), memory_space=pltpu.SMEM),
            pl.BlockSpec((td, d), lambda i: (i, 0)),
        ],
        out_specs=pl.BlockSpec(memory_space=pl.ANY),
        scratch_shapes=[pltpu.SemaphoreType.DMA(())],
        compiler_params=_cparams("arbitrary"),
        name="moe_dispatch",
    )(rows_tiled, u)


def _experts_kernel(tile_ref, exp_ref, lo_ref, hi_ref, first_ref, n_ref, x_ref, wgu_ref, wd_ref, y_ref):
    p = pl.program_id(0)

    @pl.when(p < n_ref[0])
    def _():
        ff = wd_ref.shape[0]
        half = x_ref.shape[1]
        x_lo, x_hi = _unpack_halves(x_ref[...])
        h = (_dot(x_lo.astype(BF16), wgu_ref[0:half, :].astype(BF16))
             + _dot(x_hi.astype(BF16), wgu_ref[half:, :].astype(BF16)))
        hg = h[:, :ff]
        hu = h[:, ff:]
        act = (hg * _sigmoid(hg) * hu).astype(BF16)
        row = lax.broadcasted_iota(jnp.int32, (x_ref.shape[0], 1), 0)
        mine = (row >= lo_ref[p]) & (row < hi_ref[p])

        @pl.when(first_ref[p] == 1)
        def _():
            y_ref[...] = _pack_halves(_dot(act, wd_ref[...].astype(BF16)))

        @pl.when(first_ref[p] == 0)
        def _():
            y_ref[...] = jnp.where(mine, _pack_halves(_dot(act, wd_ref[...].astype(BF16))), y_ref[...])


def _experts(x_sorted, work, w_gu, w_down, layer, tm):
    n_rows, half = x_sorted.shape
    _, ne1, d, ff2 = w_gu.shape
    ff = ff2 // 2
    n_work_max = n_rows // tm + ne1 - 1
    rows = pl.BlockSpec((tm, half), lambda p, tile, exp, *_: (tile[p], 0))
    return pl.pallas_call(
        _experts_kernel,
        out_shape=jax.ShapeDtypeStruct((n_rows, half), jnp.uint32),
        grid_spec=pltpu.PrefetchScalarGridSpec(
            num_scalar_prefetch=6,
            grid=(n_work_max,),
            in_specs=[
                rows,
                pl.BlockSpec((None, None, d, ff2), lambda p, tile, exp, *_: (layer, exp[p], 0, 0)),
                pl.BlockSpec((None, None, ff, d), lambda p, tile, exp, *_: (layer, exp[p], 0, 0)),
            ],
            out_specs=rows,
        ),
        compiler_params=_cparams("arbitrary"),
        name="moe_experts",
    )(*work, x_sorted, w_gu, w_down)


def _expert_work_list(counts, n_rows, tm):
    ne1 = counts.shape[0]
    n_work_max = n_rows // tm + ne1 - 1
    ends = jnp.cumsum(counts)
    starts = ends - counts
    first_tile = starts // tm
    n_tile = jnp.where(counts > 0, (ends - 1) // tm - first_tile + 1, 0)
    work_end = jnp.cumsum(n_tile)
    n_work = work_end[-1:]
    p = jnp.minimum(jnp.arange(n_work_max, dtype=jnp.int32), n_work - 1)
    exp = jnp.sum((work_end[None, :] <= p[:, None]).astype(jnp.int32), axis=1)
    tile = first_tile[exp] + p - (work_end - n_tile)[exp]
    lo = jnp.maximum(starts[exp] - tile * tm, 0)
    hi = jnp.minimum(ends[exp] - tile * tm, tm)
    first = jnp.concatenate([jnp.ones((1,), jnp.int32), (tile[1:] != tile[:-1]).astype(jnp.int32)])
    as_i32 = lambda a: a.astype(jnp.int32)
    return tuple(map(as_i32, (tile, exp, lo, hi, first, n_work))), starts


def _combine_kernel(rows_ref, nrows_ref, g_ref, x_ref, ys_ref, gain_ref, gate_ref, y_hbm, o_ref, buf, sems,
                    *, n_pick):
    i = pl.program_id(0)
    tt = x_ref.shape[0]
    slot = i % 2

    def gather(table_ref, dst_slot):
        def body(tok, carry):
            for k in range(n_pick):
                _row_copy(y_hbm, table_ref[0, k * tt + tok], buf.at[dst_slot, k], tok, sems.at[dst_slot]).start()
            return carry

        lax.fori_loop(0, tt, body, 0)

    @pl.when(i == 0)
    def _():
        gather(rows_ref, 0)

    @pl.when(i + 1 < pl.num_programs(0))
    def _():
        gather(nrows_ref, 1 - slot)

    for k in range(n_pick):
        pltpu.make_async_copy(y_hbm.at[pl.ds(0, tt), :], buf.at[slot, k], sems.at[slot]).wait()
    g = g_ref[...]
    acc_lo, acc_hi = _unpack_halves(ys_ref[...])
    for k in range(n_pick):
        lo, hi = _unpack_halves(buf[slot, k])
        acc_lo = acc_lo + lo * g[:, k:k + 1]
        acc_hi = acc_hi + hi * g[:, k:k + 1]
    acc = jnp.concatenate([acc_lo, acc_hi], axis=1)
    yn = acc * lax.rsqrt(jnp.mean(acc * acc, axis=-1, keepdims=True) + RMS_EPS) * gain_ref[...]
    o_ref[...] = x_ref[...] + gate_ref[...] * yn


def _combine(y_routed, y_shared, rows_tiled, gates, x2, gain, gate, seq, tt):
    t, d = x2.shape
    n_pick = rows_tiled.shape[2] // tt
    n_tiles = t // tt
    bmap = lambda i: ((i * tt) // seq, 0, 0)
    row = pl.BlockSpec((tt, d), lambda i: (i, 0))
    table = lambda imap: pl.BlockSpec((None, 1, n_pick * tt), imap, memory_space=pltpu.SMEM)
    return pl.pallas_call(
        functools.partial(_combine_kernel, n_pick=n_pick),
        out_shape=jax.ShapeDtypeStruct((t, d), F32),
        grid=(n_tiles,),
        in_specs=[
            table(lambda i: (i, 0, 0)),
            table(lambda i: (jnp.minimum(i + 1, n_tiles - 1), 0, 0)),
            pl.BlockSpec((tt, gates.shape[1]), lambda i: (i, 0)),
            row,
            pl.BlockSpec((tt, d // 2), lambda i: (i, 0)),
            pl.BlockSpec((1, d), lambda i: (0, 0)),
            pl.BlockSpec((None, 1, d), bmap),
            pl.BlockSpec(memory_space=pl.ANY),
        ],
        out_specs=row,
        scratch_shapes=[pltpu.VMEM((2, n_pick, tt, d // 2), jnp.uint32), pltpu.SemaphoreType.DMA((2,))],
        compiler_params=_cparams("arbitrary"),
        name="moe_combine",
    )(rows_tiled, rows_tiled, gates, x2, y_shared, gain, gate, y_routed)


def _tile_rows(rows, tile):
    n_pick, t = rows.shape
    return rows.reshape(n_pick, t // tile, tile).transpose(1, 0, 2).reshape(t // tile, 1, n_pick * tile)


def _moe_block(x2, y_mix, gain_mix, gate_mix, gain_in, shift, scale, rw_t, rb, w_gu, w_down, ws_gu, ws_down, layer,
               gain_out, gate, seq, tm=256, td=128, tt=64):
    t, d = x2.shape
    ne = rw_t.shape[0]
    x2, u, idx, pos, gate_k, counts = _router(x2, y_mix, gain_mix, gate_mix, gain_in, shift, scale, rw_t, rb, seq)
    work, starts = _expert_work_list(counts[:, 0].astype(jnp.int32), TOP_K * t, tm)
    hit = idx[:, :, None] == jnp.arange(ne, dtype=jnp.int32)[None, None, :]
    rows = jnp.sum(jnp.where(hit, starts[None, None, :], 0), axis=-1) + pos
    x_sorted = _dispatch(u, _tile_rows(rows, td), TOP_K * t, td)
    y_routed = _experts(x_sorted, work, w_gu, w_down, layer, tm)
    shared_work, _ = _expert_work_list(jnp.full((1,), t, jnp.int32), t, tm)
    y_shared = _experts(u, shared_work, ws_gu, ws_down, layer, tm)
    return _combine(y_routed, y_shared, _tile_rows(rows, tt), gate_k.T, x2, gain_out, gate, seq, tt)


def kernel(x, c, rel_bias, ada_w, ada_b, norm_gains, w_in, shift_mu, rwkv_w0, rwkv_w2, rwkv_a0, rwkv_a2, rwkv_g2,
           rwkv_k_k, rwkv_k_a, rwkv_r_k, rwkv_ln_w, rwkv_ln_b, attn_sinks, w_out, router_w, router_bias,
           expert_w_gu, expert_w_down, shared_w_gu, shared_w_down):
    batch, seq, d = x.shape
    depth = ada_w.shape[0]
    t = batch * seq
    rw = rwkv_w0.shape[1]
    dl, al, gl = rwkv_w2.shape[1], rwkv_a2.shape[1], rwkv_g2.shape[1]
    aw = w_out.shape[1] - rw
    n_q = aw // HEAD_DIM
    n_kv = (w_in.shape[2] - (3 * rw + dl + al + gl) - aw) // (2 * HEAD_DIM)
    q_per_kv = n_q // n_kv
    q_off = 3 * rw + dl + al + gl
    assert q_off % HEAD_DIM == 0, "attention heads must start at a multiple of the head width"

    c_pad = jnp.pad(c, ((0, 8 - batch % 8 if batch % 8 else 0), (0, 0)))
    ada = _ada(c_pad, ada_w, ada_b)[:, :batch]
    bias = _bias_table(rel_bias)
    w_in_t = jnp.swapaxes(w_in, 1, 2)
    w_gu_b = expert_w_gu
    ws_gu_b = shared_w_gu[:, None]

    x2 = x.reshape(t, d)
    for l in range(depth):
        sh_m, sc_m, gt_m, sh_f, sc_f, gt_f = [a.reshape(batch, 1, d) for a in jnp.split(ada[l], 6, axis=-1)]
        gains = norm_gains[l].reshape(4, 1, d)

        proj = _in_proj(_norm_mod_bf16(x2, gains[0], sh_m, sc_m, seq), w_in_t, l)

        r, k, v, a, g, lw = _rwkv_prep(proj, shift_mu[l][None, :], rwkv_w0[l][None, :], rwkv_w2[l],
                                       rwkv_a0[l][None, :], rwkv_a2[l], rwkv_g2[l], seq)
        y_r = _rwkv_chunk(r, k, v, a, g, lw, rwkv_k_k[l][None, :], rwkv_k_a[l][None, :],
                          rwkv_r_k[l].reshape(1, rw), rwkv_ln_w[l][None, :], rwkv_ln_b[l][None, :], batch, seq)
        y_a = _swa(proj, attn_sinks[l], bias, batch, seq, q_off, n_kv, q_per_kv)
        y_mix = _out_proj(y_r, y_a, w_out, l)
        x2 = _moe_block(x2, y_mix, gains[1], gt_m, gains[2], sh_f, sc_f, router_w[l].T, router_bias[l][:, None],
                        w_gu_b, expert_w_down, ws_gu_b, shared_w_down[:, None], l, gains[3], gt_f, seq)
    return x2.reshape(batch, seq, d)
```

```python
import functools
import math

import numpy as np
import jax
import jax.numpy as jnp
from jax import lax
from jax.experimental import pallas as pl
from jax.experimental.pallas import tpu as pltpu

F32 = jnp.float32
BF16 = jnp.bfloat16

HEAD_DIM = 64
LANES = 128
WINDOW = 128
N_BUCKETS = 32
MAX_DISTANCE = 128
N_EXPERT_GROUPS = 8
TOPK_GROUPS = 4
TOP_K = 8
ROUTED_SCALE = 2.5
RMS_EPS = 1e-6
GN_EPS = 64e-5
CHUNK = 64
HEADS_PER_GROUP = 4
GROUP_LANES = HEADS_PER_GROUP * HEAD_DIM
VMEM_LIMIT = 60 * 1024 * 1024

HIGHEST = lax.Precision.HIGHEST


def _round_up(n, m):
    return (n + m - 1) // m * m


def _cparams(*sem):
    return pltpu.CompilerParams(dimension_semantics=sem, vmem_limit_bytes=VMEM_LIMIT)


def _dot(a, b):
    return jnp.dot(a, b, preferred_element_type=F32)


def _dot_nt(a, b):
    return lax.dot_general(a, b, (((1,), (1,)), ((), ())), preferred_element_type=F32)


def _dot_tn(a, b):
    return lax.dot_general(a, b, (((0,), (0,)), ((), ())), preferred_element_type=F32)


def _bdot(a, b):
    return lax.dot_general(a, b, (((2,), (1,)), ((0,), (0,))), preferred_element_type=F32)


def _bdot_nt(a, b):
    return lax.dot_general(a, b, (((2,), (2,)), ((0,), (0,))), preferred_element_type=F32)


def _sigmoid(x):
    return 1.0 / (1.0 + jnp.exp(-x))


def _pack_halves(x):
    half = x.shape[1] // 2
    lo = lax.bitcast_convert_type(x[:, :half].astype(BF16).astype(F32), jnp.uint32)
    hi = lax.bitcast_convert_type(x[:, half:].astype(BF16).astype(F32), jnp.uint32)
    return (lo >> 16) | (hi & jnp.uint32(0xFFFF0000))


def _unpack_halves(p):
    lo = lax.bitcast_convert_type(p << 16, F32)
    hi = lax.bitcast_convert_type(p & jnp.uint32(0xFFFF0000), F32)
    return lo, hi


def _norm_mod(x, gain, shift, scale):
    y = x * lax.rsqrt(jnp.mean(x * x, axis=-1, keepdims=True) + RMS_EPS) * gain
    return y * (1.0 + scale) + shift


def _ada_kernel(c_ref, w_ref, b_ref, o_ref):
    c = c_ref[...]
    cond = c * _sigmoid(c)
    o_ref[...] = _dot(cond.astype(BF16), w_ref[...].astype(BF16)) + b_ref[...]


def _ada(c_pad, ada_w, ada_b):
    nl, d, n = ada_w.shape
    tn = 512
    return pl.pallas_call(
        _ada_kernel,
        out_shape=jax.ShapeDtypeStruct((nl, c_pad.shape[0], n), F32),
        grid=(nl, n // tn),
        in_specs=[
            pl.BlockSpec((c_pad.shape[0], d), lambda l, j: (0, 0)),
            pl.BlockSpec((None, d, tn), lambda l, j: (l, 0, j)),
            pl.BlockSpec((None, 1, tn), lambda l, j: (l, 0, j)),
        ],
        out_specs=pl.BlockSpec((None, c_pad.shape[0], tn), lambda l, j: (l, 0, j)),
        compiler_params=_cparams("parallel", "parallel"),
        name="ada",
    )(c_pad, ada_w, ada_b.reshape(nl, 1, n))


def _bias_kernel(rbt_ref, oh_ref, o_ref):
    o_ref[...] = jnp.dot(rbt_ref[...], oh_ref[...], preferred_element_type=F32, precision=HIGHEST)


def _bucket_onehot():
    qi = np.arange(WINDOW)[:, None]
    kj = np.arange(2 * WINDOW)[None, :]
    dist = qi + WINDOW - kj
    max_exact = N_BUCKETS // 2
    d = np.maximum(dist, 0)
    far = max_exact + (np.log(np.maximum(d, max_exact).astype(np.float32) / np.float32(max_exact))
                       / np.float32(math.log(MAX_DISTANCE / max_exact))
                       * np.float32(N_BUCKETS - max_exact)).astype(np.int32)
    bucket = np.where(d < max_exact, d, np.minimum(far, N_BUCKETS - 1)).reshape(-1)
    return (np.arange(N_BUCKETS)[:, None] == bucket[None, :]).astype(np.float32)


def _bias_table(rel_bias):
    nb, nh = rel_bias.shape
    onehot = jnp.asarray(_bucket_onehot())
    out = pl.pallas_call(
        _bias_kernel,
        out_shape=jax.ShapeDtypeStruct((nh, WINDOW * 2 * WINDOW), F32),
        name="bias_table",
        compiler_params=pltpu.CompilerParams(vmem_limit_bytes=VMEM_LIMIT),
    )(rel_bias.T, onehot)
    return out.reshape(nh, WINDOW, 2 * WINDOW)


def _norm_mod_kernel(x_ref, gain_ref, sh_ref, sc_ref, u_ref):
    u_ref[...] = _norm_mod(x_ref[...], gain_ref[...], sh_ref[...], sc_ref[...]).astype(u_ref.dtype)


def _norm_mod_bf16(x2, gain, shift, scale, seq, tm=512):
    t, d = x2.shape
    bmap = lambda i: ((i * tm) // seq, 0, 0)
    row = pl.BlockSpec((tm, d), lambda i: (i, 0))
    return pl.pallas_call(
        _norm_mod_kernel,
        out_shape=jax.ShapeDtypeStruct((t, d), BF16),
        grid=(t // tm,),
        in_specs=[row, pl.BlockSpec((1, d), lambda i: (0, 0)), pl.BlockSpec((None, 1, d), bmap),
                  pl.BlockSpec((None, 1, d), bmap)],
        out_specs=row,
        compiler_params=_cparams("parallel"),
        name="norm_mod",
    )(x2, gain, shift, scale)


def _in_proj_kernel(u_ref, wt_ref, o_ref, wb_ref):
    @pl.when(pl.program_id(1) == 0)
    def _():
        wb_ref[...] = wt_ref[...].astype(BF16)

    o_ref[...] = _dot_nt(u_ref[...], wb_ref[...])


def _in_proj(u, wt_all, layer, tm=1024, tn=512):
    t, d = u.shape
    n = wt_all.shape[1]
    tm = min(tm, t)
    return pl.pallas_call(
        _in_proj_kernel,
        out_shape=jax.ShapeDtypeStruct((t, n), F32),
        grid=(pl.cdiv(n, tn), t // tm),
        in_specs=[
            pl.BlockSpec((tm, d), lambda j, i: (i, 0)),
            pl.BlockSpec((None, tn, d), lambda j, i: (layer, j, 0)),
        ],
        out_specs=pl.BlockSpec((tm, tn), lambda j, i: (i, j)),
        scratch_shapes=[pltpu.VMEM((tn, d), BF16)],
        compiler_params=_cparams("parallel", "arbitrary"),
        name="in_proj",
    )(u, wt_all)


def _lane_window(start, width):
    lo = start // LANES * LANES
    return lo, _round_up(start + width, LANES), start - lo


def _window_rows(w, start):
    lo, hi, off = _lane_window(start, w.shape[0])
    return jnp.pad(w, ((off, hi - lo - off - w.shape[0]), (0, 0))).astype(BF16)


def _rwkv_prep_kernel(rw, xw_win, xa_win, xg_win, p_ref, prev_ref, mu_ref, w0_ref, w2_ref, a0_ref, a2_ref, g2_ref,
                      r_ref, k_ref, v_ref, a_ref, g_ref, lw_ref, *, blocks_per_seq):
    i = pl.program_id(0)
    cur = p_ref[...]
    first = (i % blocks_per_seq) == 0
    prev_row = jnp.where(first, 0.0, prev_ref[7:8, :])
    rolled = pltpu.roll(cur, 1, axis=0)
    row = lax.broadcasted_iota(jnp.int32, cur.shape, 0)
    shifted = jnp.where(row == 0, prev_row, rolled)
    x = cur + (shifted - cur) * mu_ref[...]
    r_ref[...] = x[:, 0:rw].astype(r_ref.dtype)
    k_ref[...] = x[:, rw:2 * rw].astype(k_ref.dtype)
    v_ref[...] = x[:, 2 * rw:3 * rw].astype(v_ref.dtype)
    xw = x[:, xw_win[0]:xw_win[1]]
    xa = x[:, xa_win[0]:xa_win[1]]
    xg = x[:, xg_win[0]:xg_win[1]]
    z = w0_ref[...] + _dot(jnp.tanh(xw).astype(BF16), w2_ref[...])
    lw_ref[...] = -math.exp(-0.5) * _sigmoid(z)
    a_ref[...] = _sigmoid(a0_ref[...] + _dot(xa.astype(BF16), a2_ref[...])).astype(a_ref.dtype)
    g_ref[...] = _dot(_sigmoid(xg).astype(BF16), g2_ref[...]).astype(g_ref.dtype)


def _rwkv_prep(proj, mu, w0, w2, a0, a2, g2, seq, ts=256):
    t = proj.shape[0]
    rw = w0.shape[1]
    dl, al, gl = w2.shape[0], a2.shape[0], g2.shape[0]
    starts = (3 * rw, 3 * rw + dl, 3 * rw + dl + al)
    wins = [_lane_window(s, w)[:2] for s, w in zip(starts, (dl, al, gl))]
    wblock = wins[2][1]
    w2p, a2p, g2p = (_window_rows(w, s) for w, s in zip((w2, a2, g2), starts))
    mu_p = jnp.pad(mu, ((0, 0), (0, wblock - mu.shape[1])))
    full = lambda i: (0, 0)
    outs = [jax.ShapeDtypeStruct((t, rw), BF16)] * 5 + [jax.ShapeDtypeStruct((t, rw), F32)]
    ospec = pl.BlockSpec((ts, rw), lambda i: (i, 0))
    return pl.pallas_call(
        functools.partial(_rwkv_prep_kernel, rw, *wins, blocks_per_seq=seq // ts),
        out_shape=outs,
        grid=(t // ts,),
        in_specs=[
            pl.BlockSpec((ts, wblock), lambda i: (i, 0)),
            pl.BlockSpec((8, wblock), lambda i: (jnp.maximum(i * (ts // 8) - 1, 0), 0)),
            pl.BlockSpec((1, wblock), full),
            pl.BlockSpec((1, rw), full),
            pl.BlockSpec(w2p.shape, full),
            pl.BlockSpec((1, rw), full),
            pl.BlockSpec(a2p.shape, full),
            pl.BlockSpec(g2p.shape, full),
        ],
        out_specs=[ospec] * 6,
        compiler_params=_cparams("parallel"),
        name="rwkv_prep",
    )(proj, proj, mu_p, w0, w2p, a0, a2p, g2p)


CHUNK_LEVELS = int(math.log2(CHUNK))


def _chunk_constants():
    n = GROUP_LANES
    i = np.arange(n)[:, None]
    j = np.arange(n)[None, :]
    same_head = (i // CHUNK) == (j // CHUNK)
    code = np.zeros((n, n), np.int32)
    for lvl in range(CHUNK_LEVELS, 0, -1):
        code = np.where(same_head & (i > j) & ((i >> lvl) == (j >> lvl)), lvl, code)
    planes = [code == lvl for lvl in range(1, CHUNK_LEVELS + 1)] + [code > 0, same_head]
    return np.stack(planes).astype(np.float32)


def _seg_sum(x, ones_bd):
    return _dot(x.astype(BF16), ones_bd)


def _split3(x):
    hi = x.astype(BF16)
    r1 = x - hi.astype(F32)
    mid = r1.astype(BF16)
    return hi, mid, (r1 - mid.astype(F32)).astype(BF16)


def _rwkv_chunk_kernel(r_ref, k_ref, v_ref, a_ref, g_ref, lw_ref,
                       kk_ref, ka_ref, rk_ref, lnw_ref, lnb_ref, planes_ref,
                       o_ref, s_ref):
    @pl.when(pl.program_id(2) == 0)
    def _():
        s_ref[...] = jnp.zeros_like(s_ref)

    n = GROUP_LANES
    lvl_mask = [planes_ref[l] for l in range(CHUNK_LEVELS)]
    low_mask = planes_ref[CHUNK_LEVELS]
    bd = planes_ref[CHUNK_LEVELS + 1]
    tb = r_ref.shape[0]
    nch = tb // CHUNK
    tri_incl = (lax.broadcasted_iota(jnp.int32, (1, CHUNK, n), 1)
                >= lax.broadcasted_iota(jnp.int32, (1, CHUNK, n), 2) % CHUNK)
    ltri = jnp.where(lax.broadcasted_iota(jnp.int32, (nch, CHUNK, CHUNK), 1)
                     >= lax.broadcasted_iota(jnp.int32, (nch, CHUNK, CHUNK), 2), 1.0, 0.0).astype(BF16)
    eye = jnp.where(lax.broadcasted_iota(jnp.int32, (1, n, n), 1)
                    == lax.broadcasted_iota(jnp.int32, (1, n, n), 2), 1.0, 0.0).astype(BF16)

    def chunks(x):
        return x.reshape(nch, CHUNK, n)

    def tile_heads(x):
        return jnp.concatenate([x] * HEADS_PER_GROUP, axis=1)

    def stack(x):
        return tile_heads(x.astype(BF16)) * bd[None]

    r = r_ref[...].astype(F32)
    k = k_ref[...].astype(F32)
    v = v_ref[...].astype(F32)
    a = a_ref[...].astype(F32)
    lw = lw_ref[...]
    kkp = k * kk_ref[...]
    kk = kkp / jnp.maximum(jnp.sqrt(_seg_sum(kkp * kkp, bd)), 1e-12)
    k2 = k * (1.0 + (a - 1.0) * ka_ref[...])
    kb = kk * a
    bonus = _seg_sum(r * k2 * rk_ref[...], bd) * v

    cum = sum(_bdot(ltri, chunks(piece)) for piece in _split3(lw))
    cum_end = cum[:, CHUNK - 1:CHUNK, :]
    e_in = jnp.exp(cum)
    e_ex = jnp.exp(cum - chunks(lw))
    e_neg = jnp.exp(-cum)
    e_dec = jnp.exp(cum_end - cum)
    decay = jnp.exp(cum_end)
    r_t = chunks(r) * e_in
    a_t = -chunks(kk) * e_ex
    kb3 = chunks(kb)
    k23 = chunks(k2)

    ar = jnp.concatenate([a_t, r_t], axis=1).astype(BF16)
    v_s = stack(chunks(v))
    ab_rb = _bdot_nt(ar, stack(kb3 * e_neg))
    ak_rk = _bdot_nt(ar, stack(k23 * e_neg))
    rb_w = jnp.where(tri_incl, ab_rb[:, CHUNK:], 0.0).astype(BF16)
    rk_w = jnp.where(tri_incl, ak_rk[:, CHUNK:], 0.0).astype(BF16)
    amat = tile_heads(ab_rb[:, :CHUNK].astype(BF16))
    akmat = tile_heads(ak_rk[:, :CHUNK].astype(BF16)) * low_mask[None]

    tinv = eye + amat * lvl_mask[0][None]
    for lvl in range(1, CHUNK_LEVELS):
        tinv = tinv + _bdot(tinv, _bdot(amat * lvl_mask[lvl][None], tinv).astype(BF16)).astype(BF16)

    w_s = _bdot(tinv, stack(a_t)).astype(BF16)
    z_s = _bdot(tinv, _bdot(akmat, v_s).astype(BF16)).astype(BF16)
    bh_s = stack(kb3 * e_dec)
    kh_s = stack(k23 * e_dec)
    r_eff = (r_t + _bdot(rb_w, w_s)).astype(BF16)
    y0 = _bdot(rb_w, z_s) + _bdot(rk_w, v_s)

    s = s_ref[...]
    ys = []
    for ci in range(nch):
        p_mat = _dot_tn(w_s[ci], bh_s[ci]).astype(BF16)
        q_mat = _dot_tn(z_s[ci], bh_s[ci]) + _dot_tn(v_s[ci], kh_s[ci])
        s_b = s.astype(BF16)
        ys.append(_dot_nt(r_eff[ci], s_b) + y0[ci])
        s = s * decay[ci] + _dot(s_b, p_mat) + q_mat
    s_ref[...] = s

    y = jnp.concatenate(ys, axis=0)
    inv_n = 1.0 / HEAD_DIM
    mean = _seg_sum(y, bd) * inv_n
    dlt = y - mean
    var = _seg_sum(dlt * dlt, bd) * inv_n
    yn = dlt * lax.rsqrt(var + GN_EPS) * lnw_ref[...] + lnb_ref[...]
    o_ref[...] = ((yn + bonus) * g_ref[...].astype(F32)).astype(o_ref.dtype)


def _rwkv_chunk(r, k, v, a, g, lw, k_k, k_a, r_k, ln_w, ln_b, batch, seq, tb=512):
    t, rw = r.shape
    n = GROUP_LANES
    planes = jnp.asarray(_chunk_constants(), dtype=BF16)
    nblk = seq // tb
    tmap = lambda b, h, c: (b * nblk + c, h)
    pmap = lambda b, h, c: (0, h)
    tspec = pl.BlockSpec((tb, n), tmap)
    pspec = pl.BlockSpec((1, n), pmap)
    return pl.pallas_call(
        _rwkv_chunk_kernel,
        out_shape=jax.ShapeDtypeStruct((t, rw), BF16),
        grid=(batch, rw // n, nblk),
        in_specs=[tspec] * 6 + [pspec] * 5 + [pl.BlockSpec(planes.shape, lambda b, h, c: (0, 0, 0))],
        out_specs=pl.BlockSpec((tb, n), tmap),
        scratch_shapes=[pltpu.VMEM((n, n), F32)],
        compiler_params=_cparams("parallel", "parallel", "arbitrary"),
        name="rwkv_chunk",
    )(r, k, v, a, g, lw, k_k, k_a, r_k, ln_w, ln_b, planes)


def _swa_kernel(sink_ref, *refs, q_per_kv, q_off, k_off, v_off, n_q_tiles):
    q_refs = refs[:n_q_tiles]
    kp_ref, kc_ref, vp_ref, vc_ref, bias_ref, o_ref = refs[n_q_tiles:]
    j = pl.program_id(1)
    nblk = pl.program_id(2)
    blk = WINDOW
    upper = lax.broadcasted_iota(jnp.int32, (1, LANES), 1) >= HEAD_DIM

    def both_halves(x, start_col):
        in_upper = (start_col // HEAD_DIM) % 2 == 1
        sel = jnp.where(upper == in_upper, x, 0.0)
        return sel + pltpu.roll(sel, HEAD_DIM, axis=1)

    kw = both_halves(jnp.concatenate([kp_ref[...], kc_ref[...]], axis=0), k_off + HEAD_DIM * j).astype(BF16)
    vw = both_halves(jnp.concatenate([vp_ref[...], vc_ref[...]], axis=0), v_off + HEAD_DIM * j)
    v_half = (jnp.where(upper, 0.0, vw).astype(BF16), jnp.where(upper, vw, 0.0).astype(BF16))
    qi = lax.broadcasted_iota(jnp.int32, (blk, 2 * blk), 0)
    kj = lax.broadcasted_iota(jnp.int32, (blk, 2 * blk), 1)
    valid = (kj > qi) & (kj <= qi + blk) & ((nblk > 0) | (kj >= blk))
    scale = HEAD_DIM ** -0.5
    for p in range(q_per_kv // 2):
        acc = None
        for half in range(2):
            g = 2 * p + half
            col = q_off % LANES + HEAD_DIM * g
            q_tile = q_refs[col // LANES][...] * scale
            q_upper = (col // HEAD_DIM) % 2 == 1
            qm = jnp.where(upper == q_upper, q_tile, 0.0).astype(BF16)
            sink = sink_ref[j * q_per_kv + g]
            s = _dot_nt(qm, kw) + bias_ref[g]
            s = jnp.where(valid, s, -jnp.inf)
            m = jnp.maximum(jnp.max(s, axis=-1, keepdims=True), sink)
            pr = jnp.exp(s - m)
            denom = jnp.sum(pr, axis=-1, keepdims=True) + jnp.exp(sink - m)
            o = _dot(pr.astype(BF16), v_half[half]) / denom
            acc = o if acc is None else acc + o
        o_ref[:, p * LANES:(p + 1) * LANES] = acc.astype(o_ref.dtype)


def _swa(proj, sinks, bias, batch, seq, q_off, n_kv, q_per_kv):
    t = proj.shape[0]
    blk = WINDOW
    nb = seq // blk
    qw = q_per_kv * HEAD_DIM
    k_off = q_off + n_kv * qw
    v_off = k_off + n_kv * HEAD_DIM
    n_q_tiles = (q_off % LANES + qw + LANES - 1) // LANES
    cur = lambda b, j, n: b * nb + n
    prev = lambda b, j, n: b * nb + jnp.maximum(n - 1, 0)
    tile = lambda rowf, colf: pl.BlockSpec((blk, LANES), lambda b, j, n: (rowf(b, j, n), colf(j)))
    q_specs = [tile(cur, lambda j, i=i: q_off // LANES + (qw // LANES) * j + i) for i in range(n_q_tiles)]
    k_col = lambda j: (k_off + HEAD_DIM * j) // LANES
    v_col = lambda j: (v_off + HEAD_DIM * j) // LANES
    return pl.pallas_call(
        functools.partial(_swa_kernel, q_per_kv=q_per_kv, q_off=q_off, k_off=k_off, v_off=v_off,
                          n_q_tiles=n_q_tiles),
        out_shape=jax.ShapeDtypeStruct((t, n_kv * qw), BF16),
        grid=(batch, n_kv, nb),
        in_specs=[pl.BlockSpec(memory_space=pltpu.SMEM)] + q_specs + [
            tile(prev, k_col), tile(cur, k_col), tile(prev, v_col), tile(cur, v_col),
            pl.BlockSpec((q_per_kv, blk, 2 * blk), lambda b, j, n: (j, 0, 0)),
        ],
        out_specs=pl.BlockSpec((blk, qw), lambda b, j, n: (cur(b, j, n), j)),
        compiler_params=_cparams("parallel", "parallel", "parallel"),
        name="swa",
    )(sinks, *([proj] * (n_q_tiles + 4)), bias)


def _out_proj_kernel(yr_ref, ya_ref, w_ref, o_ref, wb_ref):
    @pl.when(pl.program_id(1) == 0)
    def _():
        wb_ref[...] = w_ref[...].astype(BF16)

    rw = yr_ref.shape[1]
    o_ref[...] = _dot(yr_ref[...], wb_ref[0:rw, :]) + _dot(ya_ref[...], wb_ref[rw:, :])


def _out_proj(y_r, y_a, w_all, layer, tm=1024, tn=512):
    t, rw = y_r.shape
    aw = y_a.shape[1]
    d = w_all.shape[2]
    tm = min(tm, t)
    return pl.pallas_call(
        _out_proj_kernel,
        out_shape=jax.ShapeDtypeStruct((t, d), F32),
        grid=(d // tn, t // tm),
        in_specs=[
            pl.BlockSpec((tm, rw), lambda j, i: (i, 0)),
            pl.BlockSpec((tm, aw), lambda j, i: (i, 0)),
            pl.BlockSpec((None, rw + aw, tn), lambda j, i: (layer, 0, j)),
        ],
        out_specs=pl.BlockSpec((tm, tn), lambda j, i: (i, j)),
        scratch_shapes=[pltpu.VMEM((rw + aw, tn), BF16)],
        compiler_params=_cparams("parallel", "arbitrary"),
        name="out_proj",
    )(y_r, y_a, w_all)


def _rank(vals):
    n = vals.shape[0]
    idx = lax.broadcasted_iota(jnp.int32, vals.shape, 0)
    rank = jnp.zeros(vals.shape, jnp.int32)
    for e in range(n):
        row = vals[e:e + 1, :]
        beats = (row > vals) | ((row == vals) & (e < idx))
        rank = rank + beats.astype(jnp.int32)
    return rank


def _router_kernel(x_ref, y_ref, gain_y_ref, gate_y_ref, gain_ref, sh_ref, sc_ref, rwt_ref, rb_ref,
                   xn_ref, u_ref, idx_ref, pos_ref, gate_ref, cnt_ref, carry_ref):
    @pl.when(pl.program_id(0) == 0)
    def _():
        carry_ref[...] = jnp.zeros_like(carry_ref)

    y = y_ref[...]
    yn = y * lax.rsqrt(jnp.mean(y * y, axis=-1, keepdims=True) + RMS_EPS) * gain_y_ref[...]
    xn = x_ref[...] + gate_y_ref[...] * yn
    xn_ref[...] = xn
    u = _norm_mod(xn, gain_ref[...], sh_ref[...], sc_ref[...])
    u_ref[...] = _pack_halves(u)
    logits = lax.dot_general(rwt_ref[...], u, (((1,), (1,)), ((), ())),
                             preferred_element_type=F32, precision=HIGHEST)
    scores = _sigmoid(logits)
    sel = scores + rb_ref[:, 0:1]
    ne, tm = sel.shape
    gsz = ne // N_EXPERT_GROUPS
    sub = lax.broadcasted_iota(jnp.int32, (gsz, tm), 0)
    gs_rows = []
    for gi in range(N_EXPERT_GROUPS):
        grp = sel[gi * gsz:(gi + 1) * gsz, :]
        m1 = jnp.max(grp, axis=0, keepdims=True)
        i1 = jnp.min(jnp.where(grp == m1, sub, gsz), axis=0, keepdims=True)
        m2 = jnp.max(jnp.where(sub == i1, -jnp.inf, grp), axis=0, keepdims=True)
        gs_rows.append(m1 + m2)
    gkeep = _rank(jnp.concatenate(gs_rows, axis=0)) < TOPK_GROUPS
    ekeep = jnp.concatenate(
        [jnp.broadcast_to(gkeep[gi:gi + 1, :], (gsz, tm)) for gi in range(N_EXPERT_GROUPS)], axis=0)
    rank = _rank(jnp.where(ekeep, sel, -jnp.inf))
    chosen = rank < TOP_K
    wts = jnp.where(chosen, scores, 0.0)
    gates_t = wts / jnp.sum(wts, axis=0, keepdims=True) * ROUTED_SCALE

    cf = jnp.where(chosen, 1.0, 0.0)
    before = (lax.broadcasted_iota(jnp.int32, (tm, tm), 0) < lax.broadcasted_iota(jnp.int32, (tm, tm), 1))
    carry = carry_ref[:, 0:1]
    pos = carry + _dot(cf.astype(BF16), jnp.where(before, 1.0, 0.0).astype(BF16))
    carry_ref[...] = jnp.broadcast_to(carry + jnp.sum(cf, axis=1, keepdims=True), carry_ref.shape)
    cnt_ref[...] = carry_ref[...]

    eidx = lax.broadcasted_iota(jnp.int32, (ne, tm), 0)
    rows_i, rows_p, rows_g = [], [], []
    for k in range(TOP_K):
        mk = rank == k
        rows_i.append(jnp.sum(jnp.where(mk, eidx, 0), axis=0, keepdims=True))
        rows_p.append(jnp.sum(jnp.where(mk, pos, 0.0), axis=0, keepdims=True))
        rows_g.append(jnp.sum(jnp.where(mk, gates_t, 0.0), axis=0, keepdims=True))
    idx_ref[...] = jnp.concatenate(rows_i, axis=0)
    pos_ref[...] = jnp.concatenate(rows_p, axis=0).astype(jnp.int32)
    gate_ref[...] = jnp.concatenate(rows_g, axis=0)


def _router(x2, y, gain_y, gate_y, gain, shift, scale, rw_t, rb, seq, tm=256):
    t, d = x2.shape
    ne = rw_t.shape[0]
    bmap = lambda i: ((i * tm) // seq, 0, 0)
    pick = lambda dt: jax.ShapeDtypeStruct((TOP_K, t), dt)
    pspec = pl.BlockSpec((TOP_K, tm), lambda i: (0, i))
    row = pl.BlockSpec((tm, d), lambda i: (i, 0))
    vec = pl.BlockSpec((1, d), lambda i: (0, 0))
    per_batch = pl.BlockSpec((None, 1, d), bmap)
    return pl.pallas_call(
        _router_kernel,
        out_shape=[jax.ShapeDtypeStruct((t, d), F32), jax.ShapeDtypeStruct((t, d // 2), jnp.uint32),
                   pick(jnp.int32), pick(jnp.int32), pick(F32), jax.ShapeDtypeStruct((ne, LANES), F32)],
        grid=(t // tm,),
        in_specs=[row, row, vec, per_batch, vec, per_batch, per_batch,
                  pl.BlockSpec((ne, d), lambda i: (0, 0)), pl.BlockSpec((ne, 1), lambda i: (0, 0))],
        out_specs=[row, pl.BlockSpec((tm, d // 2), lambda i: (i, 0)), pspec, pspec, pspec,
                   pl.BlockSpec((ne, LANES), lambda i: (0, 0))],
        scratch_shapes=[pltpu.VMEM((ne, LANES), F32)],
        compiler_params=_cparams("arbitrary"),
        name="router",
    )(x2, y, gain_y, gate_y, gain, shift, scale, rw_t, rb)


def _row_copy(src, src_row, dst, dst_row, sem):
    return pltpu.make_async_copy(src.at[pl.ds(src_row, 1), :], dst.at[pl.ds(dst_row, 1), :], sem)


def _dispatch_kernel(rows_ref, u_ref, x_hbm, sem, *, n_pick):
    td = u_ref.shape[0]

    def body(tok, carry):
        for k in range(n_pick):
            _row_copy(u_ref, tok, x_hbm, rows_ref[0, k * td + tok], sem).start()
        return carry

    lax.fori_loop(0, td, body, 0)
    for k in range(n_pick):
        pltpu.make_async_copy(u_ref, x_hbm.at[pl.ds(0, td), :], sem).wait()


def _dispatch(u, rows_tiled, n_rows, td):
    t, d = u.shape
    n_pick = rows_tiled.shape[2] // td
    return pl.pallas_call(
        functools.partial(_dispatch_kernel, n_pick=n_pick),
        out_shape=jax.ShapeDtypeStruct((n_rows, d), u.dtype),
        grid=(t // td,),
        in_specs=[
            pl.BlockSpec((None, 1, n_pick * td), lambda i: (i, 0, 0), memory_space=pltpu.SMEM),
            pl.BlockSpec((td, d), lambda i: (i, 0)),
        ],
        out_specs=pl.BlockSpec(memory_space=pl.ANY),
        scratch_shapes=[pltpu.SemaphoreType.DMA(())],
        compiler_params=_cparams("arbitrary"),
        name="moe_dispatch",
    )(rows_tiled, u)


def _experts_kernel(tile_ref, exp_ref, lo_ref, hi_ref, first_ref, new_ref, slot_ref, next_ref, n_ref,
                    x_ref, wgu_hbm, wd_hbm, y_ref, wgu_buf, wd_buf, sems, *, layer):
    p = pl.program_id(0)

    def weight_copies(e, s):
        return (pltpu.make_async_copy(wgu_hbm.at[layer, e], wgu_buf.at[s], sems.at[0, s]),
                pltpu.make_async_copy(wd_hbm.at[layer, e], wd_buf.at[s], sems.at[1, s]))

    @pl.when(p < n_ref[0])
    def _():
        s = slot_ref[p]

        @pl.when(new_ref[p] == 1)
        def _():
            @pl.when(p == 0)
            def _():
                for c in weight_copies(exp_ref[p], s):
                    c.start()

            for c in weight_copies(exp_ref[p], s):
                c.wait()

            @pl.when(next_ref[p] >= 0)
            def _():
                for c in weight_copies(next_ref[p], 1 - s):
                    c.start()

        wgu_ref = wgu_buf.at[s]
        wd_ref = wd_buf.at[s]
        ff = wd_ref.shape[0]
        half = x_ref.shape[1]
        x_lo, x_hi = _unpack_halves(x_ref[...])
        h = (_dot(x_lo.astype(BF16), wgu_ref[0:half, :].astype(BF16))
             + _dot(x_hi.astype(BF16), wgu_ref[half:, :].astype(BF16)))
        hg = h[:, :ff]
        hu = h[:, ff:]
        act = (hg * _sigmoid(hg) * hu).astype(BF16)
        row = lax.broadcasted_iota(jnp.int32, (x_ref.shape[0], 1), 0)
        mine = (row >= lo_ref[p]) & (row < hi_ref[p])

        @pl.when(first_ref[p] == 1)
        def _():
            y_ref[...] = _pack_halves(_dot(act, wd_ref[...].astype(BF16)))

        @pl.when(first_ref[p] == 0)
        def _():
            y_ref[...] = jnp.where(mine, _pack_halves(_dot(act, wd_ref[...].astype(BF16))), y_ref[...])


def _experts(x_sorted, work, w_gu, w_down, layer, tm):
    n_rows, half = x_sorted.shape
    _, ne1, d, ff2 = w_gu.shape
    ff = ff2 // 2
    n_work_max = n_rows // tm + ne1 - 1
    rows = pl.BlockSpec((tm, half), lambda p, tile, *_: (tile[p], 0))
    return pl.pallas_call(
        functools.partial(_experts_kernel, layer=layer),
        out_shape=jax.ShapeDtypeStruct((n_rows, half), jnp.uint32),
        grid_spec=pltpu.PrefetchScalarGridSpec(
            num_scalar_prefetch=len(work),
            grid=(n_work_max,),
            in_specs=[rows, pl.BlockSpec(memory_space=pl.ANY), pl.BlockSpec(memory_space=pl.ANY)],
            out_specs=rows,
            scratch_shapes=[pltpu.VMEM((2, d, ff2), F32), pltpu.VMEM((2, ff, d), F32),
                            pltpu.SemaphoreType.DMA((2, 2))],
        ),
        compiler_params=_cparams("arbitrary"),
        name="moe_experts",
    )(*work, x_sorted, w_gu, w_down)


def _expert_work_list(counts, n_rows, tm):
    ne1 = counts.shape[0]
    n_work_max = n_rows // tm + ne1 - 1
    ends = jnp.cumsum(counts)
    starts = ends - counts
    first_tile = starts // tm
    n_tile = jnp.where(counts > 0, (ends - 1) // tm - first_tile + 1, 0)
    work_end = jnp.cumsum(n_tile)
    n_work = work_end[-1:]
    p = jnp.minimum(jnp.arange(n_work_max, dtype=jnp.int32), n_work - 1)
    exp = jnp.sum((work_end[None, :] <= p[:, None]).astype(jnp.int32), axis=1)
    tile = first_tile[exp] + p - (work_end - n_tile)[exp]
    lo = jnp.maximum(starts[exp] - tile * tm, 0)
    hi = jnp.minimum(ends[exp] - tile * tm, tm)
    one = jnp.ones((1,), jnp.int32)
    first = jnp.concatenate([one, (tile[1:] != tile[:-1]).astype(jnp.int32)])
    new = jnp.concatenate([one, (exp[1:] != exp[:-1]).astype(jnp.int32)])
    slot = (jnp.cumsum(new) - 1) % 2
    run_end = work_end[exp]
    nxt = jnp.where(run_end < n_work, exp[jnp.minimum(run_end, n_work_max - 1)], -1)
    as_i32 = lambda a: a.astype(jnp.int32)
    return tuple(map(as_i32, (tile, exp, lo, hi, first, new, slot, nxt, n_work))), starts


def _combine_kernel(rows_ref, nrows_ref, g_ref, x_ref, ys_ref, gain_ref, gate_ref, y_hbm, o_ref, buf, sems,
                    *, n_pick):
    i = pl.program_id(0)
    tt = x_ref.shape[0]
    slot = i % 2

    def gather(table_ref, dst_slot):
        def body(tok, carry):
            for k in range(n_pick):
                _row_copy(y_hbm, table_ref[0, k * tt + tok], buf.at[dst_slot, k], tok, sems.at[dst_slot]).start()
            return carry

        lax.fori_loop(0, tt, body, 0)

    @pl.when(i == 0)
    def _():
        gather(rows_ref, 0)

    @pl.when(i + 1 < pl.num_programs(0))
    def _():
        gather(nrows_ref, 1 - slot)

    for k in range(n_pick):
        pltpu.make_async_copy(y_hbm.at[pl.ds(0, tt), :], buf.at[slot, k], sems.at[slot]).wait()
    g = g_ref[...]
    acc_lo, acc_hi = _unpack_halves(ys_ref[...])
    for k in range(n_pick):
        lo, hi = _unpack_halves(buf[slot, k])
        acc_lo = acc_lo + lo * g[:, k:k + 1]
        acc_hi = acc_hi + hi * g[:, k:k + 1]
    acc = jnp.concatenate([acc_lo, acc_hi], axis=1)
    yn = acc * lax.rsqrt(jnp.mean(acc * acc, axis=-1, keepdims=True) + RMS_EPS) * gain_ref[...]
    o_ref[...] = x_ref[...] + gate_ref[...] * yn


def _combine(y_routed, y_shared, rows_tiled, gates, x2, gain, gate, seq, tt):
    t, d = x2.shape
    n_pick = rows_tiled.shape[2] // tt
    n_tiles = t // tt
    bmap = lambda i: ((i * tt) // seq, 0, 0)
    row = pl.BlockSpec((tt, d), lambda i: (i, 0))
    table = lambda imap: pl.BlockSpec((None, 1, n_pick * tt), imap, memory_space=pltpu.SMEM)
    return pl.pallas_call(
        functools.partial(_combine_kernel, n_pick=n_pick),
        out_shape=jax.ShapeDtypeStruct((t, d), F32),
        grid=(n_tiles,),
        in_specs=[
            table(lambda i: (i, 0, 0)),
            table(lambda i: (jnp.minimum(i + 1, n_tiles - 1), 0, 0)),
            pl.BlockSpec((tt, gates.shape[1]), lambda i: (i, 0)),
            row,
            pl.BlockSpec((tt, d // 2), lambda i: (i, 0)),
            pl.BlockSpec((1, d), lambda i: (0, 0)),
            pl.BlockSpec((None, 1, d), bmap),
            pl.BlockSpec(memory_space=pl.ANY),
        ],
        out_specs=row,
        scratch_shapes=[pltpu.VMEM((2, n_pick, tt, d // 2), jnp.uint32), pltpu.SemaphoreType.DMA((2,))],
        compiler_params=_cparams("arbitrary"),
        name="moe_combine",
    )(rows_tiled, rows_tiled, gates, x2, y_shared, gain, gate, y_routed)


def _tile_rows(rows, tile):
    n_pick, t = rows.shape
    return rows.reshape(n_pick, t // tile, tile).transpose(1, 0, 2).reshape(t // tile, 1, n_pick * tile)


def _moe_block(x2, y_mix, gain_mix, gate_mix, gain_in, shift, scale, rw_t, rb, w_gu, w_down, ws_gu, ws_down, layer,
               gain_out, gate, seq, tm=256, td=128, tt=64):
    t, d = x2.shape
    ne = rw_t.shape[0]
    x2, u, idx, pos, gate_k, counts = _router(x2, y_mix, gain_mix, gate_mix, gain_in, shift, scale, rw_t, rb, seq)
    work, starts = _expert_work_list(counts[:, 0].astype(jnp.int32), TOP_K * t, tm)
    hit = idx[:, :, None] == jnp.arange(ne, dtype=jnp.int32)[None, None, :]
    rows = jnp.sum(jnp.where(hit, starts[None, None, :], 0), axis=-1) + pos
    x_sorted = _dispatch(u, _tile_rows(rows, td), TOP_K * t, td)
    y_routed = _experts(x_sorted, work, w_gu, w_down, layer, tm)
    shared_work, _ = _expert_work_list(jnp.full((1,), t, jnp.int32), t, tm)
    y_shared = _experts(u, shared_work, ws_gu, ws_down, layer, tm)
    return _combine(y_routed, y_shared, _tile_rows(rows, tt), gate_k.T, x2, gain_out, gate, seq, tt)


def kernel(x, c, rel_bias, ada_w, ada_b, norm_gains, w_in, shift_mu, rwkv_w0, rwkv_w2, rwkv_a0, rwkv_a2, rwkv_g2,
           rwkv_k_k, rwkv_k_a, rwkv_r_k, rwkv_ln_w, rwkv_ln_b, attn_sinks, w_out, router_w, router_bias,
           expert_w_gu, expert_w_down, shared_w_gu, shared_w_down):
    batch, seq, d = x.shape
    depth = ada_w.shape[0]
    t = batch * seq
    rw = rwkv_w0.shape[1]
    dl, al, gl = rwkv_w2.shape[1], rwkv_a2.shape[1], rwkv_g2.shape[1]
    aw = w_out.shape[1] - rw
    n_q = aw // HEAD_DIM
    n_kv = (w_in.shape[2] - (3 * rw + dl + al + gl) - aw) // (2 * HEAD_DIM)
    q_per_kv = n_q // n_kv
    q_off = 3 * rw + dl + al + gl
    assert q_off % HEAD_DIM == 0, "attention heads must start at a multiple of the head width"

    c_pad = jnp.pad(c, ((0, 8 - batch % 8 if batch % 8 else 0), (0, 0)))
    ada = _ada(c_pad, ada_w, ada_b)[:, :batch]
    bias = _bias_table(rel_bias)
    w_in_t = jnp.swapaxes(w_in, 1, 2)
    w_gu_b = expert_w_gu
    ws_gu_b = shared_w_gu[:, None]

    x2 = x.reshape(t, d)
    for l in range(depth):
        sh_m, sc_m, gt_m, sh_f, sc_f, gt_f = [a.reshape(batch, 1, d) for a in jnp.split(ada[l], 6, axis=-1)]
        gains = norm_gains[l].reshape(4, 1, d)

        proj = _in_proj(_norm_mod_bf16(x2, gains[0], sh_m, sc_m, seq), w_in_t, l)

        r, k, v, a, g, lw = _rwkv_prep(proj, shift_mu[l][None, :], rwkv_w0[l][None, :], rwkv_w2[l],
                                       rwkv_a0[l][None, :], rwkv_a2[l], rwkv_g2[l], seq)
        y_r = _rwkv_chunk(r, k, v, a, g, lw, rwkv_k_k[l][None, :], rwkv_k_a[l][None, :],
                          rwkv_r_k[l].reshape(1, rw), rwkv_ln_w[l][None, :], rwkv_ln_b[l][None, :], batch, seq)
        y_a = _swa(proj, attn_sinks[l], bias, batch, seq, q_off, n_kv, q_per_kv)
        y_mix = _out_proj(y_r, y_a, w_out, l)
        x2 = _moe_block(x2, y_mix, gains[1], gt_m, gains[2], sh_f, sc_f, router_w[l].T, router_bias[l][:, None],
                        w_gu_b, expert_w_down, ws_gu_b, shared_w_down[:, None], l, gains[3], gt_f, seq)
    return x2.reshape(batch, seq, d)
```

```python
import functools
import math

import numpy as np
import jax
import jax.numpy as jnp
from jax import lax
from jax.experimental import pallas as pl
from jax.experimental.pallas import tpu as pltpu

F32 = jnp.float32
BF16 = jnp.bfloat16

HEAD_DIM = 64
LANES = 128
WINDOW = 128
N_BUCKETS = 32
MAX_DISTANCE = 128
N_EXPERT_GROUPS = 8
TOPK_GROUPS = 4
TOP_K = 8
ROUTED_SCALE = 2.5
RMS_EPS = 1e-6
GN_EPS = 64e-5
CHUNK = 64
HEADS_PER_GROUP = 4
GROUP_LANES = HEADS_PER_GROUP * HEAD_DIM
VMEM_LIMIT = 60 * 1024 * 1024

HIGHEST = lax.Precision.HIGHEST


def _round_up(n, m):
    return (n + m - 1) // m * m


def _cparams(*sem):
    return pltpu.CompilerParams(dimension_semantics=sem, vmem_limit_bytes=VMEM_LIMIT)


def _dot(a, b):
    return jnp.dot(a, b, preferred_element_type=F32)


def _dot_nt(a, b):
    return lax.dot_general(a, b, (((1,), (1,)), ((), ())), preferred_element_type=F32)


def _dot_tn(a, b):
    return lax.dot_general(a, b, (((0,), (0,)), ((), ())), preferred_element_type=F32)


def _bdot(a, b):
    return lax.dot_general(a, b, (((2,), (1,)), ((0,), (0,))), preferred_element_type=F32)


def _bdot_nt(a, b):
    return lax.dot_general(a, b, (((2,), (2,)), ((0,), (0,))), preferred_element_type=F32)


def _sigmoid(x):
    return 1.0 / (1.0 + jnp.exp(-x))


def _pack_halves(x):
    half = x.shape[1] // 2
    lo = lax.bitcast_convert_type(x[:, :half].astype(BF16).astype(F32), jnp.uint32)
    hi = lax.bitcast_convert_type(x[:, half:].astype(BF16).astype(F32), jnp.uint32)
    return (lo >> 16) | (hi & jnp.uint32(0xFFFF0000))


def _unpack_halves(p):
    lo = lax.bitcast_convert_type(p << 16, F32)
    hi = lax.bitcast_convert_type(p & jnp.uint32(0xFFFF0000), F32)
    return lo, hi


def _norm_mod(x, gain, shift, scale):
    y = x * lax.rsqrt(jnp.mean(x * x, axis=-1, keepdims=True) + RMS_EPS) * gain
    return y * (1.0 + scale) + shift


def _ada_kernel(c_ref, w_ref, b_ref, o_ref):
    c = c_ref[...]
    cond = c * _sigmoid(c)
    o_ref[...] = _dot(cond.astype(BF16), w_ref[...].astype(BF16)) + b_ref[...]


def _ada(c_pad, ada_w, ada_b):
    nl, d, n = ada_w.shape
    tn = 512
    return pl.pallas_call(
        _ada_kernel,
        out_shape=jax.ShapeDtypeStruct((nl, c_pad.shape[0], n), F32),
        grid=(nl, n // tn),
        in_specs=[
            pl.BlockSpec((c_pad.shape[0], d), lambda l, j: (0, 0)),
            pl.BlockSpec((None, d, tn), lambda l, j: (l, 0, j)),
            pl.BlockSpec((None, 1, tn), lambda l, j: (l, 0, j)),
        ],
        out_specs=pl.BlockSpec((None, c_pad.shape[0], tn), lambda l, j: (l, 0, j)),
        compiler_params=_cparams("parallel", "parallel"),
        name="ada",
    )(c_pad, ada_w, ada_b.reshape(nl, 1, n))


def _bias_kernel(rbt_ref, oh_ref, o_ref):
    o_ref[...] = jnp.dot(rbt_ref[...], oh_ref[...], preferred_element_type=F32, precision=HIGHEST)


def _bucket_onehot():
    qi = np.arange(WINDOW)[:, None]
    kj = np.arange(2 * WINDOW)[None, :]
    dist = qi + WINDOW - kj
    max_exact = N_BUCKETS // 2
    d = np.maximum(dist, 0)
    far = max_exact + (np.log(np.maximum(d, max_exact).astype(np.float32) / np.float32(max_exact))
                       / np.float32(math.log(MAX_DISTANCE / max_exact))
                       * np.float32(N_BUCKETS - max_exact)).astype(np.int32)
    bucket = np.where(d < max_exact, d, np.minimum(far, N_BUCKETS - 1)).reshape(-1)
    return (np.arange(N_BUCKETS)[:, None] == bucket[None, :]).astype(np.float32)


def _bias_table(rel_bias):
    nb, nh = rel_bias.shape
    onehot = jnp.asarray(_bucket_onehot())
    out = pl.pallas_call(
        _bias_kernel,
        out_shape=jax.ShapeDtypeStruct((nh, WINDOW * 2 * WINDOW), F32),
        name="bias_table",
        compiler_params=pltpu.CompilerParams(vmem_limit_bytes=VMEM_LIMIT),
    )(rel_bias.T, onehot)
    return out.reshape(nh, WINDOW, 2 * WINDOW)


def _norm_mod_kernel(x_ref, gain_ref, sh_ref, sc_ref, u_ref):
    u_ref[...] = _norm_mod(x_ref[...], gain_ref[...], sh_ref[...], sc_ref[...]).astype(u_ref.dtype)


def _norm_mod_bf16(x2, gain, shift, scale, seq, tm=512):
    t, d = x2.shape
    bmap = lambda i: ((i * tm) // seq, 0, 0)
    row = pl.BlockSpec((tm, d), lambda i: (i, 0))
    return pl.pallas_call(
        _norm_mod_kernel,
        out_shape=jax.ShapeDtypeStruct((t, d), BF16),
        grid=(t // tm,),
        in_specs=[row, pl.BlockSpec((1, d), lambda i: (0, 0)), pl.BlockSpec((None, 1, d), bmap),
                  pl.BlockSpec((None, 1, d), bmap)],
        out_specs=row,
        compiler_params=_cparams("parallel"),
        name="norm_mod",
    )(x2, gain, shift, scale)


def _in_proj_kernel(u_ref, wt_ref, o_ref, wb_ref):
    @pl.when(pl.program_id(1) == 0)
    def _():
        wb_ref[...] = wt_ref[...].astype(BF16)

    o_ref[...] = _dot_nt(u_ref[...], wb_ref[...])


def _in_proj(u, wt_all, layer, tm=1024, tn=512):
    t, d = u.shape
    n = wt_all.shape[1]
    tm = min(tm, t)
    return pl.pallas_call(
        _in_proj_kernel,
        out_shape=jax.ShapeDtypeStruct((t, n), F32),
        grid=(pl.cdiv(n, tn), t // tm),
        in_specs=[
            pl.BlockSpec((tm, d), lambda j, i: (i, 0)),
            pl.BlockSpec((None, tn, d), lambda j, i: (layer, j, 0)),
        ],
        out_specs=pl.BlockSpec((tm, tn), lambda j, i: (i, j)),
        scratch_shapes=[pltpu.VMEM((tn, d), BF16)],
        compiler_params=_cparams("parallel", "arbitrary"),
        name="in_proj",
    )(u, wt_all)


def _lane_window(start, width):
    lo = start // LANES * LANES
    return lo, _round_up(start + width, LANES), start - lo


def _window_rows(w, start):
    lo, hi, off = _lane_window(start, w.shape[0])
    return jnp.pad(w, ((off, hi - lo - off - w.shape[0]), (0, 0))).astype(BF16)


def _rwkv_prep_kernel(rw, xw_win, xa_win, xg_win, p_ref, prev_ref, mu_ref, w0_ref, w2_ref, a0_ref, a2_ref, g2_ref,
                      r_ref, k_ref, v_ref, a_ref, g_ref, lw_ref, *, blocks_per_seq):
    i = pl.program_id(0)
    cur = p_ref[...]
    first = (i % blocks_per_seq) == 0
    prev_row = jnp.where(first, 0.0, prev_ref[7:8, :])
    rolled = pltpu.roll(cur, 1, axis=0)
    row = lax.broadcasted_iota(jnp.int32, cur.shape, 0)
    shifted = jnp.where(row == 0, prev_row, rolled)
    x = cur + (shifted - cur) * mu_ref[...]
    r_ref[...] = x[:, 0:rw].astype(r_ref.dtype)
    k_ref[...] = x[:, rw:2 * rw].astype(k_ref.dtype)
    v_ref[...] = x[:, 2 * rw:3 * rw].astype(v_ref.dtype)
    xw = x[:, xw_win[0]:xw_win[1]]
    xa = x[:, xa_win[0]:xa_win[1]]
    xg = x[:, xg_win[0]:xg_win[1]]
    z = w0_ref[...] + _dot(jnp.tanh(xw).astype(BF16), w2_ref[...])
    lw_ref[...] = -math.exp(-0.5) * _sigmoid(z)
    a_ref[...] = _sigmoid(a0_ref[...] + _dot(xa.astype(BF16), a2_ref[...])).astype(a_ref.dtype)
    g_ref[...] = _dot(_sigmoid(xg).astype(BF16), g2_ref[...]).astype(g_ref.dtype)


def _rwkv_prep(proj, mu, w0, w2, a0, a2, g2, seq, ts=256):
    t = proj.shape[0]
    rw = w0.shape[1]
    dl, al, gl = w2.shape[0], a2.shape[0], g2.shape[0]
    starts = (3 * rw, 3 * rw + dl, 3 * rw + dl + al)
    wins = [_lane_window(s, w)[:2] for s, w in zip(starts, (dl, al, gl))]
    wblock = wins[2][1]
    w2p, a2p, g2p = (_window_rows(w, s) for w, s in zip((w2, a2, g2), starts))
    mu_p = jnp.pad(mu, ((0, 0), (0, wblock - mu.shape[1])))
    full = lambda i: (0, 0)
    outs = [jax.ShapeDtypeStruct((t, rw), BF16)] * 5 + [jax.ShapeDtypeStruct((t, rw), F32)]
    ospec = pl.BlockSpec((ts, rw), lambda i: (i, 0))
    return pl.pallas_call(
        functools.partial(_rwkv_prep_kernel, rw, *wins, blocks_per_seq=seq // ts),
        out_shape=outs,
        grid=(t // ts,),
        in_specs=[
            pl.BlockSpec((ts, wblock), lambda i: (i, 0)),
            pl.BlockSpec((8, wblock), lambda i: (jnp.maximum(i * (ts // 8) - 1, 0), 0)),
            pl.BlockSpec((1, wblock), full),
            pl.BlockSpec((1, rw), full),
            pl.BlockSpec(w2p.shape, full),
            pl.BlockSpec((1, rw), full),
            pl.BlockSpec(a2p.shape, full),
            pl.BlockSpec(g2p.shape, full),
        ],
        out_specs=[ospec] * 6,
        compiler_params=_cparams("parallel"),
        name="rwkv_prep",
    )(proj, proj, mu_p, w0, w2p, a0, a2p, g2p)


CHUNK_LEVELS = int(math.log2(CHUNK))


def _chunk_constants():
    n = GROUP_LANES
    i = np.arange(n)[:, None]
    j = np.arange(n)[None, :]
    same_head = (i // CHUNK) == (j // CHUNK)
    code = np.zeros((n, n), np.int32)
    for lvl in range(CHUNK_LEVELS, 0, -1):
        code = np.where(same_head & (i > j) & ((i >> lvl) == (j >> lvl)), lvl, code)
    planes = [code == lvl for lvl in range(1, CHUNK_LEVELS + 1)] + [code > 0, same_head]
    return np.stack(planes).astype(np.float32)


def _seg_sum(x, ones_bd):
    return _dot(x.astype(BF16), ones_bd)


def _split3(x):
    hi = x.astype(BF16)
    r1 = x - hi.astype(F32)
    mid = r1.astype(BF16)
    return hi, mid, (r1 - mid.astype(F32)).astype(BF16)


def _rwkv_chunk_kernel(r_ref, k_ref, v_ref, a_ref, g_ref, lw_ref,
                       kk_ref, ka_ref, rk_ref, lnw_ref, lnb_ref, planes_ref,
                       o_ref, s_ref):
    @pl.when(pl.program_id(2) == 0)
    def _():
        s_ref[...] = jnp.zeros_like(s_ref)

    n = GROUP_LANES
    lvl_mask = [planes_ref[l] for l in range(CHUNK_LEVELS)]
    low_mask = planes_ref[CHUNK_LEVELS]
    bd = planes_ref[CHUNK_LEVELS + 1]
    tb = r_ref.shape[0]
    nch = tb // CHUNK
    tri_incl = (lax.broadcasted_iota(jnp.int32, (1, CHUNK, n), 1)
                >= lax.broadcasted_iota(jnp.int32, (1, CHUNK, n), 2) % CHUNK)
    ltri = jnp.where(lax.broadcasted_iota(jnp.int32, (nch, CHUNK, CHUNK), 1)
                     >= lax.broadcasted_iota(jnp.int32, (nch, CHUNK, CHUNK), 2), 1.0, 0.0).astype(BF16)
    eye = jnp.where(lax.broadcasted_iota(jnp.int32, (1, n, n), 1)
                    == lax.broadcasted_iota(jnp.int32, (1, n, n), 2), 1.0, 0.0).astype(BF16)

    def chunks(x):
        return x.reshape(nch, CHUNK, n)

    def tile_heads(x):
        return jnp.concatenate([x] * HEADS_PER_GROUP, axis=1)

    def stack(x):
        return tile_heads(x.astype(BF16)) * bd[None]

    r = r_ref[...].astype(F32)
    k = k_ref[...].astype(F32)
    v = v_ref[...].astype(F32)
    a = a_ref[...].astype(F32)
    lw = lw_ref[...]
    kkp = k * kk_ref[...]
    kk = kkp / jnp.maximum(jnp.sqrt(_seg_sum(kkp * kkp, bd)), 1e-12)
    k2 = k * (1.0 + (a - 1.0) * ka_ref[...])
    kb = kk * a
    bonus = _seg_sum(r * k2 * rk_ref[...], bd) * v

    cum = sum(_bdot(ltri, chunks(piece)) for piece in _split3(lw))
    cum_end = cum[:, CHUNK - 1:CHUNK, :]
    e_in = jnp.exp(cum)
    e_ex = jnp.exp(cum - chunks(lw))
    e_neg = jnp.exp(-cum)
    e_dec = jnp.exp(cum_end - cum)
    decay = jnp.exp(cum_end)
    r_t = chunks(r) * e_in
    a_t = -chunks(kk) * e_ex
    kb3 = chunks(kb)
    k23 = chunks(k2)

    ar = jnp.concatenate([a_t, r_t], axis=1).astype(BF16)
    v_s = stack(chunks(v))
    ab_rb = _bdot_nt(ar, stack(kb3 * e_neg))
    ak_rk = _bdot_nt(ar, stack(k23 * e_neg))
    rb_w = jnp.where(tri_incl, ab_rb[:, CHUNK:], 0.0).astype(BF16)
    rk_w = jnp.where(tri_incl, ak_rk[:, CHUNK:], 0.0).astype(BF16)
    amat = tile_heads(ab_rb[:, :CHUNK].astype(BF16))
    akmat = tile_heads(ak_rk[:, :CHUNK].astype(BF16)) * low_mask[None]

    def lower_rows(x, s):
        m = x.shape[2]
        return x.reshape(nch * n // s, 2, s // 2, m)[:, 1].reshape(nch, n // 2, m)

    def as_lower_rows(x, s):
        m = x.shape[2]
        x4 = x.reshape(nch * n // s, 1, s // 2, m)
        return jnp.concatenate([jnp.zeros_like(x4), x4], axis=1).reshape(nch, n, m)

    tinv = eye + amat * lvl_mask[0][None]
    for lvl in range(1, CHUNK_LEVELS):
        off = amat * lvl_mask[lvl][None]
        s = 2 ** (lvl + 1)
        if s // 2 >= 16:
            m1 = as_lower_rows(_bdot(lower_rows(off, s), tinv).astype(BF16), s)
            tinv = tinv + as_lower_rows(_bdot(lower_rows(tinv, s), m1).astype(BF16), s)
        else:
            tinv = tinv + _bdot(tinv, _bdot(off, tinv).astype(BF16)).astype(BF16)

    w_s = _bdot(tinv, stack(a_t)).astype(BF16)
    z_s = _bdot(tinv, _bdot(akmat, v_s).astype(BF16)).astype(BF16)
    bh_s = stack(kb3 * e_dec)
    kh_s = stack(k23 * e_dec)
    r_eff = (r_t + _bdot(rb_w, w_s)).astype(BF16)
    y0 = _bdot(rb_w, z_s) + _bdot(rk_w, v_s)

    s = s_ref[...]
    ys = []
    for ci in range(nch):
        p_mat = _dot_tn(w_s[ci], bh_s[ci]).astype(BF16)
        q_mat = _dot_tn(z_s[ci], bh_s[ci]) + _dot_tn(v_s[ci], kh_s[ci])
        s_b = s.astype(BF16)
        ys.append(_dot_nt(r_eff[ci], s_b) + y0[ci])
        s = s * decay[ci] + _dot(s_b, p_mat) + q_mat
    s_ref[...] = s

    y = jnp.concatenate(ys, axis=0)
    inv_n = 1.0 / HEAD_DIM
    mean = _seg_sum(y, bd) * inv_n
    dlt = y - mean
    var = _seg_sum(dlt * dlt, bd) * inv_n
    yn = dlt * lax.rsqrt(var + GN_EPS) * lnw_ref[...] + lnb_ref[...]
    o_ref[...] = ((yn + bonus) * g_ref[...].astype(F32)).astype(o_ref.dtype)


def _rwkv_chunk(r, k, v, a, g, lw, k_k, k_a, r_k, ln_w, ln_b, batch, seq, tb=512):
    t, rw = r.shape
    n = GROUP_LANES
    planes = jnp.asarray(_chunk_constants(), dtype=BF16)
    nblk = seq // tb
    tmap = lambda b, h, c: (b * nblk + c, h)
    pmap = lambda b, h, c: (0, h)
    tspec = pl.BlockSpec((tb, n), tmap)
    pspec = pl.BlockSpec((1, n), pmap)
    return pl.pallas_call(
        _rwkv_chunk_kernel,
        out_shape=jax.ShapeDtypeStruct((t, rw), BF16),
        grid=(batch, rw // n, nblk),
        in_specs=[tspec] * 6 + [pspec] * 5 + [pl.BlockSpec(planes.shape, lambda b, h, c: (0, 0, 0))],
        out_specs=pl.BlockSpec((tb, n), tmap),
        scratch_shapes=[pltpu.VMEM((n, n), F32)],
        compiler_params=_cparams("parallel", "parallel", "arbitrary"),
        name="rwkv_chunk",
    )(r, k, v, a, g, lw, k_k, k_a, r_k, ln_w, ln_b, planes)


def _swa_kernel(sink_ref, *refs, q_per_kv, q_off, k_off, v_off, n_q_tiles):
    q_refs = refs[:n_q_tiles]
    kp_ref, kc_ref, vp_ref, vc_ref, bias_ref, o_ref = refs[n_q_tiles:]
    j = pl.program_id(1)
    nblk = pl.program_id(2)
    blk = WINDOW
    upper = lax.broadcasted_iota(jnp.int32, (1, LANES), 1) >= HEAD_DIM

    def both_halves(x, start_col):
        in_upper = (start_col // HEAD_DIM) % 2 == 1
        sel = jnp.where(upper == in_upper, x, 0.0)
        return sel + pltpu.roll(sel, HEAD_DIM, axis=1)

    kw = both_halves(jnp.concatenate([kp_ref[...], kc_ref[...]], axis=0), k_off + HEAD_DIM * j).astype(BF16)
    vw = both_halves(jnp.concatenate([vp_ref[...], vc_ref[...]], axis=0), v_off + HEAD_DIM * j)
    v_half = (jnp.where(upper, 0.0, vw).astype(BF16), jnp.where(upper, vw, 0.0).astype(BF16))
    qi = lax.broadcasted_iota(jnp.int32, (blk, 2 * blk), 0)
    kj = lax.broadcasted_iota(jnp.int32, (blk, 2 * blk), 1)
    valid = (kj > qi) & (kj <= qi + blk) & ((nblk > 0) | (kj >= blk))
    scale = HEAD_DIM ** -0.5
    for p in range(q_per_kv // 2):
        acc = None
        for half in range(2):
            g = 2 * p + half
            col = q_off % LANES + HEAD_DIM * g
            q_tile = q_refs[col // LANES][...] * scale
            q_upper = (col // HEAD_DIM) % 2 == 1
            qm = jnp.where(upper == q_upper, q_tile, 0.0).astype(BF16)
            sink = sink_ref[j * q_per_kv + g]
            s = _dot_nt(qm, kw) + bias_ref[g]
            s = jnp.where(valid, s, -jnp.inf)
            m = jnp.maximum(jnp.max(s, axis=-1, keepdims=True), sink)
            pr = jnp.exp(s - m)
            denom = jnp.sum(pr, axis=-1, keepdims=True) + jnp.exp(sink - m)
            o = _dot(pr.astype(BF16), v_half[half]) / denom
            acc = o if acc is None else acc + o
        o_ref[:, p * LANES:(p + 1) * LANES] = acc.astype(o_ref.dtype)


def _swa(proj, sinks, bias, batch, seq, q_off, n_kv, q_per_kv):
    t = proj.shape[0]
    blk = WINDOW
    nb = seq // blk
    qw = q_per_kv * HEAD_DIM
    k_off = q_off + n_kv * qw
    v_off = k_off + n_kv * HEAD_DIM
    n_q_tiles = (q_off % LANES + qw + LANES - 1) // LANES
    cur = lambda b, j, n: b * nb + n
    prev = lambda b, j, n: b * nb + jnp.maximum(n - 1, 0)
    tile = lambda rowf, colf: pl.BlockSpec((blk, LANES), lambda b, j, n: (rowf(b, j, n), colf(j)))
    q_specs = [tile(cur, lambda j, i=i: q_off // LANES + (qw // LANES) * j + i) for i in range(n_q_tiles)]
    k_col = lambda j: (k_off + HEAD_DIM * j) // LANES
    v_col = lambda j: (v_off + HEAD_DIM * j) // LANES
    return pl.pallas_call(
        functools.partial(_swa_kernel, q_per_kv=q_per_kv, q_off=q_off, k_off=k_off, v_off=v_off,
                          n_q_tiles=n_q_tiles),
        out_shape=jax.ShapeDtypeStruct((t, n_kv * qw), BF16),
        grid=(batch, n_kv, nb),
        in_specs=[pl.BlockSpec(memory_space=pltpu.SMEM)] + q_specs + [
            tile(prev, k_col), tile(cur, k_col), tile(prev, v_col), tile(cur, v_col),
            pl.BlockSpec((q_per_kv, blk, 2 * blk), lambda b, j, n: (j, 0, 0)),
        ],
        out_specs=pl.BlockSpec((blk, qw), lambda b, j, n: (cur(b, j, n), j)),
        compiler_params=_cparams("parallel", "parallel", "parallel"),
        name="swa",
    )(sinks, *([proj] * (n_q_tiles + 4)), bias)


def _out_proj_kernel(yr_ref, ya_ref, w_ref, o_ref, wb_ref):
    @pl.when(pl.program_id(1) == 0)
    def _():
        wb_ref[...] = w_ref[...].astype(BF16)

    rw = yr_ref.shape[1]
    o_ref[...] = _dot(yr_ref[...], wb_ref[0:rw, :]) + _dot(ya_ref[...], wb_ref[rw:, :])


def _out_proj(y_r, y_a, w_all, layer, tm=1024, tn=512):
    t, rw = y_r.shape
    aw = y_a.shape[1]
    d = w_all.shape[2]
    tm = min(tm, t)
    return pl.pallas_call(
        _out_proj_kernel,
        out_shape=jax.ShapeDtypeStruct((t, d), F32),
        grid=(d // tn, t // tm),
        in_specs=[
            pl.BlockSpec((tm, rw), lambda j, i: (i, 0)),
            pl.BlockSpec((tm, aw), lambda j, i: (i, 0)),
            pl.BlockSpec((None, rw + aw, tn), lambda j, i: (layer, 0, j)),
        ],
        out_specs=pl.BlockSpec((tm, tn), lambda j, i: (i, j)),
        scratch_shapes=[pltpu.VMEM((rw + aw, tn), BF16)],
        compiler_params=_cparams("parallel", "arbitrary"),
        name="out_proj",
    )(y_r, y_a, w_all)


def _rank(vals):
    n = vals.shape[0]
    idx = lax.broadcasted_iota(jnp.int32, vals.shape, 0)
    rank = jnp.zeros(vals.shape, jnp.int32)
    for e in range(n):
        row = vals[e:e + 1, :]
        beats = (row > vals) | ((row == vals) & (e < idx))
        rank = rank + beats.astype(jnp.int32)
    return rank


def _router_kernel(x_ref, y_ref, gain_y_ref, gate_y_ref, gain_ref, sh_ref, sc_ref, rwt_ref, rb_ref,
                   xn_ref, u_ref, idx_ref, pos_ref, gate_ref, cnt_ref, carry_ref):
    @pl.when(pl.program_id(0) == 0)
    def _():
        carry_ref[...] = jnp.zeros_like(carry_ref)

    y = y_ref[...]
    yn = y * lax.rsqrt(jnp.mean(y * y, axis=-1, keepdims=True) + RMS_EPS) * gain_y_ref[...]
    xn = x_ref[...] + gate_y_ref[...] * yn
    xn_ref[...] = xn
    u = _norm_mod(xn, gain_ref[...], sh_ref[...], sc_ref[...])
    u_ref[...] = _pack_halves(u)
    logits = lax.dot_general(rwt_ref[...], u, (((1,), (1,)), ((), ())),
                             preferred_element_type=F32, precision=HIGHEST)
    scores = _sigmoid(logits)
    sel = scores + rb_ref[:, 0:1]
    ne, tm = sel.shape
    gsz = ne // N_EXPERT_GROUPS
    sub = lax.broadcasted_iota(jnp.int32, (gsz, tm), 0)
    gs_rows = []
    for gi in range(N_EXPERT_GROUPS):
        grp = sel[gi * gsz:(gi + 1) * gsz, :]
        m1 = jnp.max(grp, axis=0, keepdims=True)
        i1 = jnp.min(jnp.where(grp == m1, sub, gsz), axis=0, keepdims=True)
        m2 = jnp.max(jnp.where(sub == i1, -jnp.inf, grp), axis=0, keepdims=True)
        gs_rows.append(m1 + m2)
    gkeep = _rank(jnp.concatenate(gs_rows, axis=0)) < TOPK_GROUPS
    ekeep = jnp.concatenate(
        [jnp.broadcast_to(gkeep[gi:gi + 1, :], (gsz, tm)) for gi in range(N_EXPERT_GROUPS)], axis=0)
    rank = _rank(jnp.where(ekeep, sel, -jnp.inf))
    chosen = rank < TOP_K
    wts = jnp.where(chosen, scores, 0.0)
    gates_t = wts / jnp.sum(wts, axis=0, keepdims=True) * ROUTED_SCALE

    cf = jnp.where(chosen, 1.0, 0.0)
    before = (lax.broadcasted_iota(jnp.int32, (tm, tm), 0) < lax.broadcasted_iota(jnp.int32, (tm, tm), 1))
    carry = carry_ref[:, 0:1]
    pos = carry + _dot(cf.astype(BF16), jnp.where(before, 1.0, 0.0).astype(BF16))
    carry_ref[...] = jnp.broadcast_to(carry + jnp.sum(cf, axis=1, keepdims=True), carry_ref.shape)
    cnt_ref[...] = carry_ref[...]

    eidx = lax.broadcasted_iota(jnp.int32, (ne, tm), 0)
    rows_i, rows_p, rows_g = [], [], []
    for k in range(TOP_K):
        mk = rank == k
        rows_i.append(jnp.sum(jnp.where(mk, eidx, 0), axis=0, keepdims=True))
        rows_p.append(jnp.sum(jnp.where(mk, pos, 0.0), axis=0, keepdims=True))
        rows_g.append(jnp.sum(jnp.where(mk, gates_t, 0.0), axis=0, keepdims=True))
    idx_ref[...] = jnp.concatenate(rows_i, axis=0)
    pos_ref[...] = jnp.concatenate(rows_p, axis=0).astype(jnp.int32)
    gate_ref[...] = jnp.concatenate(rows_g, axis=0)


def _router(x2, y, gain_y, gate_y, gain, shift, scale, rw_t, rb, seq, tm=256):
    t, d = x2.shape
    ne = rw_t.shape[0]
    bmap = lambda i: ((i * tm) // seq, 0, 0)
    pick = lambda dt: jax.ShapeDtypeStruct((TOP_K, t), dt)
    pspec = pl.BlockSpec((TOP_K, tm), lambda i: (0, i))
    row = pl.BlockSpec((tm, d), lambda i: (i, 0))
    vec = pl.BlockSpec((1, d), lambda i: (0, 0))
    per_batch = pl.BlockSpec((None, 1, d), bmap)
    return pl.pallas_call(
        _router_kernel,
        out_shape=[jax.ShapeDtypeStruct((t, d), F32), jax.ShapeDtypeStruct((t, d // 2), jnp.uint32),
                   pick(jnp.int32), pick(jnp.int32), pick(F32), jax.ShapeDtypeStruct((ne, LANES), F32)],
        grid=(t // tm,),
        in_specs=[row, row, vec, per_batch, vec, per_batch, per_batch,
                  pl.BlockSpec((ne, d), lambda i: (0, 0)), pl.BlockSpec((ne, 1), lambda i: (0, 0))],
        out_specs=[row, pl.BlockSpec((tm, d // 2), lambda i: (i, 0)), pspec, pspec, pspec,
                   pl.BlockSpec((ne, LANES), lambda i: (0, 0))],
        scratch_shapes=[pltpu.VMEM((ne, LANES), F32)],
        compiler_params=_cparams("arbitrary"),
        name="router",
    )(x2, y, gain_y, gate_y, gain, shift, scale, rw_t, rb)


def _row_copy(src, src_row, dst, dst_row, sem):
    return pltpu.make_async_copy(src.at[pl.ds(src_row, 1), :], dst.at[pl.ds(dst_row, 1), :], sem)


def _dispatch_kernel(rows_ref, u_ref, x_hbm, sem, *, n_pick):
    td = u_ref.shape[0]

    def body(tok, carry):
        for k in range(n_pick):
            _row_copy(u_ref, tok, x_hbm, rows_ref[0, k * td + tok], sem).start()
        return carry

    lax.fori_loop(0, td, body, 0)
    for k in range(n_pick):
        pltpu.make_async_copy(u_ref, x_hbm.at[pl.ds(0, td), :], sem).wait()


def _dispatch(u, rows_tiled, n_rows, td):
    t, d = u.shape
    n_pick = rows_tiled.shape[2] // td
    return pl.pallas_call(
        functools.partial(_dispatch_kernel, n_pick=n_pick),
        out_shape=jax.ShapeDtypeStruct((n_rows, d), u.dtype),
        grid=(t // td,),
        in_specs=[
            pl.BlockSpec((None, 1, n_pick * td), lambda i: (i, 0, 0), memory_space=pltpu.SMEM),
            pl.BlockSpec((td, d), lambda i: (i, 0)),
        ],
        out_specs=pl.BlockSpec(memory_space=pl.ANY),
        scratch_shapes=[pltpu.SemaphoreType.DMA(())],
        compiler_params=_cparams("arbitrary"),
        name="moe_dispatch",
    )(rows_tiled, u)


def _experts_kernel(tile_ref, exp_ref, lo_ref, hi_ref, first_ref, new_ref, slot_ref, next_ref, n_ref,
                    x_ref, wgu_hbm, wd_hbm, y_ref, wgu_buf, wd_buf, sems, *, layer):
    p = pl.program_id(0)

    def weight_copies(e, s):
        return (pltpu.make_async_copy(wgu_hbm.at[layer, e], wgu_buf.at[s], sems.at[0, s]),
                pltpu.make_async_copy(wd_hbm.at[layer, e], wd_buf.at[s], sems.at[1, s]))

    @pl.when(p < n_ref[0])
    def _():
        s = slot_ref[p]

        @pl.when(new_ref[p] == 1)
        def _():
            @pl.when(p == 0)
            def _():
                for c in weight_copies(exp_ref[p], s):
                    c.start()

            for c in weight_copies(exp_ref[p], s):
                c.wait()

            @pl.when(next_ref[p] >= 0)
            def _():
                for c in weight_copies(next_ref[p], 1 - s):
                    c.start()

        wgu_ref = wgu_buf.at[s]
        wd_ref = wd_buf.at[s]
        ff = wd_ref.shape[0]
        half = x_ref.shape[1]
        x_lo, x_hi = _unpack_halves(x_ref[...])
        h = (_dot(x_lo.astype(BF16), wgu_ref[0:half, :].astype(BF16))
             + _dot(x_hi.astype(BF16), wgu_ref[half:, :].astype(BF16)))
        hg = h[:, :ff]
        hu = h[:, ff:]
        act = (hg * _sigmoid(hg) * hu).astype(BF16)
        row = lax.broadcasted_iota(jnp.int32, (x_ref.shape[0], 1), 0)
        mine = (row >= lo_ref[p]) & (row < hi_ref[p])

        @pl.when(first_ref[p] == 1)
        def _():
            y_ref[...] = _pack_halves(_dot(act, wd_ref[...].astype(BF16)))

        @pl.when(first_ref[p] == 0)
        def _():
            y_ref[...] = jnp.where(mine, _pack_halves(_dot(act, wd_ref[...].astype(BF16))), y_ref[...])


def _experts(x_sorted, work, w_gu, w_down, layer, tm):
    n_rows, half = x_sorted.shape
    _, ne1, d, ff2 = w_gu.shape
    ff = ff2 // 2
    n_work_max = n_rows // tm + ne1 - 1
    rows = pl.BlockSpec((tm, half), lambda p, tile, *_: (tile[p], 0))
    return pl.pallas_call(
        functools.partial(_experts_kernel, layer=layer),
        out_shape=jax.ShapeDtypeStruct((n_rows, half), jnp.uint32),
        grid_spec=pltpu.PrefetchScalarGridSpec(
            num_scalar_prefetch=len(work),
            grid=(n_work_max,),
            in_specs=[rows, pl.BlockSpec(memory_space=pl.ANY), pl.BlockSpec(memory_space=pl.ANY)],
            out_specs=rows,
            scratch_shapes=[pltpu.VMEM((2, d, ff2), F32), pltpu.VMEM((2, ff, d), F32),
                            pltpu.SemaphoreType.DMA((2, 2))],
        ),
        compiler_params=_cparams("arbitrary"),
        name="moe_experts",
    )(*work, x_sorted, w_gu, w_down)


def _expert_work_list(counts, n_rows, tm):
    ne1 = counts.shape[0]
    n_work_max = n_rows // tm + ne1 - 1
    ends = jnp.cumsum(counts)
    starts = ends - counts
    first_tile = starts // tm
    n_tile = jnp.where(counts > 0, (ends - 1) // tm - first_tile + 1, 0)
    work_end = jnp.cumsum(n_tile)
    n_work = work_end[-1:]
    p = jnp.minimum(jnp.arange(n_work_max, dtype=jnp.int32), n_work - 1)
    exp = jnp.sum((work_end[None, :] <= p[:, None]).astype(jnp.int32), axis=1)
    tile = first_tile[exp] + p - (work_end - n_tile)[exp]
    lo = jnp.maximum(starts[exp] - tile * tm, 0)
    hi = jnp.minimum(ends[exp] - tile * tm, tm)
    one = jnp.ones((1,), jnp.int32)
    first = jnp.concatenate([one, (tile[1:] != tile[:-1]).astype(jnp.int32)])
    new = jnp.concatenate([one, (exp[1:] != exp[:-1]).astype(jnp.int32)])
    slot = (jnp.cumsum(new) - 1) % 2
    run_end = work_end[exp]
    nxt = jnp.where(run_end < n_work, exp[jnp.minimum(run_end, n_work_max - 1)], -1)
    as_i32 = lambda a: a.astype(jnp.int32)
    return tuple(map(as_i32, (tile, exp, lo, hi, first, new, slot, nxt, n_work))), starts


def _combine_kernel(rows_ref, nrows_ref, g_ref, x_ref, ys_ref, gain_ref, gate_ref, y_hbm, o_ref, buf, sems,
                    *, n_pick):
    i = pl.program_id(0)
    tt = x_ref.shape[0]
    slot = i % 2

    def gather(table_ref, dst_slot):
        def body(tok, carry):
            for k in range(n_pick):
                _row_copy(y_hbm, table_ref[0, k * tt + tok], buf.at[dst_slot, k], tok, sems.at[dst_slot]).start()
            return carry

        lax.fori_loop(0, tt, body, 0)

    @pl.when(i == 0)
    def _():
        gather(rows_ref, 0)

    @pl.when(i + 1 < pl.num_programs(0))
    def _():
        gather(nrows_ref, 1 - slot)

    for k in range(n_pick):
        pltpu.make_async_copy(y_hbm.at[pl.ds(0, tt), :], buf.at[slot, k], sems.at[slot]).wait()
    g = g_ref[...]
    acc_lo, acc_hi = _unpack_halves(ys_ref[...])
    for k in range(n_pick):
        lo, hi = _unpack_halves(buf[slot, k])
        acc_lo = acc_lo + lo * g[:, k:k + 1]
        acc_hi = acc_hi + hi * g[:, k:k + 1]
    acc = jnp.concatenate([acc_lo, acc_hi], axis=1)
    yn = acc * lax.rsqrt(jnp.mean(acc * acc, axis=-1, keepdims=True) + RMS_EPS) * gain_ref[...]
    o_ref[...] = x_ref[...] + gate_ref[...] * yn


def _combine(y_routed, y_shared, rows_tiled, gates, x2, gain, gate, seq, tt):
    t, d = x2.shape
    n_pick = rows_tiled.shape[2] // tt
    n_tiles = t // tt
    bmap = lambda i: ((i * tt) // seq, 0, 0)
    row = pl.BlockSpec((tt, d), lambda i: (i, 0))
    table = lambda imap: pl.BlockSpec((None, 1, n_pick * tt), imap, memory_space=pltpu.SMEM)
    return pl.pallas_call(
        functools.partial(_combine_kernel, n_pick=n_pick),
        out_shape=jax.ShapeDtypeStruct((t, d), F32),
        grid=(n_tiles,),
        in_specs=[
            table(lambda i: (i, 0, 0)),
            table(lambda i: (jnp.minimum(i + 1, n_tiles - 1), 0, 0)),
            pl.BlockSpec((tt, gates.shape[1]), lambda i: (i, 0)),
            row,
            pl.BlockSpec((tt, d // 2), lambda i: (i, 0)),
            pl.BlockSpec((1, d), lambda i: (0, 0)),
            pl.BlockSpec((None, 1, d), bmap),
            pl.BlockSpec(memory_space=pl.ANY),
        ],
        out_specs=row,
        scratch_shapes=[pltpu.VMEM((2, n_pick, tt, d // 2), jnp.uint32), pltpu.SemaphoreType.DMA((2,))],
        compiler_params=_cparams("arbitrary"),
        name="moe_combine",
    )(rows_tiled, rows_tiled, gates, x2, y_shared, gain, gate, y_routed)


def _tile_rows(rows, tile):
    n_pick, t = rows.shape
    return rows.reshape(n_pick, t // tile, tile).transpose(1, 0, 2).reshape(t // tile, 1, n_pick * tile)


def _moe_block(x2, y_mix, gain_mix, gate_mix, gain_in, shift, scale, rw_t, rb, w_gu, w_down, ws_gu, ws_down, layer,
               gain_out, gate, seq, tm=256, td=128, tt=64):
    t, d = x2.shape
    ne = rw_t.shape[0]
    x2, u, idx, pos, gate_k, counts = _router(x2, y_mix, gain_mix, gate_mix, gain_in, shift, scale, rw_t, rb, seq)
    work, starts = _expert_work_list(counts[:, 0].astype(jnp.int32), TOP_K * t, tm)
    hit = idx[:, :, None] == jnp.arange(ne, dtype=jnp.int32)[None, None, :]
    rows = jnp.sum(jnp.where(hit, starts[None, None, :], 0), axis=-1) + pos
    x_sorted = _dispatch(u, _tile_rows(rows, td), TOP_K * t, td)
    y_routed = _experts(x_sorted, work, w_gu, w_down, layer, tm)
    shared_work, _ = _expert_work_list(jnp.full((1,), t, jnp.int32), t, tm)
    y_shared = _experts(u, shared_work, ws_gu, ws_down, layer, tm)
    return _combine(y_routed, y_shared, _tile_rows(rows, tt), gate_k.T, x2, gain_out, gate, seq, tt)


def kernel(x, c, rel_bias, ada_w, ada_b, norm_gains, w_in, shift_mu, rwkv_w0, rwkv_w2, rwkv_a0, rwkv_a2, rwkv_g2,
           rwkv_k_k, rwkv_k_a, rwkv_r_k, rwkv_ln_w, rwkv_ln_b, attn_sinks, w_out, router_w, router_bias,
           expert_w_gu, expert_w_down, shared_w_gu, shared_w_down):
    batch, seq, d = x.shape
    depth = ada_w.shape[0]
    t = batch * seq
    rw = rwkv_w0.shape[1]
    dl, al, gl = rwkv_w2.shape[1], rwkv_a2.shape[1], rwkv_g2.shape[1]
    aw = w_out.shape[1] - rw
    n_q = aw // HEAD_DIM
    n_kv = (w_in.shape[2] - (3 * rw + dl + al + gl) - aw) // (2 * HEAD_DIM)
    q_per_kv = n_q // n_kv
    q_off = 3 * rw + dl + al + gl
    assert q_off % HEAD_DIM == 0, "attention heads must start at a multiple of the head width"

    c_pad = jnp.pad(c, ((0, 8 - batch % 8 if batch % 8 else 0), (0, 0)))
    ada = _ada(c_pad, ada_w, ada_b)[:, :batch]
    bias = _bias_table(rel_bias)
    w_in_t = jnp.swapaxes(w_in, 1, 2)
    w_gu_b = expert_w_gu
    ws_gu_b = shared_w_gu[:, None]

    x2 = x.reshape(t, d)
    for l in range(depth):
        sh_m, sc_m, gt_m, sh_f, sc_f, gt_f = [a.reshape(batch, 1, d) for a in jnp.split(ada[l], 6, axis=-1)]
        gains = norm_gains[l].reshape(4, 1, d)

        proj = _in_proj(_norm_mod_bf16(x2, gains[0], sh_m, sc_m, seq), w_in_t, l)

        r, k, v, a, g, lw = _rwkv_prep(proj, shift_mu[l][None, :], rwkv_w0[l][None, :], rwkv_w2[l],
                                       rwkv_a0[l][None, :], rwkv_a2[l], rwkv_g2[l], seq)
        y_r = _rwkv_chunk(r, k, v, a, g, lw, rwkv_k_k[l][None, :], rwkv_k_a[l][None, :],
                          rwkv_r_k[l].reshape(1, rw), rwkv_ln_w[l][None, :], rwkv_ln_b[l][None, :], batch, seq)
        y_a = _swa(proj, attn_sinks[l], bias, batch, seq, q_off, n_kv, q_per_kv)
        y_mix = _out_proj(y_r, y_a, w_out, l)
        x2 = _moe_block(x2, y_mix, gains[1], gt_m, gains[2], sh_f, sc_f, router_w[l].T, router_bias[l][:, None],
                        w_gu_b, expert_w_down, ws_gu_b, shared_w_down[:, None], l, gains[3], gt_f, seq)
    return x2.reshape(batch, seq, d)
```

```python
import functools
import math

import numpy as np
import jax
import jax.numpy as jnp
from jax import lax
from jax.experimental import pallas as pl
from jax.experimental.pallas import tpu as pltpu

F32 = jnp.float32
BF16 = jnp.bfloat16

HEAD_DIM = 64
LANES = 128
WINDOW = 128
N_BUCKETS = 32
MAX_DISTANCE = 128
N_EXPERT_GROUPS = 8
TOPK_GROUPS = 4
TOP_K = 8
ROUTED_SCALE = 2.5
RMS_EPS = 1e-6
GN_EPS = 64e-5
CHUNK = 64
HEADS_PER_GROUP = 4
GROUP_LANES = HEADS_PER_GROUP * HEAD_DIM
VMEM_LIMIT = 60 * 1024 * 1024

HIGHEST = lax.Precision.HIGHEST


def _round_up(n, m):
    return (n + m - 1) // m * m


def _cparams(*sem):
    return pltpu.CompilerParams(dimension_semantics=sem, vmem_limit_bytes=VMEM_LIMIT)


def _dot(a, b):
    return jnp.dot(a, b, preferred_element_type=F32)


def _dot_nt(a, b):
    return lax.dot_general(a, b, (((1,), (1,)), ((), ())), preferred_element_type=F32)


def _dot_tn(a, b):
    return lax.dot_general(a, b, (((0,), (0,)), ((), ())), preferred_element_type=F32)


def _bdot(a, b):
    return lax.dot_general(a, b, (((2,), (1,)), ((0,), (0,))), preferred_element_type=F32)


def _bdot_nt(a, b):
    return lax.dot_general(a, b, (((2,), (2,)), ((0,), (0,))), preferred_element_type=F32)


def _sigmoid(x):
    return 1.0 / (1.0 + jnp.exp(-x))


def _pack_halves(x):
    half = x.shape[1] // 2
    lo = lax.bitcast_convert_type(x[:, :half].astype(BF16).astype(F32), jnp.uint32)
    hi = lax.bitcast_convert_type(x[:, half:].astype(BF16).astype(F32), jnp.uint32)
    return (lo >> 16) | (hi & jnp.uint32(0xFFFF0000))


def _unpack_halves(p):
    lo = lax.bitcast_convert_type(p << 16, F32)
    hi = lax.bitcast_convert_type(p & jnp.uint32(0xFFFF0000), F32)
    return lo, hi


def _norm_mod(x, gain, shift, scale):
    y = x * lax.rsqrt(jnp.mean(x * x, axis=-1, keepdims=True) + RMS_EPS) * gain
    return y * (1.0 + scale) + shift


def _ada_kernel(c_ref, w_ref, b_ref, o_ref):
    c = c_ref[...]
    cond = c * _sigmoid(c)
    o_ref[...] = _dot(cond.astype(BF16), w_ref[...].astype(BF16)) + b_ref[...]


def _ada(c_pad, ada_w, ada_b):
    nl, d, n = ada_w.shape
    tn = 512
    return pl.pallas_call(
        _ada_kernel,
        out_shape=jax.ShapeDtypeStruct((nl, c_pad.shape[0], n), F32),
        grid=(nl, n // tn),
        in_specs=[
            pl.BlockSpec((c_pad.shape[0], d), lambda l, j: (0, 0)),
            pl.BlockSpec((None, d, tn), lambda l, j: (l, 0, j)),
            pl.BlockSpec((None, 1, tn), lambda l, j: (l, 0, j)),
        ],
        out_specs=pl.BlockSpec((None, c_pad.shape[0], tn), lambda l, j: (l, 0, j)),
        compiler_params=_cparams("parallel", "parallel"),
        name="ada",
    )(c_pad, ada_w, ada_b.reshape(nl, 1, n))


def _bias_kernel(rbt_ref, oh_ref, o_ref):
    o_ref[...] = jnp.dot(rbt_ref[...], oh_ref[...], preferred_element_type=F32, precision=HIGHEST)


def _bucket_onehot():
    qi = np.arange(WINDOW)[:, None]
    kj = np.arange(2 * WINDOW)[None, :]
    dist = qi + WINDOW - kj
    max_exact = N_BUCKETS // 2
    d = np.maximum(dist, 0)
    far = max_exact + (np.log(np.maximum(d, max_exact).astype(np.float32) / np.float32(max_exact))
                       / np.float32(math.log(MAX_DISTANCE / max_exact))
                       * np.float32(N_BUCKETS - max_exact)).astype(np.int32)
    bucket = np.where(d < max_exact, d, np.minimum(far, N_BUCKETS - 1)).reshape(-1)
    return (np.arange(N_BUCKETS)[:, None] == bucket[None, :]).astype(np.float32)


def _bias_table(rel_bias):
    nb, nh = rel_bias.shape
    onehot = jnp.asarray(_bucket_onehot())
    out = pl.pallas_call(
        _bias_kernel,
        out_shape=jax.ShapeDtypeStruct((nh, WINDOW * 2 * WINDOW), F32),
        name="bias_table",
        compiler_params=pltpu.CompilerParams(vmem_limit_bytes=VMEM_LIMIT),
    )(rel_bias.T, onehot)
    return out.reshape(nh, WINDOW, 2 * WINDOW)


def _norm_mod_kernel(x_ref, gain_ref, sh_ref, sc_ref, u_ref):
    u_ref[...] = _norm_mod(x_ref[...], gain_ref[...], sh_ref[...], sc_ref[...]).astype(u_ref.dtype)


def _norm_mod_bf16(x2, gain, shift, scale, seq, tm=512):
    t, d = x2.shape
    bmap = lambda i: ((i * tm) // seq, 0, 0)
    row = pl.BlockSpec((tm, d), lambda i: (i, 0))
    return pl.pallas_call(
        _norm_mod_kernel,
        out_shape=jax.ShapeDtypeStruct((t, d), BF16),
        grid=(t // tm,),
        in_specs=[row, pl.BlockSpec((1, d), lambda i: (0, 0)), pl.BlockSpec((None, 1, d), bmap),
                  pl.BlockSpec((None, 1, d), bmap)],
        out_specs=row,
        compiler_params=_cparams("parallel"),
        name="norm_mod",
    )(x2, gain, shift, scale)


def _in_proj_kernel(u_ref, wt_ref, o_ref, wb_ref):
    @pl.when(pl.program_id(1) == 0)
    def _():
        wb_ref[...] = wt_ref[...].astype(BF16)

    o_ref[...] = _dot_nt(u_ref[...], wb_ref[...])


def _in_proj(u, wt_all, layer, tm=1024, tn=512):
    t, d = u.shape
    n = wt_all.shape[1]
    tm = min(tm, t)
    return pl.pallas_call(
        _in_proj_kernel,
        out_shape=jax.ShapeDtypeStruct((t, n), F32),
        grid=(pl.cdiv(n, tn), t // tm),
        in_specs=[
            pl.BlockSpec((tm, d), lambda j, i: (i, 0)),
            pl.BlockSpec((None, tn, d), lambda j, i: (layer, j, 0)),
        ],
        out_specs=pl.BlockSpec((tm, tn), lambda j, i: (i, j)),
        scratch_shapes=[pltpu.VMEM((tn, d), BF16)],
        compiler_params=_cparams("parallel", "arbitrary"),
        name="in_proj",
    )(u, wt_all)


def _lane_window(start, width):
    lo = start // LANES * LANES
    return lo, _round_up(start + width, LANES), start - lo


def _window_rows(w, start):
    lo, hi, off = _lane_window(start, w.shape[0])
    return jnp.pad(w, ((off, hi - lo - off - w.shape[0]), (0, 0))).astype(BF16)


def _rwkv_prep_kernel(rw, xw_win, xa_win, xg_win, p_ref, prev_ref, mu_ref, w0_ref, w2_ref, a0_ref, a2_ref, g2_ref,
                      r_ref, k_ref, v_ref, a_ref, g_ref, lw_ref, *, blocks_per_seq):
    i = pl.program_id(0)
    cur = p_ref[...]
    first = (i % blocks_per_seq) == 0
    prev_row = jnp.where(first, 0.0, prev_ref[7:8, :])
    rolled = pltpu.roll(cur, 1, axis=0)
    row = lax.broadcasted_iota(jnp.int32, cur.shape, 0)
    shifted = jnp.where(row == 0, prev_row, rolled)
    x = cur + (shifted - cur) * mu_ref[...]
    r_ref[...] = x[:, 0:rw].astype(r_ref.dtype)
    k_ref[...] = x[:, rw:2 * rw].astype(k_ref.dtype)
    v_ref[...] = x[:, 2 * rw:3 * rw].astype(v_ref.dtype)
    xw = x[:, xw_win[0]:xw_win[1]]
    xa = x[:, xa_win[0]:xa_win[1]]
    xg = x[:, xg_win[0]:xg_win[1]]
    z = w0_ref[...] + _dot(jnp.tanh(xw).astype(BF16), w2_ref[...])
    lw_ref[...] = -math.exp(-0.5) * _sigmoid(z)
    a_ref[...] = _sigmoid(a0_ref[...] + _dot(xa.astype(BF16), a2_ref[...])).astype(a_ref.dtype)
    g_ref[...] = _dot(_sigmoid(xg).astype(BF16), g2_ref[...]).astype(g_ref.dtype)


def _rwkv_prep(proj, mu, w0, w2, a0, a2, g2, seq, ts=256):
    t = proj.shape[0]
    rw = w0.shape[1]
    dl, al, gl = w2.shape[0], a2.shape[0], g2.shape[0]
    starts = (3 * rw, 3 * rw + dl, 3 * rw + dl + al)
    wins = [_lane_window(s, w)[:2] for s, w in zip(starts, (dl, al, gl))]
    wblock = wins[2][1]
    w2p, a2p, g2p = (_window_rows(w, s) for w, s in zip((w2, a2, g2), starts))
    mu_p = jnp.pad(mu, ((0, 0), (0, wblock - mu.shape[1])))
    full = lambda i: (0, 0)
    outs = [jax.ShapeDtypeStruct((t, rw), BF16)] * 5 + [jax.ShapeDtypeStruct((t, rw), F32)]
    ospec = pl.BlockSpec((ts, rw), lambda i: (i, 0))
    return pl.pallas_call(
        functools.partial(_rwkv_prep_kernel, rw, *wins, blocks_per_seq=seq // ts),
        out_shape=outs,
        grid=(t // ts,),
        in_specs=[
            pl.BlockSpec((ts, wblock), lambda i: (i, 0)),
            pl.BlockSpec((8, wblock), lambda i: (jnp.maximum(i * (ts // 8) - 1, 0), 0)),
            pl.BlockSpec((1, wblock), full),
            pl.BlockSpec((1, rw), full),
            pl.BlockSpec(w2p.shape, full),
            pl.BlockSpec((1, rw), full),
            pl.BlockSpec(a2p.shape, full),
            pl.BlockSpec(g2p.shape, full),
        ],
        out_specs=[ospec] * 6,
        compiler_params=_cparams("parallel"),
        name="rwkv_prep",
    )(proj, proj, mu_p, w0, w2p, a0, a2p, g2p)


CHUNK_LEVELS = int(math.log2(CHUNK))


def _chunk_constants():
    n = GROUP_LANES
    i = np.arange(n)[:, None]
    j = np.arange(n)[None, :]
    same_head = (i // CHUNK) == (j // CHUNK)
    code = np.zeros((n, n), np.int32)
    for lvl in range(CHUNK_LEVELS, 0, -1):
        code = np.where(same_head & (i > j) & ((i >> lvl) == (j >> lvl)), lvl, code)
    planes = [code == lvl for lvl in range(1, CHUNK_LEVELS + 1)] + [code > 0, same_head]
    return np.stack(planes).astype(np.float32)


def _seg_sum(x, ones_bd):
    return _dot(x.astype(BF16), ones_bd)


def _split3(x):
    hi = x.astype(BF16)
    r1 = x - hi.astype(F32)
    mid = r1.astype(BF16)
    return hi, mid, (r1 - mid.astype(F32)).astype(BF16)


def _rwkv_chunk_kernel(r_ref, k_ref, v_ref, a_ref, g_ref, lw_ref,
                       kk_ref, ka_ref, rk_ref, lnw_ref, lnb_ref, planes_ref,
                       o_ref, s_ref):
    @pl.when(pl.program_id(2) == 0)
    def _():
        s_ref[...] = jnp.zeros_like(s_ref)

    n = GROUP_LANES
    lvl_mask = [planes_ref[l] for l in range(CHUNK_LEVELS)]
    low_mask = planes_ref[CHUNK_LEVELS]
    bd = planes_ref[CHUNK_LEVELS + 1]
    tb = r_ref.shape[0]
    nch = tb // CHUNK
    tri_incl = (lax.broadcasted_iota(jnp.int32, (1, CHUNK, n), 1)
                >= lax.broadcasted_iota(jnp.int32, (1, CHUNK, n), 2) % CHUNK)
    ltri = jnp.where(lax.broadcasted_iota(jnp.int32, (nch, CHUNK, CHUNK), 1)
                     >= lax.broadcasted_iota(jnp.int32, (nch, CHUNK, CHUNK), 2), 1.0, 0.0).astype(BF16)
    eye = jnp.where(lax.broadcasted_iota(jnp.int32, (1, n, n), 1)
                    == lax.broadcasted_iota(jnp.int32, (1, n, n), 2), 1.0, 0.0).astype(BF16)

    def chunks(x):
        return x.reshape(nch, CHUNK, n)

    def tile_heads(x):
        return jnp.concatenate([x] * HEADS_PER_GROUP, axis=1)

    def stack(x):
        return tile_heads(x.astype(BF16)) * bd[None]

    r = r_ref[...].astype(F32)
    k = k_ref[...].astype(F32)
    v = v_ref[...].astype(F32)
    a = a_ref[...].astype(F32)
    lw = lw_ref[...]
    kkp = k * kk_ref[...]
    kk = kkp / jnp.maximum(jnp.sqrt(_seg_sum(kkp * kkp, bd)), 1e-12)
    k2 = k * (1.0 + (a - 1.0) * ka_ref[...])
    kb = kk * a
    bonus = _seg_sum(r * k2 * rk_ref[...], bd) * v

    cum = sum(_bdot(ltri, chunks(piece)) for piece in _split3(lw))
    cum_end = cum[:, CHUNK - 1:CHUNK, :]
    e_in = jnp.exp(cum)
    e_ex = jnp.exp(cum - chunks(lw))
    e_neg = jnp.exp(-cum)
    e_dec = jnp.exp(cum_end - cum)
    decay = jnp.exp(cum_end)
    r_t = chunks(r) * e_in
    a_t = -chunks(kk) * e_ex
    kb3 = chunks(kb)
    k23 = chunks(k2)

    ar = jnp.concatenate([a_t, r_t], axis=1).astype(BF16)
    v_s = stack(chunks(v))
    ab_rb = _bdot_nt(ar, stack(kb3 * e_neg))
    ak_rk = _bdot_nt(ar, stack(k23 * e_neg))
    rb_w = jnp.where(tri_incl, ab_rb[:, CHUNK:], 0.0).astype(BF16)
    rk_w = jnp.where(tri_incl, ak_rk[:, CHUNK:], 0.0).astype(BF16)
    amat = tile_heads(ab_rb[:, :CHUNK].astype(BF16))
    akmat = tile_heads(ak_rk[:, :CHUNK].astype(BF16)) * low_mask[None]

    def lower_rows(x, s):
        m = x.shape[2]
        return x.reshape(nch * n // s, 2, s // 2, m)[:, 1].reshape(nch, n // 2, m)

    def as_lower_rows(x, s):
        m = x.shape[2]
        x4 = x.reshape(nch * n // s, 1, s // 2, m)
        return jnp.concatenate([jnp.zeros_like(x4), x4], axis=1).reshape(nch, n, m)

    tinv = eye + amat * lvl_mask[0][None]
    for lvl in range(1, CHUNK_LEVELS):
        off = amat * lvl_mask[lvl][None]
        s = 2 ** (lvl + 1)
        if s // 2 >= 16:
            m1 = as_lower_rows(_bdot(lower_rows(off, s), tinv).astype(BF16), s)
            tinv = tinv + as_lower_rows(_bdot(lower_rows(tinv, s), m1).astype(BF16), s)
        else:
            tinv = tinv + _bdot(tinv, _bdot(off, tinv).astype(BF16)).astype(BF16)

    w_s = _bdot(tinv, stack(a_t)).astype(BF16)
    z_s = _bdot(tinv, _bdot(akmat, v_s).astype(BF16)).astype(BF16)
    bh_s = stack(kb3 * e_dec)
    kh_s = stack(k23 * e_dec)
    r_eff = (r_t + _bdot(rb_w, w_s)).astype(BF16)
    y0 = _bdot(rb_w, z_s) + _bdot(rk_w, v_s)

    s = s_ref[...]
    ys = []
    for ci in range(nch):
        p_mat = _dot_tn(w_s[ci], bh_s[ci]).astype(BF16)
        q_mat = _dot_tn(z_s[ci], bh_s[ci]) + _dot_tn(v_s[ci], kh_s[ci])
        s_b = s.astype(BF16)
        ys.append(_dot_nt(r_eff[ci], s_b) + y0[ci])
        s = s * decay[ci] + _dot(s_b, p_mat) + q_mat
    s_ref[...] = s

    y = jnp.concatenate(ys, axis=0)
    inv_n = 1.0 / HEAD_DIM
    mean = _seg_sum(y, bd) * inv_n
    dlt = y - mean
    var = _seg_sum(dlt * dlt, bd) * inv_n
    yn = dlt * lax.rsqrt(var + GN_EPS) * lnw_ref[...] + lnb_ref[...]
    o_ref[...] = ((yn + bonus) * g_ref[...].astype(F32)).astype(o_ref.dtype)


def _rwkv_chunk(r, k, v, a, g, lw, k_k, k_a, r_k, ln_w, ln_b, batch, seq, tb=1024):
    t, rw = r.shape
    n = GROUP_LANES
    tb = min(tb, seq)
    planes = jnp.asarray(_chunk_constants(), dtype=BF16)
    nblk = seq // tb
    tmap = lambda b, h, c: (b * nblk + c, h)
    pmap = lambda b, h, c: (0, h)
    tspec = pl.BlockSpec((tb, n), tmap)
    pspec = pl.BlockSpec((1, n), pmap)
    return pl.pallas_call(
        _rwkv_chunk_kernel,
        out_shape=jax.ShapeDtypeStruct((t, rw), BF16),
        grid=(batch, rw // n, nblk),
        in_specs=[tspec] * 6 + [pspec] * 5 + [pl.BlockSpec(planes.shape, lambda b, h, c: (0, 0, 0))],
        out_specs=pl.BlockSpec((tb, n), tmap),
        scratch_shapes=[pltpu.VMEM((n, n), F32)],
        compiler_params=_cparams("parallel", "parallel", "arbitrary"),
        name="rwkv_chunk",
    )(r, k, v, a, g, lw, k_k, k_a, r_k, ln_w, ln_b, planes)


def _swa_kernel(sink_ref, *refs, q_per_kv, q_off, k_off, v_off, n_q_tiles):
    q_refs = refs[:n_q_tiles]
    kp_ref, kc_ref, vp_ref, vc_ref, bias_ref, o_ref = refs[n_q_tiles:]
    j = pl.program_id(1)
    nblk = pl.program_id(2)
    blk = WINDOW
    upper = lax.broadcasted_iota(jnp.int32, (1, LANES), 1) >= HEAD_DIM

    def both_halves(x, start_col):
        in_upper = (start_col // HEAD_DIM) % 2 == 1
        sel = jnp.where(upper == in_upper, x, 0.0)
        return sel + pltpu.roll(sel, HEAD_DIM, axis=1)

    kw = both_halves(jnp.concatenate([kp_ref[...], kc_ref[...]], axis=0), k_off + HEAD_DIM * j).astype(BF16)
    vw = both_halves(jnp.concatenate([vp_ref[...], vc_ref[...]], axis=0), v_off + HEAD_DIM * j)
    v_half = (jnp.where(upper, 0.0, vw).astype(BF16), jnp.where(upper, vw, 0.0).astype(BF16))
    qi = lax.broadcasted_iota(jnp.int32, (blk, 2 * blk), 0)
    kj = lax.broadcasted_iota(jnp.int32, (blk, 2 * blk), 1)
    valid = (kj > qi) & (kj <= qi + blk) & ((nblk > 0) | (kj >= blk))
    scale = HEAD_DIM ** -0.5
    for p in range(q_per_kv // 2):
        acc = None
        for half in range(2):
            g = 2 * p + half
            col = q_off % LANES + HEAD_DIM * g
            q_tile = q_refs[col // LANES][...] * scale
            q_upper = (col // HEAD_DIM) % 2 == 1
            qm = jnp.where(upper == q_upper, q_tile, 0.0).astype(BF16)
            sink = sink_ref[j * q_per_kv + g]
            s = _dot_nt(qm, kw) + bias_ref[g]
            s = jnp.where(valid, s, -jnp.inf)
            m = jnp.maximum(jnp.max(s, axis=-1, keepdims=True), sink)
            pr = jnp.exp(s - m)
            denom = jnp.sum(pr, axis=-1, keepdims=True) + jnp.exp(sink - m)
            o = _dot(pr.astype(BF16), v_half[half]) / denom
            acc = o if acc is None else acc + o
        o_ref[:, p * LANES:(p + 1) * LANES] = acc.astype(o_ref.dtype)


def _swa(proj, sinks, bias, batch, seq, q_off, n_kv, q_per_kv):
    t = proj.shape[0]
    blk = WINDOW
    nb = seq // blk
    qw = q_per_kv * HEAD_DIM
    k_off = q_off + n_kv * qw
    v_off = k_off + n_kv * HEAD_DIM
    n_q_tiles = (q_off % LANES + qw + LANES - 1) // LANES
    cur = lambda b, j, n: b * nb + n
    prev = lambda b, j, n: b * nb + jnp.maximum(n - 1, 0)
    tile = lambda rowf, colf: pl.BlockSpec((blk, LANES), lambda b, j, n: (rowf(b, j, n), colf(j)))
    q_specs = [tile(cur, lambda j, i=i: q_off // LANES + (qw // LANES) * j + i) for i in range(n_q_tiles)]
    k_col = lambda j: (k_off + HEAD_DIM * j) // LANES
    v_col = lambda j: (v_off + HEAD_DIM * j) // LANES
    return pl.pallas_call(
        functools.partial(_swa_kernel, q_per_kv=q_per_kv, q_off=q_off, k_off=k_off, v_off=v_off,
                          n_q_tiles=n_q_tiles),
        out_shape=jax.ShapeDtypeStruct((t, n_kv * qw), BF16),
        grid=(batch, n_kv, nb),
        in_specs=[pl.BlockSpec(memory_space=pltpu.SMEM)] + q_specs + [
            tile(prev, k_col), tile(cur, k_col), tile(prev, v_col), tile(cur, v_col),
            pl.BlockSpec((q_per_kv, blk, 2 * blk), lambda b, j, n: (j, 0, 0)),
        ],
        out_specs=pl.BlockSpec((blk, qw), lambda b, j, n: (cur(b, j, n), j)),
        compiler_params=_cparams("parallel", "parallel", "parallel"),
        name="swa",
    )(sinks, *([proj] * (n_q_tiles + 4)), bias)


def _out_proj_kernel(yr_ref, ya_ref, w_ref, o_ref, wb_ref):
    @pl.when(pl.program_id(1) == 0)
    def _():
        wb_ref[...] = w_ref[...].astype(BF16)

    rw = yr_ref.shape[1]
    o_ref[...] = _dot(yr_ref[...], wb_ref[0:rw, :]) + _dot(ya_ref[...], wb_ref[rw:, :])


def _out_proj(y_r, y_a, w_all, layer, tm=1024, tn=512):
    t, rw = y_r.shape
    aw = y_a.shape[1]
    d = w_all.shape[2]
    tm = min(tm, t)
    return pl.pallas_call(
        _out_proj_kernel,
        out_shape=jax.ShapeDtypeStruct((t, d), F32),
        grid=(d // tn, t // tm),
        in_specs=[
            pl.BlockSpec((tm, rw), lambda j, i: (i, 0)),
            pl.BlockSpec((tm, aw), lambda j, i: (i, 0)),
            pl.BlockSpec((None, rw + aw, tn), lambda j, i: (layer, 0, j)),
        ],
        out_specs=pl.BlockSpec((tm, tn), lambda j, i: (i, j)),
        scratch_shapes=[pltpu.VMEM((rw + aw, tn), BF16)],
        compiler_params=_cparams("parallel", "arbitrary"),
        name="out_proj",
    )(y_r, y_a, w_all)


def _rank(vals):
    n = vals.shape[0]
    idx = lax.broadcasted_iota(jnp.int32, vals.shape, 0)
    rank = jnp.zeros(vals.shape, jnp.int32)
    for e in range(n):
        row = vals[e:e + 1, :]
        beats = (row > vals) | ((row == vals) & (e < idx))
        rank = rank + beats.astype(jnp.int32)
    return rank


def _router_kernel(x_ref, y_ref, gain_y_ref, gate_y_ref, gain_ref, sh_ref, sc_ref, rwt_ref, rb_ref,
                   xn_ref, u_ref, idx_ref, pos_ref, gate_ref, cnt_ref, carry_ref):
    @pl.when(pl.program_id(0) == 0)
    def _():
        carry_ref[...] = jnp.zeros_like(carry_ref)

    y = y_ref[...]
    yn = y * lax.rsqrt(jnp.mean(y * y, axis=-1, keepdims=True) + RMS_EPS) * gain_y_ref[...]
    xn = x_ref[...] + gate_y_ref[...] * yn
    xn_ref[...] = xn
    u = _norm_mod(xn, gain_ref[...], sh_ref[...], sc_ref[...])
    u_ref[...] = _pack_halves(u)
    logits = lax.dot_general(rwt_ref[...], u, (((1,), (1,)), ((), ())),
                             preferred_element_type=F32, precision=HIGHEST)
    scores = _sigmoid(logits)
    sel = scores + rb_ref[:, 0:1]
    ne, tm = sel.shape
    gsz = ne // N_EXPERT_GROUPS
    sub = lax.broadcasted_iota(jnp.int32, (gsz, tm), 0)
    gs_rows = []
    for gi in range(N_EXPERT_GROUPS):
        grp = sel[gi * gsz:(gi + 1) * gsz, :]
        m1 = jnp.max(grp, axis=0, keepdims=True)
        i1 = jnp.min(jnp.where(grp == m1, sub, gsz), axis=0, keepdims=True)
        m2 = jnp.max(jnp.where(sub == i1, -jnp.inf, grp), axis=0, keepdims=True)
        gs_rows.append(m1 + m2)
    gkeep = _rank(jnp.concatenate(gs_rows, axis=0)) < TOPK_GROUPS
    ekeep = jnp.concatenate(
        [jnp.broadcast_to(gkeep[gi:gi + 1, :], (gsz, tm)) for gi in range(N_EXPERT_GROUPS)], axis=0)
    rank = _rank(jnp.where(ekeep, sel, -jnp.inf))
    chosen = rank < TOP_K
    wts = jnp.where(chosen, scores, 0.0)
    gates_t = wts / jnp.sum(wts, axis=0, keepdims=True) * ROUTED_SCALE

    cf = jnp.where(chosen, 1.0, 0.0)
    before = (lax.broadcasted_iota(jnp.int32, (tm, tm), 0) < lax.broadcasted_iota(jnp.int32, (tm, tm), 1))
    carry = carry_ref[:, 0:1]
    pos = carry + _dot(cf.astype(BF16), jnp.where(before, 1.0, 0.0).astype(BF16))
    carry_ref[...] = jnp.broadcast_to(carry + jnp.sum(cf, axis=1, keepdims=True), carry_ref.shape)
    cnt_ref[...] = carry_ref[...]

    eidx = lax.broadcasted_iota(jnp.int32, (ne, tm), 0)
    rows_i, rows_p, rows_g = [], [], []
    for k in range(TOP_K):
        mk = rank == k
        rows_i.append(jnp.sum(jnp.where(mk, eidx, 0), axis=0, keepdims=True))
        rows_p.append(jnp.sum(jnp.where(mk, pos, 0.0), axis=0, keepdims=True))
        rows_g.append(jnp.sum(jnp.where(mk, gates_t, 0.0), axis=0, keepdims=True))
    idx_ref[...] = jnp.concatenate(rows_i, axis=0)
    pos_ref[...] = jnp.concatenate(rows_p, axis=0).astype(jnp.int32)
    gate_ref[...] = jnp.concatenate(rows_g, axis=0)


def _router(x2, y, gain_y, gate_y, gain, shift, scale, rw_t, rb, seq, tm=256):
    t, d = x2.shape
    ne = rw_t.shape[0]
    bmap = lambda i: ((i * tm) // seq, 0, 0)
    pick = lambda dt: jax.ShapeDtypeStruct((TOP_K, t), dt)
    pspec = pl.BlockSpec((TOP_K, tm), lambda i: (0, i))
    row = pl.BlockSpec((tm, d), lambda i: (i, 0))
    vec = pl.BlockSpec((1, d), lambda i: (0, 0))
    per_batch = pl.BlockSpec((None, 1, d), bmap)
    return pl.pallas_call(
        _router_kernel,
        out_shape=[jax.ShapeDtypeStruct((t, d), F32), jax.ShapeDtypeStruct((t, d // 2), jnp.uint32),
                   pick(jnp.int32), pick(jnp.int32), pick(F32), jax.ShapeDtypeStruct((ne, LANES), F32)],
        grid=(t // tm,),
        in_specs=[row, row, vec, per_batch, vec, per_batch, per_batch,
                  pl.BlockSpec((ne, d), lambda i: (0, 0)), pl.BlockSpec((ne, 1), lambda i: (0, 0))],
        out_specs=[row, pl.BlockSpec((tm, d // 2), lambda i: (i, 0)), pspec, pspec, pspec,
                   pl.BlockSpec((ne, LANES), lambda i: (0, 0))],
        scratch_shapes=[pltpu.VMEM((ne, LANES), F32)],
        compiler_params=_cparams("arbitrary"),
        name="router",
    )(x2, y, gain_y, gate_y, gain, shift, scale, rw_t, rb)


def _row_copy(src, src_row, dst, dst_row, sem):
    return pltpu.make_async_copy(src.at[pl.ds(src_row, 1), :], dst.at[pl.ds(dst_row, 1), :], sem)


def _dispatch_kernel(rows_ref, u_ref, x_hbm, sem, *, n_pick):
    td = u_ref.shape[0]

    def body(tok, carry):
        for k in range(n_pick):
            _row_copy(u_ref, tok, x_hbm, rows_ref[0, k * td + tok], sem).start()
        return carry

    lax.fori_loop(0, td, body, 0)
    for k in range(n_pick):
        pltpu.make_async_copy(u_ref, x_hbm.at[pl.ds(0, td), :], sem).wait()


def _dispatch(u, rows_tiled, n_rows, td):
    t, d = u.shape
    n_pick = rows_tiled.shape[2] // td
    return pl.pallas_call(
        functools.partial(_dispatch_kernel, n_pick=n_pick),
        out_shape=jax.ShapeDtypeStruct((n_rows, d), u.dtype),
        grid=(t // td,),
        in_specs=[
            pl.BlockSpec((None, 1, n_pick * td), lambda i: (i, 0, 0), memory_space=pltpu.SMEM),
            pl.BlockSpec((td, d), lambda i: (i, 0)),
        ],
        out_specs=pl.BlockSpec(memory_space=pl.ANY),
        scratch_shapes=[pltpu.SemaphoreType.DMA(())],
        compiler_params=_cparams("arbitrary"),
        name="moe_dispatch",
    )(rows_tiled, u)


def _experts_kernel(tile_ref, exp_ref, lo_ref, hi_ref, first_ref, new_ref, slot_ref, next_ref, n_ref,
                    x_ref, wgu_hbm, wd_hbm, y_ref, wgu_buf, wd_buf, sems, *, layer):
    p = pl.program_id(0)

    def weight_copies(e, s):
        return (pltpu.make_async_copy(wgu_hbm.at[layer, e], wgu_buf.at[s], sems.at[0, s]),
                pltpu.make_async_copy(wd_hbm.at[layer, e], wd_buf.at[s], sems.at[1, s]))

    @pl.when(p < n_ref[0])
    def _():
        s = slot_ref[p]

        @pl.when(new_ref[p] == 1)
        def _():
            @pl.when(p == 0)
            def _():
                for c in weight_copies(exp_ref[p], s):
                    c.start()

            for c in weight_copies(exp_ref[p], s):
                c.wait()

            @pl.when(next_ref[p] >= 0)
            def _():
                for c in weight_copies(next_ref[p], 1 - s):
                    c.start()

        wgu_ref = wgu_buf.at[s]
        wd_ref = wd_buf.at[s]
        ff = wd_ref.shape[0]
        half = x_ref.shape[1]
        x_lo, x_hi = _unpack_halves(x_ref[...])
        h = (_dot(x_lo.astype(BF16), wgu_ref[0:half, :].astype(BF16))
             + _dot(x_hi.astype(BF16), wgu_ref[half:, :].astype(BF16)))
        hg = h[:, :ff]
        hu = h[:, ff:]
        act = (hg * _sigmoid(hg) * hu).astype(BF16)
        row = lax.broadcasted_iota(jnp.int32, (x_ref.shape[0], 1), 0)
        mine = (row >= lo_ref[p]) & (row < hi_ref[p])

        @pl.when(first_ref[p] == 1)
        def _():
            y_ref[...] = _pack_halves(_dot(act, wd_ref[...].astype(BF16)))

        @pl.when(first_ref[p] == 0)
        def _():
            y_ref[...] = jnp.where(mine, _pack_halves(_dot(act, wd_ref[...].astype(BF16))), y_ref[...])


def _experts(x_sorted, work, w_gu, w_down, layer, tm):
    n_rows, half = x_sorted.shape
    _, ne1, d, ff2 = w_gu.shape
    ff = ff2 // 2
    n_work_max = n_rows // tm + ne1 - 1
    rows = pl.BlockSpec((tm, half), lambda p, tile, *_: (tile[p], 0))
    return pl.pallas_call(
        functools.partial(_experts_kernel, layer=layer),
        out_shape=jax.ShapeDtypeStruct((n_rows, half), jnp.uint32),
        grid_spec=pltpu.PrefetchScalarGridSpec(
            num_scalar_prefetch=len(work),
            grid=(n_work_max,),
            in_specs=[rows, pl.BlockSpec(memory_space=pl.ANY), pl.BlockSpec(memory_space=pl.ANY)],
            out_specs=rows,
            scratch_shapes=[pltpu.VMEM((2, d, ff2), F32), pltpu.VMEM((2, ff, d), F32),
                            pltpu.SemaphoreType.DMA((2, 2))],
        ),
        compiler_params=_cparams("arbitrary"),
        name="moe_experts",
    )(*work, x_sorted, w_gu, w_down)


def _expert_work_list(counts, n_rows, tm):
    ne1 = counts.shape[0]
    n_work_max = n_rows // tm + ne1 - 1
    ends = jnp.cumsum(counts)
    starts = ends - counts
    first_tile = starts // tm
    n_tile = jnp.where(counts > 0, (ends - 1) // tm - first_tile + 1, 0)
    work_end = jnp.cumsum(n_tile)
    n_work = work_end[-1:]
    p = jnp.minimum(jnp.arange(n_work_max, dtype=jnp.int32), n_work - 1)
    exp = jnp.sum((work_end[None, :] <= p[:, None]).astype(jnp.int32), axis=1)
    tile = first_tile[exp] + p - (work_end - n_tile)[exp]
    lo = jnp.maximum(starts[exp] - tile * tm, 0)
    hi = jnp.minimum(ends[exp] - tile * tm, tm)
    one = jnp.ones((1,), jnp.int32)
    first = jnp.concatenate([one, (tile[1:] != tile[:-1]).astype(jnp.int32)])
    new = jnp.concatenate([one, (exp[1:] != exp[:-1]).astype(jnp.int32)])
    slot = (jnp.cumsum(new) - 1) % 2
    run_end = work_end[exp]
    nxt = jnp.where(run_end < n_work, exp[jnp.minimum(run_end, n_work_max - 1)], -1)
    as_i32 = lambda a: a.astype(jnp.int32)
    return tuple(map(as_i32, (tile, exp, lo, hi, first, new, slot, nxt, n_work))), starts


def _combine_kernel(rows_ref, nrows_ref, g_ref, x_ref, ys_ref, gain_ref, gate_ref, y_hbm, o_ref, buf, sems,
                    *, n_pick):
    i = pl.program_id(0)
    tt = x_ref.shape[0]
    slot = i % 2

    def gather(table_ref, dst_slot):
        def body(tok, carry):
            for k in range(n_pick):
                _row_copy(y_hbm, table_ref[0, k * tt + tok], buf.at[dst_slot, k], tok, sems.at[dst_slot]).start()
            return carry

        lax.fori_loop(0, tt, body, 0)

    @pl.when(i == 0)
    def _():
        gather(rows_ref, 0)

    @pl.when(i + 1 < pl.num_programs(0))
    def _():
        gather(nrows_ref, 1 - slot)

    for k in range(n_pick):
        pltpu.make_async_copy(y_hbm.at[pl.ds(0, tt), :], buf.at[slot, k], sems.at[slot]).wait()
    g = g_ref[...]
    acc_lo, acc_hi = _unpack_halves(ys_ref[...])
    for k in range(n_pick):
        lo, hi = _unpack_halves(buf[slot, k])
        acc_lo = acc_lo + lo * g[:, k:k + 1]
        acc_hi = acc_hi + hi * g[:, k:k + 1]
    acc = jnp.concatenate([acc_lo, acc_hi], axis=1)
    yn = acc * lax.rsqrt(jnp.mean(acc * acc, axis=-1, keepdims=True) + RMS_EPS) * gain_ref[...]
    o_ref[...] = x_ref[...] + gate_ref[...] * yn


def _combine(y_routed, y_shared, rows_tiled, gates, x2, gain, gate, seq, tt):
    t, d = x2.shape
    n_pick = rows_tiled.shape[2] // tt
    n_tiles = t // tt
    bmap = lambda i: ((i * tt) // seq, 0, 0)
    row = pl.BlockSpec((tt, d), lambda i: (i, 0))
    table = lambda imap: pl.BlockSpec((None, 1, n_pick * tt), imap, memory_space=pltpu.SMEM)
    return pl.pallas_call(
        functools.partial(_combine_kernel, n_pick=n_pick),
        out_shape=jax.ShapeDtypeStruct((t, d), F32),
        grid=(n_tiles,),
        in_specs=[
            table(lambda i: (i, 0, 0)),
            table(lambda i: (jnp.minimum(i + 1, n_tiles - 1), 0, 0)),
            pl.BlockSpec((tt, gates.shape[1]), lambda i: (i, 0)),
            row,
            pl.BlockSpec((tt, d // 2), lambda i: (i, 0)),
            pl.BlockSpec((1, d), lambda i: (0, 0)),
            pl.BlockSpec((None, 1, d), bmap),
            pl.BlockSpec(memory_space=pl.ANY),
        ],
        out_specs=row,
        scratch_shapes=[pltpu.VMEM((2, n_pick, tt, d // 2), jnp.uint32), pltpu.SemaphoreType.DMA((2,))],
        compiler_params=_cparams("arbitrary"),
        name="moe_combine",
    )(rows_tiled, rows_tiled, gates, x2, y_shared, gain, gate, y_routed)


def _tile_rows(rows, tile):
    n_pick, t = rows.shape
    return rows.reshape(n_pick, t // tile, tile).transpose(1, 0, 2).reshape(t // tile, 1, n_pick * tile)


def _moe_block(x2, y_mix, gain_mix, gate_mix, gain_in, shift, scale, rw_t, rb, w_gu, w_down, ws_gu, ws_down, layer,
               gain_out, gate, seq, tm=256, td=512, tt=128):
    t, d = x2.shape
    ne = rw_t.shape[0]
    x2, u, idx, pos, gate_k, counts = _router(x2, y_mix, gain_mix, gate_mix, gain_in, shift, scale, rw_t, rb, seq)
    work, starts = _expert_work_list(counts[:, 0].astype(jnp.int32), TOP_K * t, tm)
    hit = idx[:, :, None] == jnp.arange(ne, dtype=jnp.int32)[None, None, :]
    rows = jnp.sum(jnp.where(hit, starts[None, None, :], 0), axis=-1) + pos
    x_sorted = _dispatch(u, _tile_rows(rows, td), TOP_K * t, td)
    y_routed = _experts(x_sorted, work, w_gu, w_down, layer, tm)
    shared_work, _ = _expert_work_list(jnp.full((1,), t, jnp.int32), t, tm)
    y_shared = _experts(u, shared_work, ws_gu, ws_down, layer, tm)
    return _combine(y_routed, y_shared, _tile_rows(rows, tt), gate_k.T, x2, gain_out, gate, seq, tt)


def kernel(x, c, rel_bias, ada_w, ada_b, norm_gains, w_in, shift_mu, rwkv_w0, rwkv_w2, rwkv_a0, rwkv_a2, rwkv_g2,
           rwkv_k_k, rwkv_k_a, rwkv_r_k, rwkv_ln_w, rwkv_ln_b, attn_sinks, w_out, router_w, router_bias,
           expert_w_gu, expert_w_down, shared_w_gu, shared_w_down):
    batch, seq, d = x.shape
    depth = ada_w.shape[0]
    t = batch * seq
    rw = rwkv_w0.shape[1]
    dl, al, gl = rwkv_w2.shape[1], rwkv_a2.shape[1], rwkv_g2.shape[1]
    aw = w_out.shape[1] - rw
    n_q = aw // HEAD_DIM
    n_kv = (w_in.shape[2] - (3 * rw + dl + al + gl) - aw) // (2 * HEAD_DIM)
    q_per_kv = n_q // n_kv
    q_off = 3 * rw + dl + al + gl
    assert q_off % HEAD_DIM == 0, "attention heads must start at a multiple of the head width"

    c_pad = jnp.pad(c, ((0, 8 - batch % 8 if batch % 8 else 0), (0, 0)))
    ada = _ada(c_pad, ada_w, ada_b)[:, :batch]
    bias = _bias_table(rel_bias)
    w_in_t = jnp.swapaxes(w_in, 1, 2)
    w_gu_b = expert_w_gu
    ws_gu_b = shared_w_gu[:, None]

    x2 = x.reshape(t, d)
    for l in range(depth):
        sh_m, sc_m, gt_m, sh_f, sc_f, gt_f = [a.reshape(batch, 1, d) for a in jnp.split(ada[l], 6, axis=-1)]
        gains = norm_gains[l].reshape(4, 1, d)

        proj = _in_proj(_norm_mod_bf16(x2, gains[0], sh_m, sc_m, seq), w_in_t, l)

        r, k, v, a, g, lw = _rwkv_prep(proj, shift_mu[l][None, :], rwkv_w0[l][None, :], rwkv_w2[l],
                                       rwkv_a0[l][None, :], rwkv_a2[l], rwkv_g2[l], seq)
        y_r = _rwkv_chunk(r, k, v, a, g, lw, rwkv_k_k[l][None, :], rwkv_k_a[l][None, :],
                          rwkv_r_k[l].reshape(1, rw), rwkv_ln_w[l][None, :], rwkv_ln_b[l][None, :], batch, seq)
        y_a = _swa(proj, attn_sinks[l], bias, batch, seq, q_off, n_kv, q_per_kv)
        y_mix = _out_proj(y_r, y_a, w_out, l)
        x2 = _moe_block(x2, y_mix, gains[1], gt_m, gains[2], sh_f, sc_f, router_w[l].T, router_bias[l][:, None],
                        w_gu_b, expert_w_down, ws_gu_b, shared_w_down[:, None], l, gains[3], gt_f, seq)
    return x2.reshape(batch, seq, d)
```

```python
import functools
import math

import numpy as np
import jax
import jax.numpy as jnp
from jax import lax
from jax.experimental import pallas as pl
from jax.experimental.pallas import tpu as pltpu

F32 = jnp.float32
BF16 = jnp.bfloat16

HEAD_DIM = 64
LANES = 128
WINDOW = 128
N_BUCKETS = 32
MAX_DISTANCE = 128
N_EXPERT_GROUPS = 8
TOPK_GROUPS = 4
TOP_K = 8
ROUTED_SCALE = 2.5
RMS_EPS = 1e-6
GN_EPS = 64e-5
CHUNK = 64
HEADS_PER_GROUP = 4
GROUP_LANES = HEADS_PER_GROUP * HEAD_DIM
VMEM_LIMIT = 60 * 1024 * 1024

HIGHEST = lax.Precision.HIGHEST


def _round_up(n, m):
    return (n + m - 1) // m * m


def _cparams(*sem):
    return pltpu.CompilerParams(dimension_semantics=sem, vmem_limit_bytes=VMEM_LIMIT)


def _dot(a, b):
    return jnp.dot(a, b, preferred_element_type=F32)


def _dot_nt(a, b):
    return lax.dot_general(a, b, (((1,), (1,)), ((), ())), preferred_element_type=F32)


def _dot_tn(a, b):
    return lax.dot_general(a, b, (((0,), (0,)), ((), ())), preferred_element_type=F32)


def _bdot(a, b):
    return lax.dot_general(a, b, (((2,), (1,)), ((0,), (0,))), preferred_element_type=F32)


def _bdot_nt(a, b):
    return lax.dot_general(a, b, (((2,), (2,)), ((0,), (0,))), preferred_element_type=F32)


def _sigmoid(x):
    return 1.0 / (1.0 + jnp.exp(-x))


def _pack_halves(x):
    half = x.shape[1] // 2
    lo = lax.bitcast_convert_type(x[:, :half].astype(BF16).astype(F32), jnp.uint32)
    hi = lax.bitcast_convert_type(x[:, half:].astype(BF16).astype(F32), jnp.uint32)
    return (lo >> 16) | (hi & jnp.uint32(0xFFFF0000))


def _unpack_halves(p):
    lo = lax.bitcast_convert_type(p << 16, F32)
    hi = lax.bitcast_convert_type(p & jnp.uint32(0xFFFF0000), F32)
    return lo, hi


def _norm_mod(x, gain, shift, scale):
    y = x * lax.rsqrt(jnp.mean(x * x, axis=-1, keepdims=True) + RMS_EPS) * gain
    return y * (1.0 + scale) + shift


def _ada_kernel(c_ref, w_ref, b_ref, o_ref):
    c = c_ref[...]
    cond = c * _sigmoid(c)
    o_ref[...] = _dot(cond.astype(BF16), w_ref[...].astype(BF16)) + b_ref[...]


def _ada(c_pad, ada_w, ada_b):
    nl, d, n = ada_w.shape
    tn = 512
    return pl.pallas_call(
        _ada_kernel,
        out_shape=jax.ShapeDtypeStruct((nl, c_pad.shape[0], n), F32),
        grid=(nl, n // tn),
        in_specs=[
            pl.BlockSpec((c_pad.shape[0], d), lambda l, j: (0, 0)),
            pl.BlockSpec((None, d, tn), lambda l, j: (l, 0, j)),
            pl.BlockSpec((None, 1, tn), lambda l, j: (l, 0, j)),
        ],
        out_specs=pl.BlockSpec((None, c_pad.shape[0], tn), lambda l, j: (l, 0, j)),
        compiler_params=_cparams("parallel", "parallel"),
        name="ada",
    )(c_pad, ada_w, ada_b.reshape(nl, 1, n))


def _bias_kernel(rbt_ref, oh_ref, o_ref):
    o_ref[...] = jnp.dot(rbt_ref[...], oh_ref[...], preferred_element_type=F32, precision=HIGHEST)


def _bucket_onehot():
    qi = np.arange(WINDOW)[:, None]
    kj = np.arange(2 * WINDOW)[None, :]
    dist = qi + WINDOW - kj
    max_exact = N_BUCKETS // 2
    d = np.maximum(dist, 0)
    far = max_exact + (np.log(np.maximum(d, max_exact).astype(np.float32) / np.float32(max_exact))
                       / np.float32(math.log(MAX_DISTANCE / max_exact))
                       * np.float32(N_BUCKETS - max_exact)).astype(np.int32)
    bucket = np.where(d < max_exact, d, np.minimum(far, N_BUCKETS - 1)).reshape(-1)
    return (np.arange(N_BUCKETS)[:, None] == bucket[None, :]).astype(np.float32)


def _bias_table(rel_bias):
    nb, nh = rel_bias.shape
    onehot = jnp.asarray(_bucket_onehot())
    out = pl.pallas_call(
        _bias_kernel,
        out_shape=jax.ShapeDtypeStruct((nh, WINDOW * 2 * WINDOW), F32),
        name="bias_table",
        compiler_params=pltpu.CompilerParams(vmem_limit_bytes=VMEM_LIMIT),
    )(rel_bias.T, onehot)
    return out.reshape(nh, WINDOW, 2 * WINDOW)


def _norm_mod_kernel(x_ref, gain_ref, sh_ref, sc_ref, u_ref):
    u_ref[...] = _norm_mod(x_ref[...], gain_ref[...], sh_ref[...], sc_ref[...]).astype(u_ref.dtype)


def _norm_mod_bf16(x2, gain, shift, scale, seq, tm=512):
    t, d = x2.shape
    bmap = lambda i: ((i * tm) // seq, 0, 0)
    row = pl.BlockSpec((tm, d), lambda i: (i, 0))
    return pl.pallas_call(
        _norm_mod_kernel,
        out_shape=jax.ShapeDtypeStruct((t, d), BF16),
        grid=(t // tm,),
        in_specs=[row, pl.BlockSpec((1, d), lambda i: (0, 0)), pl.BlockSpec((None, 1, d), bmap),
                  pl.BlockSpec((None, 1, d), bmap)],
        out_specs=row,
        compiler_params=_cparams("parallel"),
        name="norm_mod",
    )(x2, gain, shift, scale)


def _in_proj_kernel(u_ref, wt_ref, o_ref, wb_ref):
    @pl.when(pl.program_id(1) == 0)
    def _():
        wb_ref[...] = wt_ref[...].astype(BF16)

    o_ref[...] = _dot_nt(u_ref[...], wb_ref[...])


def _in_proj(u, wt_all, layer, tm=1024, tn=512):
    t, d = u.shape
    n = wt_all.shape[1]
    tm = min(tm, t)
    return pl.pallas_call(
        _in_proj_kernel,
        out_shape=jax.ShapeDtypeStruct((t, n), F32),
        grid=(pl.cdiv(n, tn), t // tm),
        in_specs=[
            pl.BlockSpec((tm, d), lambda j, i: (i, 0)),
            pl.BlockSpec((None, tn, d), lambda j, i: (layer, j, 0)),
        ],
        out_specs=pl.BlockSpec((tm, tn), lambda j, i: (i, j)),
        scratch_shapes=[pltpu.VMEM((tn, d), BF16)],
        compiler_params=_cparams("parallel", "arbitrary"),
        name="in_proj",
    )(u, wt_all)


def _lane_window(start, width):
    lo = start // LANES * LANES
    return lo, _round_up(start + width, LANES), start - lo


def _window_rows(w, start):
    lo, hi, off = _lane_window(start, w.shape[0])
    return jnp.pad(w, ((off, hi - lo - off - w.shape[0]), (0, 0))).astype(BF16)


def _rwkv_prep_kernel(rw, xw_win, xa_win, xg_win, p_ref, prev_ref, mu_ref, w0_ref, w2_ref, a0_ref, a2_ref, g2_ref,
                      r_ref, k_ref, v_ref, a_ref, g_ref, lw_ref, *, blocks_per_seq):
    i = pl.program_id(0)
    cur = p_ref[...]
    first = (i % blocks_per_seq) == 0
    prev_row = jnp.where(first, 0.0, prev_ref[7:8, :])
    rolled = pltpu.roll(cur, 1, axis=0)
    row = lax.broadcasted_iota(jnp.int32, cur.shape, 0)
    shifted = jnp.where(row == 0, prev_row, rolled)
    x = cur + (shifted - cur) * mu_ref[...]
    r_ref[...] = x[:, 0:rw].astype(r_ref.dtype)
    k_ref[...] = x[:, rw:2 * rw].astype(k_ref.dtype)
    v_ref[...] = x[:, 2 * rw:3 * rw].astype(v_ref.dtype)
    xw = x[:, xw_win[0]:xw_win[1]]
    xa = x[:, xa_win[0]:xa_win[1]]
    xg = x[:, xg_win[0]:xg_win[1]]
    z = w0_ref[...] + _dot(jnp.tanh(xw).astype(BF16), w2_ref[...])
    lw_ref[...] = -math.exp(-0.5) * _sigmoid(z)
    a_ref[...] = _sigmoid(a0_ref[...] + _dot(xa.astype(BF16), a2_ref[...])).astype(a_ref.dtype)
    g_ref[...] = _dot(_sigmoid(xg).astype(BF16), g2_ref[...]).astype(g_ref.dtype)


def _rwkv_prep(proj, mu, w0, w2, a0, a2, g2, seq, ts=256):
    t = proj.shape[0]
    rw = w0.shape[1]
    dl, al, gl = w2.shape[0], a2.shape[0], g2.shape[0]
    starts = (3 * rw, 3 * rw + dl, 3 * rw + dl + al)
    wins = [_lane_window(s, w)[:2] for s, w in zip(starts, (dl, al, gl))]
    wblock = wins[2][1]
    w2p, a2p, g2p = (_window_rows(w, s) for w, s in zip((w2, a2, g2), starts))
    mu_p = jnp.pad(mu, ((0, 0), (0, wblock - mu.shape[1])))
    full = lambda i: (0, 0)
    outs = [jax.ShapeDtypeStruct((t, rw), BF16)] * 5 + [jax.ShapeDtypeStruct((t, rw), F32)]
    ospec = pl.BlockSpec((ts, rw), lambda i: (i, 0))
    return pl.pallas_call(
        functools.partial(_rwkv_prep_kernel, rw, *wins, blocks_per_seq=seq // ts),
        out_shape=outs,
        grid=(t // ts,),
        in_specs=[
            pl.BlockSpec((ts, wblock), lambda i: (i, 0)),
            pl.BlockSpec((8, wblock), lambda i: (jnp.maximum(i * (ts // 8) - 1, 0), 0)),
            pl.BlockSpec((1, wblock), full),
            pl.BlockSpec((1, rw), full),
            pl.BlockSpec(w2p.shape, full),
            pl.BlockSpec((1, rw), full),
            pl.BlockSpec(a2p.shape, full),
            pl.BlockSpec(g2p.shape, full),
        ],
        out_specs=[ospec] * 6,
        compiler_params=_cparams("parallel"),
        name="rwkv_prep",
    )(proj, proj, mu_p, w0, w2p, a0, a2p, g2p)


CHUNK_LEVELS = int(math.log2(CHUNK))


def _chunk_constants():
    n = GROUP_LANES
    i = np.arange(n)[:, None]
    j = np.arange(n)[None, :]
    same_head = (i // CHUNK) == (j // CHUNK)
    code = np.zeros((n, n), np.int32)
    for lvl in range(CHUNK_LEVELS, 0, -1):
        code = np.where(same_head & (i > j) & ((i >> lvl) == (j >> lvl)), lvl, code)
    planes = [code == lvl for lvl in range(1, CHUNK_LEVELS + 1)] + [code > 0, same_head]
    return np.stack(planes).astype(np.float32)


def _seg_sum(x, ones_bd):
    return _dot(x.astype(BF16), ones_bd)


def _split3(x):
    hi = x.astype(BF16)
    r1 = x - hi.astype(F32)
    mid = r1.astype(BF16)
    return hi, mid, (r1 - mid.astype(F32)).astype(BF16)


def _rwkv_chunk_kernel(r_ref, k_ref, v_ref, a_ref, g_ref, lw_ref,
                       kk_ref, ka_ref, rk_ref, lnw_ref, lnb_ref, planes_ref,
                       o_ref, s_ref):
    @pl.when(pl.program_id(2) == 0)
    def _():
        s_ref[...] = jnp.zeros_like(s_ref)

    n = GROUP_LANES
    lvl_mask = [planes_ref[l] for l in range(CHUNK_LEVELS)]
    low_mask = planes_ref[CHUNK_LEVELS]
    bd = planes_ref[CHUNK_LEVELS + 1]
    tb = r_ref.shape[0]
    nch = tb // CHUNK
    tri_incl = (lax.broadcasted_iota(jnp.int32, (1, CHUNK, n), 1)
                >= lax.broadcasted_iota(jnp.int32, (1, CHUNK, n), 2) % CHUNK)
    ltri = jnp.where(lax.broadcasted_iota(jnp.int32, (nch, CHUNK, CHUNK), 1)
                     >= lax.broadcasted_iota(jnp.int32, (nch, CHUNK, CHUNK), 2), 1.0, 0.0).astype(BF16)
    eye = jnp.where(lax.broadcasted_iota(jnp.int32, (1, n, n), 1)
                    == lax.broadcasted_iota(jnp.int32, (1, n, n), 2), 1.0, 0.0).astype(BF16)

    def chunks(x):
        return x.reshape(nch, CHUNK, n)

    def tile_heads(x):
        return jnp.concatenate([x] * HEADS_PER_GROUP, axis=1)

    def stack(x):
        return tile_heads(x.astype(BF16)) * bd[None]

    r = r_ref[...].astype(F32)
    k = k_ref[...].astype(F32)
    v = v_ref[...].astype(F32)
    a = a_ref[...].astype(F32)
    lw = lw_ref[...]
    kkp = k * kk_ref[...]
    kk = kkp / jnp.maximum(jnp.sqrt(_seg_sum(kkp * kkp, bd)), 1e-12)
    k2 = k * (1.0 + (a - 1.0) * ka_ref[...])
    kb = kk * a
    bonus = _seg_sum(r * k2 * rk_ref[...], bd) * v

    cum = sum(_bdot(ltri, chunks(piece)) for piece in _split3(lw))
    cum_end = cum[:, CHUNK - 1:CHUNK, :]
    e_in = jnp.exp(cum)
    e_ex = jnp.exp(cum - chunks(lw))
    e_neg = jnp.exp(-cum)
    e_dec = jnp.exp(cum_end - cum)
    decay = jnp.exp(cum_end)
    r_t = chunks(r) * e_in
    a_t = -chunks(kk) * e_ex
    kb3 = chunks(kb)
    k23 = chunks(k2)

    ar = jnp.concatenate([a_t, r_t], axis=1).astype(BF16)
    v_s = stack(chunks(v))
    ab_rb = _bdot_nt(ar, stack(kb3 * e_neg))
    ak_rk = _bdot_nt(ar, stack(k23 * e_neg))
    rb_w = jnp.where(tri_incl, ab_rb[:, CHUNK:], 0.0).astype(BF16)
    rk_w = jnp.where(tri_incl, ak_rk[:, CHUNK:], 0.0).astype(BF16)
    amat = tile_heads(ab_rb[:, :CHUNK].astype(BF16))
    akmat = tile_heads(ak_rk[:, :CHUNK].astype(BF16)) * low_mask[None]

    def lower_rows(x, s):
        m = x.shape[2]
        return x.reshape(nch * n // s, 2, s // 2, m)[:, 1].reshape(nch, n // 2, m)

    def as_lower_rows(x, s):
        m = x.shape[2]
        x4 = x.reshape(nch * n // s, 1, s // 2, m)
        return jnp.concatenate([jnp.zeros_like(x4), x4], axis=1).reshape(nch, n, m)

    tinv = eye + amat * lvl_mask[0][None]
    for lvl in range(1, CHUNK_LEVELS):
        off = amat * lvl_mask[lvl][None]
        s = 2 ** (lvl + 1)
        if s // 2 >= 16:
            m1 = as_lower_rows(_bdot(lower_rows(off, s), tinv).astype(BF16), s)
            tinv = tinv + as_lower_rows(_bdot(lower_rows(tinv, s), m1).astype(BF16), s)
        else:
            tinv = tinv + _bdot(tinv, _bdot(off, tinv).astype(BF16)).astype(BF16)

    w_s = _bdot(tinv, stack(a_t)).astype(BF16)
    z_s = _bdot(tinv, _bdot(akmat, v_s).astype(BF16)).astype(BF16)
    bh_s = stack(kb3 * e_dec)
    kh_s = stack(k23 * e_dec)
    r_eff = (r_t + _bdot(rb_w, w_s)).astype(BF16)
    y0 = _bdot(rb_w, z_s) + _bdot(rk_w, v_s)

    s = s_ref[...]
    ys = []
    for ci in range(nch):
        p_mat = _dot_tn(w_s[ci], bh_s[ci]).astype(BF16)
        q_mat = _dot_tn(z_s[ci], bh_s[ci]) + _dot_tn(v_s[ci], kh_s[ci])
        s_b = s.astype(BF16)
        ys.append(_dot_nt(r_eff[ci], s_b) + y0[ci])
        s = s * decay[ci] + _dot(s_b, p_mat) + q_mat
    s_ref[...] = s

    y = jnp.concatenate(ys, axis=0)
    inv_n = 1.0 / HEAD_DIM
    mean = _seg_sum(y, bd) * inv_n
    dlt = y - mean
    var = _seg_sum(dlt * dlt, bd) * inv_n
    yn = dlt * lax.rsqrt(var + GN_EPS) * lnw_ref[...] + lnb_ref[...]
    o_ref[...] = ((yn + bonus) * g_ref[...].astype(F32)).astype(o_ref.dtype)


def _rwkv_chunk(r, k, v, a, g, lw, k_k, k_a, r_k, ln_w, ln_b, batch, seq, tb=1024):
    t, rw = r.shape
    n = GROUP_LANES
    tb = min(tb, seq)
    planes = jnp.asarray(_chunk_constants(), dtype=BF16)
    nblk = seq // tb
    tmap = lambda b, h, c: (b * nblk + c, h)
    pmap = lambda b, h, c: (0, h)
    tspec = pl.BlockSpec((tb, n), tmap)
    pspec = pl.BlockSpec((1, n), pmap)
    return pl.pallas_call(
        _rwkv_chunk_kernel,
        out_shape=jax.ShapeDtypeStruct((t, rw), BF16),
        grid=(batch, rw // n, nblk),
        in_specs=[tspec] * 6 + [pspec] * 5 + [pl.BlockSpec(planes.shape, lambda b, h, c: (0, 0, 0))],
        out_specs=pl.BlockSpec((tb, n), tmap),
        scratch_shapes=[pltpu.VMEM((n, n), F32)],
        compiler_params=_cparams("parallel", "parallel", "arbitrary"),
        name="rwkv_chunk",
    )(r, k, v, a, g, lw, k_k, k_a, r_k, ln_w, ln_b, planes)


def _swa_kernel(sink_ref, *refs, q_per_kv, q_off, k_off, v_off, n_q_tiles):
    q_refs = refs[:n_q_tiles]
    kp_ref, kc_ref, vp_ref, vc_ref, bias_ref, o_ref = refs[n_q_tiles:]
    j = pl.program_id(1)
    nblk = pl.program_id(2)
    blk = WINDOW
    upper = lax.broadcasted_iota(jnp.int32, (1, LANES), 1) >= HEAD_DIM

    def both_halves(x, start_col):
        in_upper = (start_col // HEAD_DIM) % 2 == 1
        sel = jnp.where(upper == in_upper, x, 0.0)
        return sel + pltpu.roll(sel, HEAD_DIM, axis=1)

    kw = both_halves(jnp.concatenate([kp_ref[...], kc_ref[...]], axis=0), k_off + HEAD_DIM * j).astype(BF16)
    vw = both_halves(jnp.concatenate([vp_ref[...], vc_ref[...]], axis=0), v_off + HEAD_DIM * j)
    v_half = (jnp.where(upper, 0.0, vw).astype(BF16), jnp.where(upper, vw, 0.0).astype(BF16))
    qi = lax.broadcasted_iota(jnp.int32, (blk, 2 * blk), 0)
    kj = lax.broadcasted_iota(jnp.int32, (blk, 2 * blk), 1)
    valid = (kj > qi) & (kj <= qi + blk) & ((nblk > 0) | (kj >= blk))
    scale = HEAD_DIM ** -0.5
    for p in range(q_per_kv // 2):
        acc = None
        for half in range(2):
            g = 2 * p + half
            col = q_off % LANES + HEAD_DIM * g
            q_tile = q_refs[col // LANES][...] * scale
            q_upper = (col // HEAD_DIM) % 2 == 1
            qm = jnp.where(upper == q_upper, q_tile, 0.0).astype(BF16)
            sink = sink_ref[j * q_per_kv + g]
            s = _dot_nt(qm, kw) + bias_ref[g]
            s = jnp.where(valid, s, -jnp.inf)
            m = jnp.maximum(jnp.max(s, axis=-1, keepdims=True), sink)
            pr = jnp.exp(s - m)
            denom = jnp.sum(pr, axis=-1, keepdims=True) + jnp.exp(sink - m)
            o = _dot(pr.astype(BF16), v_half[half]) / denom
            acc = o if acc is None else acc + o
        o_ref[:, p * LANES:(p + 1) * LANES] = acc.astype(o_ref.dtype)


def _swa(proj, sinks, bias, batch, seq, q_off, n_kv, q_per_kv):
    t = proj.shape[0]
    blk = WINDOW
    nb = seq // blk
    qw = q_per_kv * HEAD_DIM
    k_off = q_off + n_kv * qw
    v_off = k_off + n_kv * HEAD_DIM
    n_q_tiles = (q_off % LANES + qw + LANES - 1) // LANES
    cur = lambda b, j, n: b * nb + n
    prev = lambda b, j, n: b * nb + jnp.maximum(n - 1, 0)
    tile = lambda rowf, colf: pl.BlockSpec((blk, LANES), lambda b, j, n: (rowf(b, j, n), colf(j)))
    q_specs = [tile(cur, lambda j, i=i: q_off // LANES + (qw // LANES) * j + i) for i in range(n_q_tiles)]
    k_col = lambda j: (k_off + HEAD_DIM * j) // LANES
    v_col = lambda j: (v_off + HEAD_DIM * j) // LANES
    return pl.pallas_call(
        functools.partial(_swa_kernel, q_per_kv=q_per_kv, q_off=q_off, k_off=k_off, v_off=v_off,
                          n_q_tiles=n_q_tiles),
        out_shape=jax.ShapeDtypeStruct((t, n_kv * qw), BF16),
        grid=(batch, n_kv, nb),
        in_specs=[pl.BlockSpec(memory_space=pltpu.SMEM)] + q_specs + [
            tile(prev, k_col), tile(cur, k_col), tile(prev, v_col), tile(cur, v_col),
            pl.BlockSpec((q_per_kv, blk, 2 * blk), lambda b, j, n: (j, 0, 0)),
        ],
        out_specs=pl.BlockSpec((blk, qw), lambda b, j, n: (cur(b, j, n), j)),
        compiler_params=_cparams("parallel", "parallel", "parallel"),
        name="swa",
    )(sinks, *([proj] * (n_q_tiles + 4)), bias)


def _out_proj_kernel(yr_ref, ya_ref, w_ref, o_ref, wb_ref):
    @pl.when(pl.program_id(1) == 0)
    def _():
        wb_ref[...] = w_ref[...].astype(BF16)

    rw = yr_ref.shape[1]
    o_ref[...] = _dot(yr_ref[...], wb_ref[0:rw, :]) + _dot(ya_ref[...], wb_ref[rw:, :])


def _out_proj(y_r, y_a, w_all, layer, tm=1024, tn=512):
    t, rw = y_r.shape
    aw = y_a.shape[1]
    d = w_all.shape[2]
    tm = min(tm, t)
    return pl.pallas_call(
        _out_proj_kernel,
        out_shape=jax.ShapeDtypeStruct((t, d), F32),
        grid=(d // tn, t // tm),
        in_specs=[
            pl.BlockSpec((tm, rw), lambda j, i: (i, 0)),
            pl.BlockSpec((tm, aw), lambda j, i: (i, 0)),
            pl.BlockSpec((None, rw + aw, tn), lambda j, i: (layer, 0, j)),
        ],
        out_specs=pl.BlockSpec((tm, tn), lambda j, i: (i, j)),
        scratch_shapes=[pltpu.VMEM((rw + aw, tn), BF16)],
        compiler_params=_cparams("parallel", "arbitrary"),
        name="out_proj",
    )(y_r, y_a, w_all)


def _rank(vals):
    n = vals.shape[0]
    idx = lax.broadcasted_iota(jnp.int32, vals.shape, 0)
    rank = jnp.zeros(vals.shape, jnp.int32)
    for e in range(n):
        row = vals[e:e + 1, :]
        beats = (row > vals) | ((row == vals) & (e < idx))
        rank = rank + beats.astype(jnp.int32)
    return rank


def _router_kernel(x_ref, y_ref, gain_y_ref, gate_y_ref, gain_ref, sh_ref, sc_ref, rwt_ref, rb_ref,
                   xn_ref, u_ref, idx_ref, pos_ref, gate_ref, cnt_ref, carry_ref):
    @pl.when(pl.program_id(0) == 0)
    def _():
        carry_ref[...] = jnp.zeros_like(carry_ref)

    y = y_ref[...]
    yn = y * lax.rsqrt(jnp.mean(y * y, axis=-1, keepdims=True) + RMS_EPS) * gain_y_ref[...]
    xn = x_ref[...] + gate_y_ref[...] * yn
    xn_ref[...] = xn
    u = _norm_mod(xn, gain_ref[...], sh_ref[...], sc_ref[...])
    u_ref[...] = _pack_halves(u)
    logits = lax.dot_general(rwt_ref[...], u, (((1,), (1,)), ((), ())),
                             preferred_element_type=F32, precision=HIGHEST)
    scores = _sigmoid(logits)
    sel = scores + rb_ref[:, 0:1]
    ne, tm = sel.shape
    gsz = ne // N_EXPERT_GROUPS
    sub = lax.broadcasted_iota(jnp.int32, (gsz, tm), 0)
    gs_rows = []
    for gi in range(N_EXPERT_GROUPS):
        grp = sel[gi * gsz:(gi + 1) * gsz, :]
        m1 = jnp.max(grp, axis=0, keepdims=True)
        i1 = jnp.min(jnp.where(grp == m1, sub, gsz), axis=0, keepdims=True)
        m2 = jnp.max(jnp.where(sub == i1, -jnp.inf, grp), axis=0, keepdims=True)
        gs_rows.append(m1 + m2)
    gkeep = _rank(jnp.concatenate(gs_rows, axis=0)) < TOPK_GROUPS
    ekeep = jnp.concatenate(
        [jnp.broadcast_to(gkeep[gi:gi + 1, :], (gsz, tm)) for gi in range(N_EXPERT_GROUPS)], axis=0)
    rank = _rank(jnp.where(ekeep, sel, -jnp.inf))
    chosen = rank < TOP_K
    wts = jnp.where(chosen, scores, 0.0)
    gates_t = wts / jnp.sum(wts, axis=0, keepdims=True) * ROUTED_SCALE

    cf = jnp.where(chosen, 1.0, 0.0)
    before = (lax.broadcasted_iota(jnp.int32, (tm, tm), 0) < lax.broadcasted_iota(jnp.int32, (tm, tm), 1))
    carry = carry_ref[:, 0:1]
    pos = carry + _dot(cf.astype(BF16), jnp.where(before, 1.0, 0.0).astype(BF16))
    carry_ref[...] = jnp.broadcast_to(carry + jnp.sum(cf, axis=1, keepdims=True), carry_ref.shape)
    cnt_ref[...] = carry_ref[...]

    eidx = lax.broadcasted_iota(jnp.int32, (ne, tm), 0)
    rows_i, rows_p, rows_g = [], [], []
    for k in range(TOP_K):
        mk = rank == k
        rows_i.append(jnp.sum(jnp.where(mk, eidx, 0), axis=0, keepdims=True))
        rows_p.append(jnp.sum(jnp.where(mk, pos, 0.0), axis=0, keepdims=True))
        rows_g.append(jnp.sum(jnp.where(mk, gates_t, 0.0), axis=0, keepdims=True))
    idx_ref[...] = jnp.concatenate(rows_i, axis=0)
    pos_ref[...] = jnp.concatenate(rows_p, axis=0).astype(jnp.int32)
    gate_ref[...] = jnp.concatenate(rows_g, axis=0)


def _router(x2, y, gain_y, gate_y, gain, shift, scale, rw_t, rb, seq, tm=256):
    t, d = x2.shape
    ne = rw_t.shape[0]
    bmap = lambda i: ((i * tm) // seq, 0, 0)
    pick = lambda dt: jax.ShapeDtypeStruct((TOP_K, t), dt)
    pspec = pl.BlockSpec((TOP_K, tm), lambda i: (0, i))
    row = pl.BlockSpec((tm, d), lambda i: (i, 0))
    vec = pl.BlockSpec((1, d), lambda i: (0, 0))
    per_batch = pl.BlockSpec((None, 1, d), bmap)
    return pl.pallas_call(
        _router_kernel,
        out_shape=[jax.ShapeDtypeStruct((t, d), F32), jax.ShapeDtypeStruct((t, d // 2), jnp.uint32),
                   pick(jnp.int32), pick(jnp.int32), pick(F32), jax.ShapeDtypeStruct((ne, LANES), F32)],
        grid=(t // tm,),
        in_specs=[row, row, vec, per_batch, vec, per_batch, per_batch,
                  pl.BlockSpec((ne, d), lambda i: (0, 0)), pl.BlockSpec((ne, 1), lambda i: (0, 0))],
        out_specs=[row, pl.BlockSpec((tm, d // 2), lambda i: (i, 0)), pspec, pspec, pspec,
                   pl.BlockSpec((ne, LANES), lambda i: (0, 0))],
        scratch_shapes=[pltpu.VMEM((ne, LANES), F32)],
        compiler_params=_cparams("arbitrary"),
        name="router",
    )(x2, y, gain_y, gate_y, gain, shift, scale, rw_t, rb)


def _row_copy(src, src_row, dst, dst_row, sem):
    return pltpu.make_async_copy(src.at[pl.ds(src_row, 1), :], dst.at[pl.ds(dst_row, 1), :], sem)


def _dispatch_kernel(rows_ref, u_ref, x_hbm, sem, *, n_pick):
    td = u_ref.shape[0]

    def body(tok, carry):
        for k in range(n_pick):
            _row_copy(u_ref, tok, x_hbm, rows_ref[0, k * td + tok], sem).start(priority=k % 2)
        return carry

    lax.fori_loop(0, td, body, 0)
    for k in range(n_pick):
        pltpu.make_async_copy(u_ref, x_hbm.at[pl.ds(0, td), :], sem).wait()


def _dispatch(u, rows_tiled, n_rows, td):
    t, d = u.shape
    n_pick = rows_tiled.shape[2] // td
    return pl.pallas_call(
        functools.partial(_dispatch_kernel, n_pick=n_pick),
        out_shape=jax.ShapeDtypeStruct((n_rows, d), u.dtype),
        grid=(t // td,),
        in_specs=[
            pl.BlockSpec((None, 1, n_pick * td), lambda i: (i, 0, 0), memory_space=pltpu.SMEM),
            pl.BlockSpec((td, d), lambda i: (i, 0)),
        ],
        out_specs=pl.BlockSpec(memory_space=pl.ANY),
        scratch_shapes=[pltpu.SemaphoreType.DMA(())],
        compiler_params=_cparams("arbitrary"),
        name="moe_dispatch",
    )(rows_tiled, u)


def _experts_kernel(tile_ref, exp_ref, lo_ref, hi_ref, first_ref, new_ref, slot_ref, next_ref, n_ref,
                    x_ref, wgu_hbm, wd_hbm, y_ref, wgu_buf, wd_buf, sems, *, layer):
    p = pl.program_id(0)

    def weight_copies(e, s):
        return (pltpu.make_async_copy(wgu_hbm.at[layer, e], wgu_buf.at[s], sems.at[0, s]),
                pltpu.make_async_copy(wd_hbm.at[layer, e], wd_buf.at[s], sems.at[1, s]))

    @pl.when(p < n_ref[0])
    def _():
        s = slot_ref[p]

        @pl.when(new_ref[p] == 1)
        def _():
            @pl.when(p == 0)
            def _():
                for c in weight_copies(exp_ref[p], s):
                    c.start()

            for c in weight_copies(exp_ref[p], s):
                c.wait()

            @pl.when(next_ref[p] >= 0)
            def _():
                for c in weight_copies(next_ref[p], 1 - s):
                    c.start()

        wgu_ref = wgu_buf.at[s]
        wd_ref = wd_buf.at[s]
        ff = wd_ref.shape[0]
        half = x_ref.shape[1]
        x_lo, x_hi = _unpack_halves(x_ref[...])
        h = (_dot(x_lo.astype(BF16), wgu_ref[0:half, :].astype(BF16))
             + _dot(x_hi.astype(BF16), wgu_ref[half:, :].astype(BF16)))
        hg = h[:, :ff]
        hu = h[:, ff:]
        act = (hg * _sigmoid(hg) * hu).astype(BF16)
        row = lax.broadcasted_iota(jnp.int32, (x_ref.shape[0], 1), 0)
        mine = (row >= lo_ref[p]) & (row < hi_ref[p])

        @pl.when(first_ref[p] == 1)
        def _():
            y_ref[...] = _pack_halves(_dot(act, wd_ref[...].astype(BF16)))

        @pl.when(first_ref[p] == 0)
        def _():
            y_ref[...] = jnp.where(mine, _pack_halves(_dot(act, wd_ref[...].astype(BF16))), y_ref[...])


def _experts(x_sorted, work, w_gu, w_down, layer, tm):
    n_rows, half = x_sorted.shape
    _, ne1, d, ff2 = w_gu.shape
    ff = ff2 // 2
    n_work_max = n_rows // tm + ne1 - 1
    rows = pl.BlockSpec((tm, half), lambda p, tile, *_: (tile[p], 0))
    return pl.pallas_call(
        functools.partial(_experts_kernel, layer=layer),
        out_shape=jax.ShapeDtypeStruct((n_rows, half), jnp.uint32),
        grid_spec=pltpu.PrefetchScalarGridSpec(
            num_scalar_prefetch=len(work),
            grid=(n_work_max,),
            in_specs=[rows, pl.BlockSpec(memory_space=pl.ANY), pl.BlockSpec(memory_space=pl.ANY)],
            out_specs=rows,
            scratch_shapes=[pltpu.VMEM((2, d, ff2), F32), pltpu.VMEM((2, ff, d), F32),
                            pltpu.SemaphoreType.DMA((2, 2))],
        ),
        compiler_params=_cparams("arbitrary"),
        name="moe_experts",
    )(*work, x_sorted, w_gu, w_down)


def _expert_work_list(counts, n_rows, tm):
    ne1 = counts.shape[0]
    n_work_max = n_rows // tm + ne1 - 1
    ends = jnp.cumsum(counts)
    starts = ends - counts
    first_tile = starts // tm
    n_tile = jnp.where(counts > 0, (ends - 1) // tm - first_tile + 1, 0)
    work_end = jnp.cumsum(n_tile)
    n_work = work_end[-1:]
    p = jnp.minimum(jnp.arange(n_work_max, dtype=jnp.int32), n_work - 1)
    exp = jnp.sum((work_end[None, :] <= p[:, None]).astype(jnp.int32), axis=1)
    tile = first_tile[exp] + p - (work_end - n_tile)[exp]
    lo = jnp.maximum(starts[exp] - tile * tm, 0)
    hi = jnp.minimum(ends[exp] - tile * tm, tm)
    one = jnp.ones((1,), jnp.int32)
    first = jnp.concatenate([one, (tile[1:] != tile[:-1]).astype(jnp.int32)])
    new = jnp.concatenate([one, (exp[1:] != exp[:-1]).astype(jnp.int32)])
    slot = (jnp.cumsum(new) - 1) % 2
    run_end = work_end[exp]
    nxt = jnp.where(run_end < n_work, exp[jnp.minimum(run_end, n_work_max - 1)], -1)
    as_i32 = lambda a: a.astype(jnp.int32)
    return tuple(map(as_i32, (tile, exp, lo, hi, first, new, slot, nxt, n_work))), starts


def _combine_kernel(rows_ref, nrows_ref, g_ref, x_ref, ys_ref, gain_ref, gate_ref, y_hbm, o_ref, buf, sems,
                    *, n_pick):
    i = pl.program_id(0)
    tt = x_ref.shape[0]
    slot = i % 2

    def gather(table_ref, dst_slot):
        def body(tok, carry):
            for k in range(n_pick):
                _row_copy(y_hbm, table_ref[0, k * tt + tok], buf.at[dst_slot, k], tok,
                          sems.at[dst_slot]).start(priority=k % 2)
            return carry

        lax.fori_loop(0, tt, body, 0)

    @pl.when(i == 0)
    def _():
        gather(rows_ref, 0)

    @pl.when(i + 1 < pl.num_programs(0))
    def _():
        gather(nrows_ref, 1 - slot)

    for k in range(n_pick):
        pltpu.make_async_copy(y_hbm.at[pl.ds(0, tt), :], buf.at[slot, k], sems.at[slot]).wait()
    g = g_ref[...]
    acc_lo, acc_hi = _unpack_halves(ys_ref[...])
    for k in range(n_pick):
        lo, hi = _unpack_halves(buf[slot, k])
        acc_lo = acc_lo + lo * g[:, k:k + 1]
        acc_hi = acc_hi + hi * g[:, k:k + 1]
    acc = jnp.concatenate([acc_lo, acc_hi], axis=1)
    yn = acc * lax.rsqrt(jnp.mean(acc * acc, axis=-1, keepdims=True) + RMS_EPS) * gain_ref[...]
    o_ref[...] = x_ref[...] + gate_ref[...] * yn


def _combine(y_routed, y_shared, rows_tiled, gates, x2, gain, gate, seq, tt):
    t, d = x2.shape
    n_pick = rows_tiled.shape[2] // tt
    n_tiles = t // tt
    bmap = lambda i: ((i * tt) // seq, 0, 0)
    row = pl.BlockSpec((tt, d), lambda i: (i, 0))
    table = lambda imap: pl.BlockSpec((None, 1, n_pick * tt), imap, memory_space=pltpu.SMEM)
    return pl.pallas_call(
        functools.partial(_combine_kernel, n_pick=n_pick),
        out_shape=jax.ShapeDtypeStruct((t, d), F32),
        grid=(n_tiles,),
        in_specs=[
            table(lambda i: (i, 0, 0)),
            table(lambda i: (jnp.minimum(i + 1, n_tiles - 1), 0, 0)),
            pl.BlockSpec((tt, gates.shape[1]), lambda i: (i, 0)),
            row,
            pl.BlockSpec((tt, d // 2), lambda i: (i, 0)),
            pl.BlockSpec((1, d), lambda i: (0, 0)),
            pl.BlockSpec((None, 1, d), bmap),
            pl.BlockSpec(memory_space=pl.ANY),
        ],
        out_specs=row,
        scratch_shapes=[pltpu.VMEM((2, n_pick, tt, d // 2), jnp.uint32), pltpu.SemaphoreType.DMA((2,))],
        compiler_params=_cparams("arbitrary"),
        name="moe_combine",
    )(rows_tiled, rows_tiled, gates, x2, y_shared, gain, gate, y_routed)


def _tile_rows(rows, tile):
    n_pick, t = rows.shape
    return rows.reshape(n_pick, t // tile, tile).transpose(1, 0, 2).reshape(t // tile, 1, n_pick * tile)


def _moe_block(x2, y_mix, gain_mix, gate_mix, gain_in, shift, scale, rw_t, rb, w_gu, w_down, ws_gu, ws_down, layer,
               gain_out, gate, seq, tm=256, td=512, tt=128):
    t, d = x2.shape
    ne = rw_t.shape[0]
    x2, u, idx, pos, gate_k, counts = _router(x2, y_mix, gain_mix, gate_mix, gain_in, shift, scale, rw_t, rb, seq)
    work, starts = _expert_work_list(counts[:, 0].astype(jnp.int32), TOP_K * t, tm)
    hit = idx[:, :, None] == jnp.arange(ne, dtype=jnp.int32)[None, None, :]
    rows = jnp.sum(jnp.where(hit, starts[None, None, :], 0), axis=-1) + pos
    x_sorted = _dispatch(u, _tile_rows(rows, td), TOP_K * t, td)
    y_routed = _experts(x_sorted, work, w_gu, w_down, layer, tm)
    shared_work, _ = _expert_work_list(jnp.full((1,), t, jnp.int32), t, tm)
    y_shared = _experts(u, shared_work, ws_gu, ws_down, layer, tm)
    return _combine(y_routed, y_shared, _tile_rows(rows, tt), gate_k.T, x2, gain_out, gate, seq, tt)


def kernel(x, c, rel_bias, ada_w, ada_b, norm_gains, w_in, shift_mu, rwkv_w0, rwkv_w2, rwkv_a0, rwkv_a2, rwkv_g2,
           rwkv_k_k, rwkv_k_a, rwkv_r_k, rwkv_ln_w, rwkv_ln_b, attn_sinks, w_out, router_w, router_bias,
           expert_w_gu, expert_w_down, shared_w_gu, shared_w_down):
    batch, seq, d = x.shape
    depth = ada_w.shape[0]
    t = batch * seq
    rw = rwkv_w0.shape[1]
    dl, al, gl = rwkv_w2.shape[1], rwkv_a2.shape[1], rwkv_g2.shape[1]
    aw = w_out.shape[1] - rw
    n_q = aw // HEAD_DIM
    n_kv = (w_in.shape[2] - (3 * rw + dl + al + gl) - aw) // (2 * HEAD_DIM)
    q_per_kv = n_q // n_kv
    q_off = 3 * rw + dl + al + gl
    assert q_off % HEAD_DIM == 0, "attention heads must start at a multiple of the head width"

    c_pad = jnp.pad(c, ((0, 8 - batch % 8 if batch % 8 else 0), (0, 0)))
    ada = _ada(c_pad, ada_w, ada_b)[:, :batch]
    bias = _bias_table(rel_bias)
    w_in_t = jnp.swapaxes(w_in, 1, 2)
    w_gu_b = expert_w_gu
    ws_gu_b = shared_w_gu[:, None]

    x2 = x.reshape(t, d)
    for l in range(depth):
        sh_m, sc_m, gt_m, sh_f, sc_f, gt_f = [a.reshape(batch, 1, d) for a in jnp.split(ada[l], 6, axis=-1)]
        gains = norm_gains[l].reshape(4, 1, d)

        proj = _in_proj(_norm_mod_bf16(x2, gains[0], sh_m, sc_m, seq), w_in_t, l)

        r, k, v, a, g, lw = _rwkv_prep(proj, shift_mu[l][None, :], rwkv_w0[l][None, :], rwkv_w2[l],
                                       rwkv_a0[l][None, :], rwkv_a2[l], rwkv_g2[l], seq)
        y_r = _rwkv_chunk(r, k, v, a, g, lw, rwkv_k_k[l][None, :], rwkv_k_a[l][None, :],
                          rwkv_r_k[l].reshape(1, rw), rwkv_ln_w[l][None, :], rwkv_ln_b[l][None, :], batch, seq)
        y_a = _swa(proj, attn_sinks[l], bias, batch, seq, q_off, n_kv, q_per_kv)
        y_mix = _out_proj(y_r, y_a, w_out, l)
        x2 = _moe_block(x2, y_mix, gains[1], gt_m, gains[2], sh_f, sc_f, router_w[l].T, router_bias[l][:, None],
                        w_gu_b, expert_w_down, ws_gu_b, shared_w_down[:, None], l, gains[3], gt_f, seq)
    return x2.reshape(batch, seq, d)
```
